```python
import jax, jax.numpy as jnp
from jax import lax
import numpy as np

D_MODEL = 2048
BATCH = 8
SEQ = 2048
DEPTH = 2

D_MIX = D_MODEL
D_GMLP = D_MIX // 2
D_HGRN = D_MIX - D_GMLP
GMLP_HEAD = 128
N_GMLP_HEADS = D_GMLP // GMLP_HEAD
GMLP_CHUNK = 128
HGRN_HEAD = 128
N_HGRN_HEADS = D_HGRN // HGRN_HEAD
HGRN_CHUNK = 64
D_IN = 2 * D_GMLP + 4 * D_HGRN
D_FF = 5632
N_EXPERTS = 8
TOP_K = 2
N_DENSE = (DEPTH + 1) // 2
N_MOE = DEPTH // 2
EPS = 1e-6
F_MIN = 1e-6

kernel_name = 'hybrid_gmlp_hgrn2_moe_trunk'


def rmsnorm(x, w):
    xf = x.astype(jnp.float32)
    y = xf * lax.rsqrt(jnp.mean(xf * xf, axis=-1, keepdims=True) + EPS)
    return (y * w.astype(jnp.float32)).astype(x.dtype)


def head_rmsnorm(y, w):
    h, hd = y.shape[-2], y.shape[-1]
    y = y * lax.rsqrt(jnp.mean(y * y, axis=-1, keepdims=True) + EPS)
    return y * w.astype(jnp.float32).reshape(h, hd)


def gmlp_mixer(z_u, z_v, w_s, b_s, ln_w, ln_b, out_w):
    b, s, _ = z_u.shape
    n = s // GMLP_CHUNK
    shp = (b, n, GMLP_CHUNK, N_GMLP_HEADS, GMLP_HEAD)
    u = jax.nn.gelu(z_u.astype(jnp.float32)).reshape(shp)
    v = jax.nn.gelu(z_v.astype(jnp.float32)).reshape(shp)
    mu = jnp.mean(v, axis=-1, keepdims=True)
    var = jnp.mean(jnp.square(v - mu), axis=-1, keepdims=True)
    v = (v - mu) * lax.rsqrt(var + EPS) * ln_w.astype(jnp.float32).reshape(N_GMLP_HEADS, GMLP_HEAD) \
        + ln_b.astype(jnp.float32).reshape(N_GMLP_HEADS, GMLP_HEAD)
    causal = jnp.tril(jnp.ones((GMLP_CHUNK, GMLP_CHUNK), dtype=bool))
    w_c = jnp.where(causal[None], w_s.astype(jnp.float32), 0.0)
    sv = jnp.einsum('hts,bnshc->bnthc', w_c, v) \
        + b_s.astype(jnp.float32).T[None, None, :, :, None]
    y = (u * sv).reshape(b, s, N_GMLP_HEADS, GMLP_HEAD)
    y = head_rmsnorm(y, out_w)
    return y.reshape(b, s, D_GMLP).astype(z_u.dtype)


def hgrn2_mixer(z_q, z_f, z_i, z_g, lb, out_w):
    b, s, _ = z_q.shape
    n = s // HGRN_CHUNK
    h, d = N_HGRN_HEADS, HGRN_HEAD
    shp = (b, s, h, d)
    lb = lb.astype(jnp.float32).reshape(h, d)
    zf = z_f.astype(jnp.float32).reshape(shp)
    q = jax.nn.silu(z_q.astype(jnp.float32)).reshape(shp)
    sig = jax.nn.sigmoid(zf)
    f = lb + (1.0 - lb) * sig
    logf = jnp.log(jnp.clip(f, F_MIN, 1.0))
    k = (1.0 - lb) * (1.0 - sig)
    i = z_i.astype(jnp.float32).reshape(shp)

    def to_chunks(t):
        return t.reshape(b, n, HGRN_CHUNK, h, d).transpose(1, 0, 3, 2, 4)

    causal = jnp.tril(jnp.ones((HGRN_CHUNK, HGRN_CHUNK), dtype=bool))[:, :, None]

    def step(state, inp):
        qc, kc, ic, gc = inp
        a = jnp.cumsum(gc, axis=2)
        a_last = a[:, :, -1, :]
        diff = a[:, :, :, None, :] - a[:, :, None, :, :]
        decay = jnp.where(causal, jnp.exp(jnp.where(causal, diff, 0.0)), 0.0)
        scores = jnp.sum(qc[:, :, :, None, :] * kc[:, :, None, :, :] * decay, axis=-1)
        o = jnp.einsum('bhts,bhsv->bhtv', scores, ic) \
            + jnp.einsum('bhtd,bhdv->bhtv', qc * jnp.exp(a), state)
        state = jnp.exp(a_last)[..., None] * state \
            + jnp.einsum('bhsd,bhsv->bhdv', kc * jnp.exp(a_last[:, :, None, :] - a), ic)
        return state, o

    s0 = jnp.zeros((b, h, d, d), jnp.float32)
    _, o = lax.scan(step, s0, (to_chunks(q), to_chunks(k), to_chunks(i), to_chunks(logf)))
    o = o.transpose(1, 0, 3, 2, 4).reshape(shp)
    o = head_rmsnorm(o, out_w) * jax.nn.silu(z_g.astype(jnp.float32)).reshape(shp)
    return o.reshape(b, s, D_HGRN).astype(z_q.dtype)


def swiglu(h, w_gate, w_up, w_down):
    return (jax.nn.silu(h @ w_gate) * (h @ w_up)) @ w_down


def moe_ffn(h, router_w, w_gate, w_up, w_down):
    b, s, dm = h.shape
    ht = h.reshape(b * s, dm)
    logits = (ht @ router_w).astype(jnp.float32)
    top_v, top_i = lax.top_k(logits, TOP_K)
    top_w = jax.nn.softmax(top_v, axis=-1)
    gates = jnp.sum(jax.nn.one_hot(top_i, N_EXPERTS, dtype=jnp.float32) * top_w[..., None], axis=1)
    y = jnp.zeros((b * s, dm), jnp.float32)
    for e in range(N_EXPERTS):
        y = y + gates[:, e:e + 1] * swiglu(ht, w_gate[e], w_up[e], w_down[e]).astype(jnp.float32)
    return y.reshape(b, s, dm).astype(h.dtype)


def setup_inputs(seed: int = 0) -> dict:
    key = jax.random.key(seed)
    ks = jax.random.split(key, 24)
    f32 = jnp.float32

    def nrm(k, shape, scale):
        return jax.random.normal(k, shape, f32) * scale

    def gain(k, shape):
        return 1.0 + 0.02 * jax.random.normal(k, shape, f32)

    return {
        'x': jax.random.normal(ks[0], (BATCH, SEQ, D_MODEL), f32),
        'mix_norm_w': gain(ks[1], (DEPTH, D_MODEL)),
        'w_in': nrm(ks[2], (DEPTH, D_MODEL, D_IN), D_MODEL ** -0.5),
        'gmlp_w_s': nrm(ks[3], (DEPTH, N_GMLP_HEADS, GMLP_CHUNK, GMLP_CHUNK), GMLP_CHUNK ** -0.5),
        'gmlp_b_s': gain(ks[4], (DEPTH, N_GMLP_HEADS, GMLP_CHUNK)),
        'gmlp_ln_w': gain(ks[5], (DEPTH, D_GMLP)),
        'gmlp_ln_b': nrm(ks[6], (DEPTH, D_GMLP), 0.02),
        'gmlp_out_w': gain(ks[7], (DEPTH, D_GMLP)),
        'hgrn_lb_logits': nrm(ks[8], (DEPTH, D_HGRN), 0.5),
        'hgrn_out_w': gain(ks[9], (DEPTH, D_HGRN)),
        'w_o': nrm(ks[10], (DEPTH, D_MIX, D_MODEL), D_MIX ** -0.5),
        'ffn_norm_w': gain(ks[11], (DEPTH, D_MODEL)),
        'dense_w_gate': nrm(ks[12], (N_DENSE, D_MODEL, D_FF), D_MODEL ** -0.5),
        'dense_w_up': nrm(ks[13], (N_DENSE, D_MODEL, D_FF), D_MODEL ** -0.5),
        'dense_w_down': nrm(ks[14], (N_DENSE, D_FF, D_MODEL), D_FF ** -0.5),
        'router_w': nrm(ks[15], (N_MOE, D_MODEL, N_EXPERTS), D_MODEL ** -0.5),
        'expert_w_gate': nrm(ks[16], (N_MOE, N_EXPERTS, D_MODEL, D_FF), D_MODEL ** -0.5),
        'expert_w_up': nrm(ks[17], (N_MOE, N_EXPERTS, D_MODEL, D_FF), D_MODEL ** -0.5),
        'expert_w_down': nrm(ks[18], (N_MOE, N_EXPERTS, D_FF, D_MODEL), D_FF ** -0.5),
        'final_norm_w': gain(ks[19], (D_MODEL,)),
    }


def reference(x, mix_norm_w, w_in, gmlp_w_s, gmlp_b_s, gmlp_ln_w, gmlp_ln_b, gmlp_out_w,
              hgrn_lb_logits, hgrn_out_w, w_o, ffn_norm_w, dense_w_gate, dense_w_up, dense_w_down,
              router_w, expert_w_gate, expert_w_up, expert_w_down, final_norm_w):
    lbs = jax.nn.softmax(hgrn_lb_logits.astype(jnp.float32), axis=0)
    lbs = jnp.cumsum(lbs, axis=0) - lbs[0:1]
    o1 = D_GMLP
    o2 = 2 * D_GMLP
    for l in range(DEPTH):
        h = rmsnorm(x, mix_norm_w[l])
        z = h @ w_in[l]
        z_u, z_v = z[..., :o1], z[..., o1:o2]
        z_q = z[..., o2:o2 + D_HGRN]
        z_f = z[..., o2 + D_HGRN:o2 + 2 * D_HGRN]
        z_i = z[..., o2 + 2 * D_HGRN:o2 + 3 * D_HGRN]
        z_g = z[..., o2 + 3 * D_HGRN:]
        y_a = gmlp_mixer(z_u, z_v, gmlp_w_s[l], gmlp_b_s[l], gmlp_ln_w[l], gmlp_ln_b[l], gmlp_out_w[l])
        y_b = hgrn2_mixer(z_q, z_f, z_i, z_g, lbs[l], hgrn_out_w[l])
        x = x + jnp.concatenate([y_a, y_b], axis=-1) @ w_o[l]
        h = rmsnorm(x, ffn_norm_w[l])
        if l % 2 == 0:
            j = l // 2
            x = x + swiglu(h, dense_w_gate[j], dense_w_up[j], dense_w_down[j])
        else:
            j = l // 2
            x = x + moe_ffn(h, router_w[j], expert_w_gate[j], expert_w_up[j], expert_w_down[j])
    return rmsnorm(x, final_norm_w)
```

```python
import functools

import jax
import jax.numpy as jnp
from jax import lax
from jax.experimental import pallas as pl
from jax.experimental.pallas import tpu as pltpu

F32 = jnp.float32
BF16 = jnp.bfloat16

GMLP_HEAD = 128
GMLP_CHUNK = 128
HGRN_HEAD = 128
N_EXPERTS = 8
TOP_K = 2
EPS = 1e-6
F_MIN = 1e-6

LANES = 128
HGRN_C = 128
HGRN_DIAG = 8
HGRN_DIAG_REF = 3
VMEM_LIMIT = 56 * 1024 * 1024


def _cparams(sem, vmem=VMEM_LIMIT):
    return pltpu.CompilerParams(dimension_semantics=sem, vmem_limit_bytes=vmem)


def _sigmoid(x):
    return 1.0 / (1.0 + jnp.exp(-x))


def _silu(x):
    return x * _sigmoid(x)


def _rms(x, w):
    return x * lax.rsqrt(jnp.mean(x * x, axis=-1, keepdims=True) + EPS) * w


def _inproj_kernel(x_ref, nw_ref, w_ref, z_ref, h_scr):
    @pl.when(pl.program_id(1) == 0)
    def _():
        h_scr[...] = _rms(x_ref[...], nw_ref[...]).astype(BF16)

    z_ref[...] = jnp.dot(h_scr[...], w_ref[...], preferred_element_type=F32)


def in_proj(x, norm_w, w_bf16, bm=1024, bn=1024):
    t, d = x.shape
    n = w_bf16.shape[1]
    bm = min(bm, t)
    return pl.pallas_call(
        _inproj_kernel,
        grid=(t // bm, n // bn),
        in_specs=[
            pl.BlockSpec((bm, d), lambda i, j: (i, 0)),
            pl.BlockSpec((1, d), lambda i, j: (0, 0)),
            pl.BlockSpec((d, bn), lambda i, j: (0, j)),
        ],
        out_specs=pl.BlockSpec((bm, bn), lambda i, j: (i, j)),
        out_shape=jax.ShapeDtypeStruct((t, n), F32),
        scratch_shapes=[pltpu.VMEM((bm, d), BF16)],
        compiler_params=_cparams(("parallel", "arbitrary")),
        name="in_proj",
    )(x, norm_w.reshape(1, d), w_bf16)


def _gmlp_kernel(zu_ref, zv_ref, ws_ref, bs_ref, lnw_ref, lnb_ref, ow_ref, y_ref, *, n_heads, n_chunks):
    c = GMLP_CHUNK
    row = lax.broadcasted_iota(jnp.int32, (c, c), 0)
    col = lax.broadcasted_iota(jnp.int32, (c, c), 1)
    causal = row >= col

    def chunk(ci, carry):
        r0 = pl.multiple_of(ci * c, c)
        for h in range(n_heads):
            sl = slice(h * GMLP_HEAD, (h + 1) * GMLP_HEAD)
            u = jax.nn.gelu(zu_ref[pl.ds(r0, c), sl])
            v = jax.nn.gelu(zv_ref[pl.ds(r0, c), sl])
            mu = jnp.mean(v, axis=-1, keepdims=True)
            vc = v - mu
            var = jnp.mean(vc * vc, axis=-1, keepdims=True)
            vn = vc * lax.rsqrt(var + EPS) * lnw_ref[:, sl] + lnb_ref[:, sl]
            w = jnp.where(causal, ws_ref[h], 0.0).astype(BF16)
            sv = jnp.dot(w, vn.astype(BF16), preferred_element_type=F32) + bs_ref[:, h:h + 1]
            y = u * sv
            y_ref[pl.ds(r0, c), sl] = _rms(y, ow_ref[:, sl]).astype(BF16)
        return carry

    lax.fori_loop(0, n_chunks, chunk, 0)


def gmlp_mixer(z, w_s, b_s, ln_w, ln_b, out_w, rows=512):
    t = z.shape[0]
    n_heads = w_s.shape[0]
    dg = n_heads * GMLP_HEAD
    rows = min(rows, t)
    kern = functools.partial(_gmlp_kernel, n_heads=n_heads, n_chunks=rows // GMLP_CHUNK)
    vec = lambda a: a.reshape(1, dg)
    return pl.pallas_call(
        kern,
        grid=(t // rows,),
        in_specs=[
            pl.BlockSpec((rows, dg), lambda i: (i, 0)),
            pl.BlockSpec((rows, dg), lambda i: (i, 1)),
            pl.BlockSpec((n_heads, GMLP_CHUNK, GMLP_CHUNK), lambda i: (0, 0, 0)),
            pl.BlockSpec((GMLP_CHUNK, n_heads), lambda i: (0, 0)),
            pl.BlockSpec((1, dg), lambda i: (0, 0)),
            pl.BlockSpec((1, dg), lambda i: (0, 0)),
            pl.BlockSpec((1, dg), lambda i: (0, 0)),
        ],
        out_specs=pl.BlockSpec((rows, dg), lambda i: (i, 0)),
        out_shape=jax.ShapeDtypeStruct((t, dg), BF16),
        compiler_params=_cparams(("parallel",)),
        name="gmlp_mixer",
    )(z, z, w_s, b_s.T, vec(ln_w), vec(ln_b), vec(out_w))


def _hgrn_level_terms(a, q, k, blk):
    c, d = a.shape
    half = blk // 2
    a3 = a.reshape(c // blk, blk, d)
    ref = a3[:, half - 1:half, :]
    upper = lax.broadcasted_iota(jnp.int32, (c // blk, blk, d), 1) >= half
    e = jnp.exp(jnp.where(upper, a3 - ref, ref - a3)).reshape(c, d)
    upper = upper.reshape(c, d)
    return jnp.where(upper, q * e, 0.0).astype(BF16), jnp.where(upper, 0.0, k * e).astype(BF16)


def _hgrn_kernel(zq_ref, zf_ref, zi_ref, zg_ref, lb_ref, ow_ref, y_ref, st_ref, *, n_heads, n_chunks):
    c, d = HGRN_C, HGRN_HEAD
    nt = (((1,), (1,)), ((), ()))
    tn = (((0,), (0,)), ((), ()))

    @pl.when(pl.program_id(1) == 0)
    def _():
        st_ref[...] = jnp.zeros_like(st_ref)

    row = lax.broadcasted_iota(jnp.int32, (c, c), 0)
    col = lax.broadcasted_iota(jnp.int32, (c, c), 1)
    tril = (row >= col).astype(BF16)
    diag_mask = (row >= col) & ((row // HGRN_DIAG) == (col // HGRN_DIAG))
    level_masks = []
    blk = 2 * HGRN_DIAG
    while blk <= c:
        level_masks.append((blk, None if blk == c else (row // blk) == (col // blk)))
        blk *= 2

    def chunk(ci, carry):
        r0 = pl.multiple_of(ci * c, c)
        for h in range(n_heads):
            sl = slice(h * d, (h + 1) * d)
            lb = lb_ref[:, sl]
            sig = _sigmoid(zf_ref[pl.ds(r0, c), sl])
            q = _silu(zq_ref[pl.ds(r0, c), sl])
            f = lb + (1.0 - lb) * sig
            logf = jnp.log(jnp.clip(f, F_MIN, 1.0))
            k = (1.0 - lb) * (1.0 - sig)
            iv = zi_ref[pl.ds(r0, c), sl].astype(BF16)

            hi = logf.astype(BF16)
            lo = (logf - hi.astype(F32)).astype(BF16)
            cs = jnp.dot(tril, jnp.concatenate([hi, lo], axis=1), preferred_element_type=F32)
            a = cs[:, :d] + cs[:, d:]
            a_last = a[c - 1:c, :]

            a3 = a.reshape(c // HGRN_DIAG, HGRN_DIAG, d)
            dd = (a3 - a3[:, HGRN_DIAG_REF:HGRN_DIAG_REF + 1, :]).reshape(c, d)
            sd = lax.dot_general((q * jnp.exp(dd)).astype(BF16), (k * jnp.exp(-dd)).astype(BF16), nt,
                                 preferred_element_type=F32)
            scores = jnp.where(diag_mask, sd, 0.0)

            for blk, same_block in level_masks:
                qh, kh = _hgrn_level_terms(a, q, k, blk)
                sl_ = lax.dot_general(qh, kh, nt, preferred_element_type=F32)
                scores = scores + (sl_ if same_block is None else jnp.where(same_block, sl_, 0.0))

            st = st_ref[h]
            o = jnp.dot(scores.astype(BF16), iv, preferred_element_type=F32) \
                + lax.dot_general((q * jnp.exp(a)).astype(BF16), st.astype(BF16), nt,
                                  preferred_element_type=F32)
            kd = (k * jnp.exp(a_last - a)).astype(BF16)
            st_ref[h] = jnp.exp(a_last) * st + lax.dot_general(iv, kd, tn, preferred_element_type=F32)

            o = _rms(o, ow_ref[:, sl]) * _silu(zg_ref[pl.ds(r0, c), sl])
            y_ref[pl.ds(r0, c), sl] = o.astype(BF16)
        return carry

    lax.fori_loop(0, n_chunks, chunk, 0)


def hgrn2_mixer(z, lb, out_w, batch, col0, rows=512):
    t = z.shape[0]
    s = t // batch
    dh = lb.shape[0]
    n_heads = dh // HGRN_HEAD
    rows = min(rows, s)
    spb = s // rows
    kern = functools.partial(_hgrn_kernel, n_heads=n_heads, n_chunks=rows // HGRN_C)
    zspec = lambda j: pl.BlockSpec((rows, dh), lambda b, i, j=j: (b * spb + i, col0 + j))
    return pl.pallas_call(
        kern,
        grid=(batch, spb),
        in_specs=[zspec(0), zspec(1), zspec(2), zspec(3),
                  pl.BlockSpec((1, dh), lambda b, i: (0, 0)),
                  pl.BlockSpec((1, dh), lambda b, i: (0, 0))],
        out_specs=pl.BlockSpec((rows, dh), lambda b, i: (b * spb + i, 0)),
        out_shape=jax.ShapeDtypeStruct((t, dh), BF16),
        scratch_shapes=[pltpu.VMEM((n_heads, HGRN_HEAD, HGRN_HEAD), F32)],
        compiler_params=_cparams(("parallel", "arbitrary")),
        name="hgrn2_mixer",
    )(z, z, z, z, lb.reshape(1, dh), out_w.reshape(1, dh))


def _outproj_kernel(x_ref, ya_ref, yb_ref, wa_ref, wb_ref, nw_ref, xo_ref, h_ref):
    acc = x_ref[...] + jnp.dot(ya_ref[...], wa_ref[...], preferred_element_type=F32) \
        + jnp.dot(yb_ref[...], wb_ref[...], preferred_element_type=F32)
    xo_ref[...] = acc
    h_ref[...] = _rms(acc, nw_ref[...])


def out_proj(x, y_a, y_b, w_o_bf16, norm_w, bm=256):
    t, d = x.shape
    da, db = y_a.shape[1], y_b.shape[1]
    bm = min(bm, t)
    return pl.pallas_call(
        _outproj_kernel,
        grid=(t // bm,),
        in_specs=[
            pl.BlockSpec((bm, d), lambda i: (i, 0)),
            pl.BlockSpec((bm, da), lambda i: (i, 0)),
            pl.BlockSpec((bm, db), lambda i: (i, 0)),
            pl.BlockSpec((da, d), lambda i: (0, 0)),
            pl.BlockSpec((db, d), lambda i: (da // db, 0)),
            pl.BlockSpec((1, d), lambda i: (0, 0)),
        ],
        out_specs=[pl.BlockSpec((bm, d), lambda i: (i, 0)), pl.BlockSpec((bm, d), lambda i: (i, 0))],
        out_shape=[jax.ShapeDtypeStruct((t, d), F32), jax.ShapeDtypeStruct((t, d), F32)],
        compiler_params=_cparams(("parallel",)),
        name="out_proj",
    )(x, y_a, y_b, w_o_bf16, w_o_bf16, norm_w.reshape(1, d))


def _ffn_kernel(x_ref, h_ref, wg_ref, wu_ref, wd_ref, o_ref, hb_scr):
    @pl.when(pl.program_id(1) == 0)
    def _():
        hb_scr[...] = h_ref[...].astype(BF16)
        o_ref[...] = x_ref[...]

    hb = hb_scr[...]
    g = jnp.dot(hb, wg_ref[...], preferred_element_type=F32)
    u = jnp.dot(hb, wu_ref[...], preferred_element_type=F32)
    act = (_silu(g) * u).astype(BF16)
    o_ref[...] += jnp.dot(act, wd_ref[...], preferred_element_type=F32)


def dense_ffn(x, h, wg, wu, wd, bm=512, bf=512):
    t, d = x.shape
    f = wg.shape[1]
    bm = min(bm, t)
    return pl.pallas_call(
        _ffn_kernel,
        grid=(t // bm, f // bf),
        in_specs=[
            pl.BlockSpec((bm, d), lambda i, j: (i, 0)),
            pl.BlockSpec((bm, d), lambda i, j: (i, 0)),
            pl.BlockSpec((d, bf), lambda i, j: (0, j)),
            pl.BlockSpec((d, bf), lambda i, j: (0, j)),
            pl.BlockSpec((bf, d), lambda i, j: (j, 0)),
        ],
        out_specs=pl.BlockSpec((bm, d), lambda i, j: (i, 0)),
        out_shape=jax.ShapeDtypeStruct((t, d), F32),
        scratch_shapes=[pltpu.VMEM((bm, d), BF16)],
        compiler_params=_cparams(("parallel", "arbitrary")),
        name="dense_ffn",
    )(x, h, wg, wu, wd)


ROUTE_E0, ROUTE_E1, ROUTE_W0, ROUTE_W1, ROUTE_R0, ROUTE_R1 = range(6)


def _router_kernel(h_ref, rw_ref, route_ref, cnt_ref, carry_scr):
    bm = h_ref.shape[0]

    @pl.when(pl.program_id(0) == 0)
    def _():
        carry_scr[...] = jnp.zeros_like(carry_scr)

    logits = jnp.dot(h_ref[...], rw_ref[...], preferred_element_type=F32, precision=lax.Precision.HIGHEST)
    lane = lax.broadcasted_iota(jnp.int32, (bm, LANES), 1)
    neg = jnp.float32(-jnp.inf)
    logits = jnp.where(lane < N_EXPERTS, logits, neg)
    v0 = jnp.max(logits, axis=-1, keepdims=True)
    e0 = jnp.min(jnp.where(logits == v0, lane, LANES), axis=-1, keepdims=True)
    rest = jnp.where(lane == e0, neg, logits)
    v1 = jnp.max(rest, axis=-1, keepdims=True)
    e1 = jnp.min(jnp.where(rest == v1, lane, LANES), axis=-1, keepdims=True)
    ex = jnp.exp(v1 - v0)
    w0 = 1.0 / (1.0 + ex)
    w1 = ex / (1.0 + ex)

    onehot = ((lane == e0) | (lane == e1)).astype(F32)
    row = lax.broadcasted_iota(jnp.int32, (bm, bm), 0)
    col = lax.broadcasted_iota(jnp.int32, (bm, bm), 1)
    before = (row > col).astype(BF16)
    excl = jnp.dot(before, onehot.astype(BF16), preferred_element_type=F32) + carry_scr[...]
    r0 = jnp.sum(jnp.where(lane == e0, excl, 0.0), axis=-1, keepdims=True)
    r1 = jnp.sum(jnp.where(lane == e1, excl, 0.0), axis=-1, keepdims=True)
    carry_scr[...] += jnp.sum(onehot, axis=0, keepdims=True)
    cnt_ref[...] = carry_scr[...]

    out = jnp.zeros((bm, LANES), F32)
    for idx, val in ((ROUTE_E0, e0.astype(F32)), (ROUTE_E1, e1.astype(F32)), (ROUTE_W0, w0), (ROUTE_W1, w1),
                     (ROUTE_R0, r0), (ROUTE_R1, r1)):
        out = jnp.where(lane == idx, val, out)
    route_ref[...] = out


def moe_router(h, router_w, bm=512):
    t, d = h.shape
    bm = min(bm, t)
    rw = jnp.zeros((d, LANES), F32).at[:, :N_EXPERTS].set(router_w)
    return pl.pallas_call(
        _router_kernel,
        grid=(t // bm,),
        in_specs=[pl.BlockSpec((bm, d), lambda i: (i, 0)), pl.BlockSpec((d, LANES), lambda i: (0, 0))],
        out_specs=[pl.BlockSpec((bm, LANES), lambda i: (i, 0)), pl.BlockSpec((1, LANES), lambda i: (0, 0))],
        out_shape=[jax.ShapeDtypeStruct((t, LANES), F32), jax.ShapeDtypeStruct((1, LANES), F32)],
        scratch_shapes=[pltpu.VMEM((1, LANES), F32)],
        compiler_params=_cparams(("arbitrary",)),
        name="moe_router",
    )(h, rw)


def _row_copy(src_ref, dst_ref, sem, src_row, dst_row):
    return pltpu.make_async_copy(src_ref.at[pl.ds(src_row, 1)], dst_ref.at[pl.ds(dst_row, 1)], sem)


def _dispatch_kernel(pos_ref, h_ref, xs_in_ref, xs_ref, sem):
    del xs_in_ref
    bt = h_ref.shape[0]

    def start(r, carry):
        for kk in range(TOP_K):
            _row_copy(h_ref, xs_ref, sem, r, pos_ref[0, TOP_K * r + kk]).start()
        return carry

    def wait(r, carry):
        for kk in range(TOP_K):
            _row_copy(h_ref, xs_ref, sem, r, pos_ref[0, TOP_K * r + kk]).wait()
        return carry

    lax.fori_loop(0, bt, start, 0)
    lax.fori_loop(0, bt, wait, 0)


def moe_dispatch(h, pos, n_rows, bt=256):
    t, d = h.shape
    bt = min(bt, t)
    pos2 = pos.reshape(t // bt, 1, TOP_K * bt)
    return pl.pallas_call(
        _dispatch_kernel,
        grid=(t // bt,),
        in_specs=[
            pl.BlockSpec((None, 1, TOP_K * bt), lambda i: (i, 0, 0), memory_space=pltpu.SMEM),
            pl.BlockSpec((bt, d), lambda i: (i, 0)),
            pl.BlockSpec(memory_space=pl.ANY),
        ],
        out_specs=pl.BlockSpec(memory_space=pl.ANY),
        out_shape=jax.ShapeDtypeStruct((n_rows, d), F32),
        scratch_shapes=[pltpu.SemaphoreType.DMA(())],
        input_output_aliases={2: 0},
        compiler_params=_cparams(("arbitrary",)),
        name="moe_dispatch",
    )(pos2, h, jnp.zeros((n_rows, d), F32))


def _gmm_kernel(te_ref, nv_ref, xs_ref, wg_ref, wu_ref, wd_ref, ys_ref, xb_scr):
    i, j = pl.program_id(0), pl.program_id(1)

    @pl.when(i < nv_ref[0])
    def _():
        @pl.when(j == 0)
        def _():
            xb_scr[...] = xs_ref[...].astype(BF16)
            ys_ref[...] = jnp.zeros_like(ys_ref)

        xb = xb_scr[...]
        g = jnp.dot(xb, wg_ref[...], preferred_element_type=F32)
        u = jnp.dot(xb, wu_ref[...], preferred_element_type=F32)
        act = (_silu(g) * u).astype(BF16)
        ys_ref[...] += jnp.dot(act, wd_ref[...], preferred_element_type=F32)

    @pl.when((i >= nv_ref[0]) & (j == 0))
    def _():
        ys_ref[...] = jnp.zeros_like(ys_ref)


def moe_grouped_ffn(xs, tile_expert, n_valid, wg, wu, wd, bm, bf=512):
    p, d = xs.shape
    f = wg.shape[2]
    nf = f // bf

    def row_map(i, j, te, nv):
        return (jnp.minimum(i, nv[0] - 1), 0)

    def fcol(i, j, nv):
        return jnp.where(i < nv[0], j, nf - 1)

    return pl.pallas_call(
        _gmm_kernel,
        grid_spec=pltpu.PrefetchScalarGridSpec(
            num_scalar_prefetch=2,
            grid=(p // bm, nf),
            in_specs=[
                pl.BlockSpec((bm, d), row_map),
                pl.BlockSpec((None, d, bf), lambda i, j, te, nv: (te[i], 0, fcol(i, j, nv))),
                pl.BlockSpec((None, d, bf), lambda i, j, te, nv: (te[i], 0, fcol(i, j, nv))),
                pl.BlockSpec((None, bf, d), lambda i, j, te, nv: (te[i], fcol(i, j, nv), 0)),
            ],
            out_specs=pl.BlockSpec((bm, d), lambda i, j, te, nv: (i, 0)),
            scratch_shapes=[pltpu.VMEM((bm, d), BF16)],
        ),
        out_shape=jax.ShapeDtypeStruct((p, d), F32),
        compiler_params=_cparams(("arbitrary", "arbitrary")),
        name="moe_grouped_ffn",
    )(tile_expert, n_valid, xs, wg, wu, wd)


def _combine_kernel(pos_ref, x_ref, route_ref, nw_ref, ys_ref, o_ref, y_scr, sem, *, final_norm):
    bt = x_ref.shape[0]

    def start(r, carry):
        for kk in range(TOP_K):
            _row_copy(ys_ref, y_scr.at[kk], sem, pos_ref[0, TOP_K * r + kk], r).start()
        return carry

    def wait(r, carry):
        for kk in range(TOP_K):
            _row_copy(ys_ref, y_scr.at[kk], sem, pos_ref[0, TOP_K * r + kk], r).wait()
        return carry

    lax.fori_loop(0, bt, start, 0)
    lax.fori_loop(0, bt, wait, 0)
    w0 = route_ref[:, ROUTE_W0:ROUTE_W0 + 1]
    w1 = route_ref[:, ROUTE_W1:ROUTE_W1 + 1]
    out = x_ref[...] + (w0 * y_scr[0] + w1 * y_scr[1])
    if final_norm:
        out = _rms(out, nw_ref[...])
    o_ref[...] = out


def moe_combine(x, route, pos, ys, norm_w, final_norm, bt=256):
    t, d = x.shape
    bt = min(bt, t)
    pos2 = pos.reshape(t // bt, 1, TOP_K * bt)
    return pl.pallas_call(
        functools.partial(_combine_kernel, final_norm=final_norm),
        grid=(t // bt,),
        in_specs=[
            pl.BlockSpec((None, 1, TOP_K * bt), lambda i: (i, 0, 0), memory_space=pltpu.SMEM),
            pl.BlockSpec((bt, d), lambda i: (i, 0)),
            pl.BlockSpec((bt, LANES), lambda i: (i, 0)),
            pl.BlockSpec((1, d), lambda i: (0, 0)),
            pl.BlockSpec(memory_space=pl.ANY),
        ],
        out_specs=pl.BlockSpec((bt, d), lambda i: (i, 0)),
        out_shape=jax.ShapeDtypeStruct((t, d), F32),
        scratch_shapes=[pltpu.VMEM((TOP_K, bt, d), F32), pltpu.SemaphoreType.DMA(())],
        compiler_params=_cparams(("arbitrary",)),
        name="moe_combine",
    )(pos2, x, route, norm_w.reshape(1, d), ys)


def moe_ffn(x, h, router_w, wg, wu, wd, norm_w, final_norm, bm=512):
    t, d = x.shape
    route, counts = moe_router(h, router_w)

    cnt = counts[0, :N_EXPERTS].astype(jnp.int32)
    tiles = (cnt + bm - 1) // bm
    tile_end = jnp.cumsum(tiles)
    offset = (tile_end - tiles) * bm
    n_tiles = (TOP_K * t) // bm + N_EXPERTS
    tile_expert = jnp.minimum(jnp.searchsorted(tile_end, jnp.arange(n_tiles, dtype=jnp.int32), side="right"),
                              N_EXPERTS - 1).astype(jnp.int32)
    n_valid = tile_end[-1:].astype(jnp.int32)
    last_expert = tile_expert[n_valid[0] - 1]
    tile_expert = jnp.where(jnp.arange(n_tiles) < n_valid[0], tile_expert, last_expert)
    experts = route[:, ROUTE_E0:ROUTE_E1 + 1].astype(jnp.int32)
    ranks = route[:, ROUTE_R0:ROUTE_R1 + 1].astype(jnp.int32)
    pos = offset[experts] + ranks

    xs = moe_dispatch(h, pos, n_tiles * bm)
    ys = moe_grouped_ffn(xs, tile_expert, n_valid, wg, wu, wd, bm)
    return moe_combine(x, route, pos, ys, norm_w, final_norm)


def _norm_kernel(x_ref, nw_ref, o_ref):
    o_ref[...] = _rms(x_ref[...], nw_ref[...])


def final_norm(x, norm_w, bm=512):
    t, d = x.shape
    bm = min(bm, t)
    return pl.pallas_call(
        _norm_kernel,
        grid=(t // bm,),
        in_specs=[pl.BlockSpec((bm, d), lambda i: (i, 0)), pl.BlockSpec((1, d), lambda i: (0, 0))],
        out_specs=pl.BlockSpec((bm, d), lambda i: (i, 0)),
        out_shape=jax.ShapeDtypeStruct((t, d), F32),
        compiler_params=_cparams(("parallel",)),
        name="final_norm",
    )(x, norm_w.reshape(1, d))


def kernel(x, mix_norm_w, w_in, gmlp_w_s, gmlp_b_s, gmlp_ln_w, gmlp_ln_b, gmlp_out_w, hgrn_lb_logits, hgrn_out_w, w_o, ffn_norm_w, dense_w_gate, dense_w_up, dense_w_down, router_w, expert_w_gate, expert_w_up, expert_w_down, final_norm_w):
    batch, seq, d = x.shape
    depth = w_in.shape[0]
    d_gmlp = gmlp_ln_w.shape[1]
    d_hgrn = hgrn_out_w.shape[1]
    assert d_gmlp == d_hgrn and seq % HGRN_C == 0 and seq % GMLP_CHUNK == 0

    lbs = jax.nn.softmax(hgrn_lb_logits.astype(F32), axis=0)
    lbs = jnp.cumsum(lbs, axis=0) - lbs[0:1]

    xt = x.reshape(batch * seq, d)
    for l in range(depth):
        z = in_proj(xt, mix_norm_w[l], w_in[l].astype(BF16))
        y_a = gmlp_mixer(z, gmlp_w_s[l], gmlp_b_s[l], gmlp_ln_w[l], gmlp_ln_b[l], gmlp_out_w[l])
        y_b = hgrn2_mixer(z, lbs[l], hgrn_out_w[l], batch, col0=2 * d_gmlp // d_hgrn)
        xt, h = out_proj(xt, y_a, y_b, w_o[l].astype(BF16), ffn_norm_w[l])
        last = l == depth - 1
        j = l // 2
        if l % 2 == 0:
            xt = dense_ffn(xt, h, dense_w_gate[j].astype(BF16), dense_w_up[j].astype(BF16),
                           dense_w_down[j].astype(BF16))
            if last:
                xt = final_norm(xt, final_norm_w)
        else:
            xt = moe_ffn(xt, h, router_w[j], expert_w_gate[j].astype(BF16), expert_w_up[j].astype(BF16),
                         expert_w_down[j].astype(BF16), final_norm_w, final_norm=last)
    return xt.reshape(batch, seq, d)
```

```python
import functools

import jax
import jax.numpy as jnp
from jax import lax
from jax.experimental import pallas as pl
from jax.experimental.pallas import tpu as pltpu

F32 = jnp.float32
BF16 = jnp.bfloat16

GMLP_HEAD = 128
GMLP_CHUNK = 128
HGRN_HEAD = 128
N_EXPERTS = 8
TOP_K = 2
EPS = 1e-6
F_MIN = 1e-6

LANES = 128
HGRN_C = 128
HGRN_DIAG = 8
HGRN_DIAG_REF = 3
VMEM_LIMIT = 56 * 1024 * 1024


def _cparams(sem, vmem=VMEM_LIMIT):
    return pltpu.CompilerParams(dimension_semantics=sem, vmem_limit_bytes=vmem)


def _sigmoid(x):
    return 1.0 / (1.0 + jnp.exp(-x))


def _silu(x):
    return x * _sigmoid(x)


def _rms(x, w):
    return x * lax.rsqrt(jnp.mean(x * x, axis=-1, keepdims=True) + EPS) * w


def _inproj_kernel(x_ref, nw_ref, w_ref, z_ref, h_scr):
    @pl.when(pl.program_id(1) == 0)
    def _():
        h_scr[...] = _rms(x_ref[...], nw_ref[...]).astype(BF16)

    z_ref[...] = jnp.dot(h_scr[...], w_ref[...].astype(BF16), preferred_element_type=F32)


def in_proj(x, norm_w, w, bm=1024, bn=1024):
    t, d = x.shape
    n = w.shape[1]
    bm = min(bm, t)
    return pl.pallas_call(
        _inproj_kernel,
        grid=(t // bm, n // bn),
        in_specs=[
            pl.BlockSpec((bm, d), lambda i, j: (i, 0)),
            pl.BlockSpec((1, d), lambda i, j: (0, 0)),
            pl.BlockSpec((d, bn), lambda i, j: (0, j)),
        ],
        out_specs=pl.BlockSpec((bm, bn), lambda i, j: (i, j)),
        out_shape=jax.ShapeDtypeStruct((t, n), F32),
        scratch_shapes=[pltpu.VMEM((bm, d), BF16)],
        compiler_params=_cparams(("parallel", "arbitrary")),
        name="in_proj",
    )(x, norm_w.reshape(1, d), w)


def _gmlp_kernel(zu_ref, zv_ref, ws_ref, bs_ref, lnw_ref, lnb_ref, ow_ref, y_ref, *, n_heads, n_chunks):
    c = GMLP_CHUNK
    row = lax.broadcasted_iota(jnp.int32, (c, c), 0)
    col = lax.broadcasted_iota(jnp.int32, (c, c), 1)
    causal = row >= col

    def chunk(ci, carry):
        r0 = pl.multiple_of(ci * c, c)
        for h in range(n_heads):
            sl = slice(h * GMLP_HEAD, (h + 1) * GMLP_HEAD)
            u = jax.nn.gelu(zu_ref[pl.ds(r0, c), sl])
            v = jax.nn.gelu(zv_ref[pl.ds(r0, c), sl])
            mu = jnp.mean(v, axis=-1, keepdims=True)
            vc = v - mu
            var = jnp.mean(vc * vc, axis=-1, keepdims=True)
            vn = vc * lax.rsqrt(var + EPS) * lnw_ref[:, sl] + lnb_ref[:, sl]
            w = jnp.where(causal, ws_ref[h], 0.0).astype(BF16)
            sv = jnp.dot(w, vn.astype(BF16), preferred_element_type=F32) + bs_ref[:, h:h + 1]
            y = u * sv
            y_ref[pl.ds(r0, c), sl] = _rms(y, ow_ref[:, sl]).astype(BF16)
        return carry

    lax.fori_loop(0, n_chunks, chunk, 0)


def gmlp_mixer(z, w_s, b_s, ln_w, ln_b, out_w, rows=512):
    t = z.shape[0]
    n_heads = w_s.shape[0]
    dg = n_heads * GMLP_HEAD
    rows = min(rows, t)
    kern = functools.partial(_gmlp_kernel, n_heads=n_heads, n_chunks=rows // GMLP_CHUNK)
    vec = lambda a: a.reshape(1, dg)
    return pl.pallas_call(
        kern,
        grid=(t // rows,),
        in_specs=[
            pl.BlockSpec((rows, dg), lambda i: (i, 0)),
            pl.BlockSpec((rows, dg), lambda i: (i, 1)),
            pl.BlockSpec((n_heads, GMLP_CHUNK, GMLP_CHUNK), lambda i: (0, 0, 0)),
            pl.BlockSpec((GMLP_CHUNK, n_heads), lambda i: (0, 0)),
            pl.BlockSpec((1, dg), lambda i: (0, 0)),
            pl.BlockSpec((1, dg), lambda i: (0, 0)),
            pl.BlockSpec((1, dg), lambda i: (0, 0)),
        ],
        out_specs=pl.BlockSpec((rows, dg), lambda i: (i, 0)),
        out_shape=jax.ShapeDtypeStruct((t, dg), BF16),
        compiler_params=_cparams(("parallel",)),
        name="gmlp_mixer",
    )(z, z, w_s, b_s.T, vec(ln_w), vec(ln_b), vec(out_w))


def _hgrn_level_terms(a, q, k, blk):
    c, d = a.shape
    half = blk // 2
    a3 = a.reshape(c // blk, blk, d)
    ref = a3[:, half - 1:half, :]
    upper = lax.broadcasted_iota(jnp.int32, (c // blk, blk, d), 1) >= half
    e = jnp.exp(jnp.where(upper, a3 - ref, ref - a3)).reshape(c, d)
    upper = upper.reshape(c, d)
    return jnp.where(upper, q * e, 0.0).astype(BF16), jnp.where(upper, 0.0, k * e).astype(BF16)


def _hgrn_kernel(zq_ref, zf_ref, zi_ref, zg_ref, lb_ref, ow_ref, y_ref, st_ref, *, n_heads, n_chunks):
    c, d = HGRN_C, HGRN_HEAD
    nt = (((1,), (1,)), ((), ()))
    tn = (((0,), (0,)), ((), ()))

    @pl.when(pl.program_id(1) == 0)
    def _():
        st_ref[...] = jnp.zeros_like(st_ref)

    row = lax.broadcasted_iota(jnp.int32, (c, c), 0)
    col = lax.broadcasted_iota(jnp.int32, (c, c), 1)
    tril = (row >= col).astype(BF16)
    diag_mask = (row >= col) & ((row // HGRN_DIAG) == (col // HGRN_DIAG))
    level_masks = []
    blk = 2 * HGRN_DIAG
    while blk <= c:
        level_masks.append((blk, None if blk == c else (row // blk) == (col // blk)))
        blk *= 2

    def chunk(ci, carry):
        r0 = pl.multiple_of(ci * c, c)
        for h in range(n_heads):
            sl = slice(h * d, (h + 1) * d)
            lb = lb_ref[:, sl]
            sig = _sigmoid(zf_ref[pl.ds(r0, c), sl])
            q = _silu(zq_ref[pl.ds(r0, c), sl])
            f = lb + (1.0 - lb) * sig
            logf = jnp.log(jnp.clip(f, F_MIN, 1.0))
            k = (1.0 - lb) * (1.0 - sig)
            iv = zi_ref[pl.ds(r0, c), sl].astype(BF16)

            hi = logf.astype(BF16)
            lo = (logf - hi.astype(F32)).astype(BF16)
            cs = jnp.dot(tril, jnp.concatenate([hi, lo], axis=1), preferred_element_type=F32)
            a = cs[:, :d] + cs[:, d:]
            a_last = a[c - 1:c, :]

            a3 = a.reshape(c // HGRN_DIAG, HGRN_DIAG, d)
            dd = (a3 - a3[:, HGRN_DIAG_REF:HGRN_DIAG_REF + 1, :]).reshape(c, d)
            sd = lax.dot_general((q * jnp.exp(dd)).astype(BF16), (k * jnp.exp(-dd)).astype(BF16), nt,
                                 preferred_element_type=F32)
            scores = jnp.where(diag_mask, sd, 0.0)

            for blk, same_block in level_masks:
                qh, kh = _hgrn_level_terms(a, q, k, blk)
                sl_ = lax.dot_general(qh, kh, nt, preferred_element_type=F32)
                scores = scores + (sl_ if same_block is None else jnp.where(same_block, sl_, 0.0))

            st = st_ref[h]
            o = jnp.dot(scores.astype(BF16), iv, preferred_element_type=F32) \
                + lax.dot_general((q * jnp.exp(a)).astype(BF16), st.astype(BF16), nt,
                                  preferred_element_type=F32)
            kd = (k * jnp.exp(a_last - a)).astype(BF16)
            st_ref[h] = jnp.exp(a_last) * st + lax.dot_general(iv, kd, tn, preferred_element_type=F32)

            o = _rms(o, ow_ref[:, sl]) * _silu(zg_ref[pl.ds(r0, c), sl])
            y_ref[pl.ds(r0, c), sl] = o.astype(BF16)
        return carry

    lax.fori_loop(0, n_chunks, chunk, 0)


def hgrn2_mixer(z, lb, out_w, batch, col0, rows=512):
    t = z.shape[0]
    s = t // batch
    dh = lb.shape[0]
    n_heads = dh // HGRN_HEAD
    rows = min(rows, s)
    spb = s // rows
    kern = functools.partial(_hgrn_kernel, n_heads=n_heads, n_chunks=rows // HGRN_C)
    zspec = lambda j: pl.BlockSpec((rows, dh), lambda b, i, j=j: (b * spb + i, col0 + j))
    return pl.pallas_call(
        kern,
        grid=(batch, spb),
        in_specs=[zspec(0), zspec(1), zspec(2), zspec(3),
                  pl.BlockSpec((1, dh), lambda b, i: (0, 0)),
                  pl.BlockSpec((1, dh), lambda b, i: (0, 0))],
        out_specs=pl.BlockSpec((rows, dh), lambda b, i: (b * spb + i, 0)),
        out_shape=jax.ShapeDtypeStruct((t, dh), BF16),
        scratch_shapes=[pltpu.VMEM((n_heads, HGRN_HEAD, HGRN_HEAD), F32)],
        compiler_params=_cparams(("parallel", "arbitrary")),
        name="hgrn2_mixer",
    )(z, z, z, z, lb.reshape(1, dh), out_w.reshape(1, dh))


ROUTE_E0, ROUTE_E1, ROUTE_W0, ROUTE_W1, ROUTE_R0, ROUTE_R1 = range(6)


def _split_bf16(v):
    hi = v.astype(BF16)
    return hi, (v - hi.astype(F32)).astype(BF16)


def _route_rows(h, rw_hi, rw_lo, carry_ref):
    bm = h.shape[0]
    h_hi, h_lo = _split_bf16(h)
    logits = jnp.dot(h_hi, rw_hi, preferred_element_type=F32) + jnp.dot(h_lo, rw_hi, preferred_element_type=F32) \
        + jnp.dot(h_hi, rw_lo, preferred_element_type=F32)
    lane = lax.broadcasted_iota(jnp.int32, (bm, LANES), 1)
    neg = jnp.float32(-jnp.inf)
    logits = jnp.where(lane < N_EXPERTS, logits, neg)
    v0 = jnp.max(logits, axis=-1, keepdims=True)
    e0 = jnp.min(jnp.where(logits == v0, lane, LANES), axis=-1, keepdims=True)
    rest = jnp.where(lane == e0, neg, logits)
    v1 = jnp.max(rest, axis=-1, keepdims=True)
    e1 = jnp.min(jnp.where(rest == v1, lane, LANES), axis=-1, keepdims=True)
    ex = jnp.exp(v1 - v0)
    w0 = 1.0 / (1.0 + ex)
    w1 = ex / (1.0 + ex)

    onehot = ((lane == e0) | (lane == e1)).astype(F32)
    row = lax.broadcasted_iota(jnp.int32, (bm, bm), 0)
    col = lax.broadcasted_iota(jnp.int32, (bm, bm), 1)
    before = (row > col).astype(BF16)
    excl = jnp.dot(before, onehot.astype(BF16), preferred_element_type=F32) + carry_ref[...]
    r0 = jnp.sum(jnp.where(lane == e0, excl, 0.0), axis=-1, keepdims=True)
    r1 = jnp.sum(jnp.where(lane == e1, excl, 0.0), axis=-1, keepdims=True)
    carry_ref[...] += jnp.sum(onehot, axis=0, keepdims=True)

    out = jnp.zeros((bm, LANES), F32)
    for idx, val in ((ROUTE_E0, e0.astype(F32)), (ROUTE_E1, e1.astype(F32)), (ROUTE_W0, w0), (ROUTE_W1, w1),
                     (ROUTE_R0, r0), (ROUTE_R1, r1)):
        out = jnp.where(lane == idx, val, out)
    return out


def _outproj_kernel(*refs, da, moe):
    if moe:
        (x_ref, ya_ref, yb_ref, wo_ref, nw_ref, rw_ref, xo_ref, h_ref, route_ref, cnt_ref,
         wo_scr, rw_scr, carry_scr) = refs
    else:
        x_ref, ya_ref, yb_ref, wo_ref, nw_ref, xo_ref, h_ref, wo_scr = refs

    @pl.when(pl.program_id(0) == 0)
    def _():
        wo_scr[...] = wo_ref[...].astype(BF16)
        if moe:
            rw_hi, rw_lo = _split_bf16(rw_ref[...])
            rw_scr[0] = rw_hi
            rw_scr[1] = rw_lo
            carry_scr[...] = jnp.zeros_like(carry_scr)

    acc = x_ref[...] + jnp.dot(ya_ref[...], wo_scr[:da, :], preferred_element_type=F32) \
        + jnp.dot(yb_ref[...], wo_scr[da:, :], preferred_element_type=F32)
    xo_ref[...] = acc
    h = _rms(acc, nw_ref[...])
    h_ref[...] = h.astype(h_ref.dtype)
    if moe:
        route_ref[...] = _route_rows(h, rw_scr[0], rw_scr[1], carry_scr)
        cnt_ref[...] = carry_scr[...]


def out_proj(x, y_a, y_b, w_o, norm_w, router_w=None, bm=256):
    t, d = x.shape
    da, db = y_a.shape[1], y_b.shape[1]
    bm = min(bm, t)
    moe = router_w is not None
    row_spec = lambda w: pl.BlockSpec((bm, w), lambda i: (i, 0))
    const = lambda shape: pl.BlockSpec(shape, lambda i: (0,) * len(shape), pipeline_mode=pl.Buffered(1))
    in_specs = [row_spec(d), row_spec(da), row_spec(db), const((da + db, d)), const((1, d))]
    out_specs = [row_spec(d), row_spec(d)]
    out_shape = [jax.ShapeDtypeStruct((t, d), F32), jax.ShapeDtypeStruct((t, d), F32 if moe else BF16)]
    scratch = [pltpu.VMEM((da + db, d), BF16)]
    args = [x, y_a, y_b, w_o, norm_w.reshape(1, d)]
    if moe:
        in_specs.append(const((d, LANES)))
        args.append(jnp.zeros((d, LANES), F32).at[:, :N_EXPERTS].set(router_w))
        out_specs += [row_spec(LANES), pl.BlockSpec((1, LANES), lambda i: (0, 0))]
        out_shape += [jax.ShapeDtypeStruct((t, LANES), F32), jax.ShapeDtypeStruct((1, LANES), F32)]
        scratch += [pltpu.VMEM((2, d, LANES), BF16), pltpu.VMEM((1, LANES), F32)]
    return pl.pallas_call(
        functools.partial(_outproj_kernel, da=da, moe=moe),
        grid=(t // bm,),
        in_specs=in_specs,
        out_specs=out_specs,
        out_shape=out_shape,
        scratch_shapes=scratch,
        compiler_params=_cparams(("arbitrary",)),
        name="out_proj_route" if moe else "out_proj",
    )(*args)


def _ffn_kernel(x_ref, h_ref, wg_ref, wu_ref, wd_ref, o_ref):
    @pl.when(pl.program_id(1) == 0)
    def _():
        o_ref[...] = x_ref[...]

    hb = h_ref[...]
    g = jnp.dot(hb, wg_ref[...], preferred_element_type=F32)
    u = jnp.dot(hb, wu_ref[...], preferred_element_type=F32)
    act = (_silu(g) * u).astype(BF16)
    o_ref[...] += jnp.dot(act, wd_ref[...], preferred_element_type=F32)


def dense_ffn(x, h, wg, wu, wd, bm=512, bf=512):
    t, d = x.shape
    f = wg.shape[1]
    bm = min(bm, t)
    return pl.pallas_call(
        _ffn_kernel,
        grid=(t // bm, f // bf),
        in_specs=[
            pl.BlockSpec((bm, d), lambda i, j: (i, 0)),
            pl.BlockSpec((bm, d), lambda i, j: (i, 0)),
            pl.BlockSpec((d, bf), lambda i, j: (0, j)),
            pl.BlockSpec((d, bf), lambda i, j: (0, j)),
            pl.BlockSpec((bf, d), lambda i, j: (j, 0)),
        ],
        out_specs=pl.BlockSpec((bm, d), lambda i, j: (i, 0)),
        out_shape=jax.ShapeDtypeStruct((t, d), F32),
        compiler_params=_cparams(("parallel", "arbitrary")),
        name="dense_ffn",
    )(x, h, wg, wu, wd)


def _row_copy(src_ref, dst_ref, sem, src_row, dst_row):
    return pltpu.make_async_copy(src_ref.at[pl.ds(src_row, 1)], dst_ref.at[pl.ds(dst_row, 1)], sem)


ROW_DMA_UNROLL = 8


def _dispatch_kernel(zlo_ref, zhi_ref, pos_ref, h_ref, xs_ref, zero_scr, sem, zsem):
    bt = h_ref.shape[0]

    @pl.when(pl.program_id(0) == 0)
    def _():
        zero_scr[...] = jnp.zeros_like(zero_scr)
        for e in range(N_EXPERTS):
            def zstart(r, carry):
                _row_copy(zero_scr, xs_ref, zsem, 0, r).start()
                return carry
            lax.fori_loop(zlo_ref[e], zhi_ref[e], zstart, 0)
        for e in range(N_EXPERTS):
            def zwait(r, carry):
                _row_copy(zero_scr, xs_ref, zsem, 0, r).wait()
                return carry
            lax.fori_loop(zlo_ref[e], zhi_ref[e], zwait, 0)

    def start(r, carry):
        for kk in range(TOP_K):
            _row_copy(h_ref, xs_ref, sem, r, pos_ref[0, TOP_K * r + kk]).start()
        return carry

    lax.fori_loop(0, bt, start, 0, unroll=ROW_DMA_UNROLL)
    for kk in range(TOP_K):
        pltpu.make_async_copy(h_ref, xs_ref.at[pl.ds(0, bt)], sem).wait()


def moe_dispatch(h, pos, zero_lo, zero_hi, n_rows, bt=256):
    t, d = h.shape
    bt = min(bt, t)
    pos2 = pos.reshape(t // bt, 1, TOP_K * bt)
    return pl.pallas_call(
        _dispatch_kernel,
        grid_spec=pltpu.PrefetchScalarGridSpec(
            num_scalar_prefetch=2,
            grid=(t // bt,),
            in_specs=[
                pl.BlockSpec((None, 1, TOP_K * bt), lambda i, zl, zh: (i, 0, 0), memory_space=pltpu.SMEM),
                pl.BlockSpec((bt, d), lambda i, zl, zh: (i, 0)),
            ],
            out_specs=pl.BlockSpec(memory_space=pl.ANY),
            scratch_shapes=[pltpu.VMEM((8, d), F32), pltpu.SemaphoreType.DMA(()), pltpu.SemaphoreType.DMA(())],
        ),
        out_shape=jax.ShapeDtypeStruct((n_rows, d), F32),
        compiler_params=_cparams(("arbitrary",)),
        name="moe_dispatch",
    )(zero_lo, zero_hi, pos2, h)


def _gmm_kernel(te_ref, nv_ref, xs_ref, wg_ref, wu_ref, wd_ref, ys_ref, xb_scr):
    i, j = pl.program_id(0), pl.program_id(1)

    @pl.when(i < nv_ref[0])
    def _():
        @pl.when(j == 0)
        def _():
            xb_scr[...] = xs_ref[...].astype(BF16)
            ys_ref[...] = jnp.zeros_like(ys_ref)

        xb = xb_scr[...]
        g = jnp.dot(xb, wg_ref[...], preferred_element_type=F32)
        u = jnp.dot(xb, wu_ref[...], preferred_element_type=F32)
        act = (_silu(g) * u).astype(BF16)
        ys_ref[...] += jnp.dot(act, wd_ref[...], preferred_element_type=F32)

    @pl.when((i >= nv_ref[0]) & (j == 0))
    def _():
        ys_ref[...] = jnp.zeros_like(ys_ref)


def moe_grouped_ffn(xs, tile_expert, n_valid, wg, wu, wd, bm, bf=512):
    p, d = xs.shape
    f = wg.shape[2]
    nf = f // bf

    def row_map(i, j, te, nv):
        return (jnp.minimum(i, nv[0] - 1), 0)

    def fcol(i, j, nv):
        return jnp.where(i < nv[0], j, nf - 1)

    return pl.pallas_call(
        _gmm_kernel,
        grid_spec=pltpu.PrefetchScalarGridSpec(
            num_scalar_prefetch=2,
            grid=(p // bm, nf),
            in_specs=[
                pl.BlockSpec((bm, d), row_map),
                pl.BlockSpec((None, d, bf), lambda i, j, te, nv: (te[i], 0, fcol(i, j, nv))),
                pl.BlockSpec((None, d, bf), lambda i, j, te, nv: (te[i], 0, fcol(i, j, nv))),
                pl.BlockSpec((None, bf, d), lambda i, j, te, nv: (te[i], fcol(i, j, nv), 0)),
            ],
            out_specs=pl.BlockSpec((bm, d), lambda i, j, te, nv: (i, 0)),
            scratch_shapes=[pltpu.VMEM((bm, d), BF16)],
        ),
        out_shape=jax.ShapeDtypeStruct((p, d), F32),
        compiler_params=_cparams(("arbitrary", "arbitrary")),
        name="moe_grouped_ffn",
    )(tile_expert, n_valid, xs, wg, wu, wd)


def _combine_kernel(pos_ref, pos_next_ref, x_ref, route_ref, nw_ref, ys_ref, o_ref, y_scr, sem, *, final_norm):
    bt = x_ref.shape[0]
    i = pl.program_id(0)
    slot = i % 2

    def gather(p_ref, s):
        def start(r, carry):
            for kk in range(TOP_K):
                _row_copy(ys_ref, y_scr.at[s, kk], sem.at[s], p_ref[0, TOP_K * r + kk], r).start()
            return carry
        lax.fori_loop(0, bt, start, 0, unroll=ROW_DMA_UNROLL)

    @pl.when(i == 0)
    def _():
        gather(pos_ref, 0)

    @pl.when(i + 1 < pl.num_programs(0))
    def _():
        gather(pos_next_ref, 1 - slot)

    for kk in range(TOP_K):
        pltpu.make_async_copy(ys_ref.at[pl.ds(0, bt)], y_scr.at[slot, kk], sem.at[slot]).wait()
    w0 = route_ref[:, ROUTE_W0:ROUTE_W0 + 1]
    w1 = route_ref[:, ROUTE_W1:ROUTE_W1 + 1]
    out = x_ref[...] + (w0 * y_scr[slot, 0] + w1 * y_scr[slot, 1])
    if final_norm:
        out = _rms(out, nw_ref[...])
    o_ref[...] = out


def moe_combine(x, route, pos, ys, norm_w, final_norm, bt=256):
    t, d = x.shape
    bt = min(bt, t)
    n = t // bt
    pos2 = pos.reshape(n, 1, TOP_K * bt)
    pos_spec = lambda nxt: pl.BlockSpec((None, 1, TOP_K * bt), lambda i: (jnp.minimum(i + nxt, n - 1), 0, 0),
                                        memory_space=pltpu.SMEM)
    return pl.pallas_call(
        functools.partial(_combine_kernel, final_norm=final_norm),
        grid=(n,),
        in_specs=[
            pos_spec(0),
            pos_spec(1),
            pl.BlockSpec((bt, d), lambda i: (i, 0)),
            pl.BlockSpec((bt, LANES), lambda i: (i, 0)),
            pl.BlockSpec((1, d), lambda i: (0, 0)),
            pl.BlockSpec(memory_space=pl.ANY),
        ],
        out_specs=pl.BlockSpec((bt, d), lambda i: (i, 0)),
        out_shape=jax.ShapeDtypeStruct((t, d), F32),
        scratch_shapes=[pltpu.VMEM((2, TOP_K, bt, d), F32), pltpu.SemaphoreType.DMA((2,))],
        compiler_params=_cparams(("arbitrary",)),
        name="moe_combine",
    )(pos2, pos2, x, route, norm_w.reshape(1, d), ys)


def moe_ffn(x, h, route, counts, wg, wu, wd, norm_w, final_norm, bm=512):
    t, d = x.shape

    cnt = counts[0, :N_EXPERTS].astype(jnp.int32)
    tiles = (cnt + bm - 1) // bm
    tile_end = jnp.cumsum(tiles)
    offset = (tile_end - tiles) * bm
    n_tiles = (TOP_K * t) // bm + N_EXPERTS
    n_valid = tile_end[-1:].astype(jnp.int32)
    tile_id = jnp.minimum(jnp.arange(n_tiles, dtype=jnp.int32), n_valid[0] - 1)
    tile_expert = jnp.sum((tile_end[None, :] <= tile_id[:, None]).astype(jnp.int32), axis=1)
    experts = route[:, ROUTE_E0:ROUTE_E1 + 1].astype(jnp.int32)
    ranks = route[:, ROUTE_R0:ROUTE_R1 + 1].astype(jnp.int32)
    pos = offset[experts] + ranks
    zero_lo = offset + cnt
    zero_hi = jnp.concatenate([offset[1:], jnp.full((1,), n_tiles * bm, jnp.int32)])

    xs = moe_dispatch(h, pos, zero_lo, zero_hi, n_tiles * bm)
    ys = moe_grouped_ffn(xs, tile_expert, n_valid, wg, wu, wd, bm)
    return moe_combine(x, route, pos, ys, norm_w, final_norm)


def _norm_kernel(x_ref, nw_ref, o_ref):
    o_ref[...] = _rms(x_ref[...], nw_ref[...])


def final_norm(x, norm_w, bm=512):
    t, d = x.shape
    bm = min(bm, t)
    return pl.pallas_call(
        _norm_kernel,
        grid=(t // bm,),
        in_specs=[pl.BlockSpec((bm, d), lambda i: (i, 0)), pl.BlockSpec((1, d), lambda i: (0, 0))],
        out_specs=pl.BlockSpec((bm, d), lambda i: (i, 0)),
        out_shape=jax.ShapeDtypeStruct((t, d), F32),
        compiler_params=_cparams(("parallel",)),
        name="final_norm",
    )(x, norm_w.reshape(1, d))


def kernel(x, mix_norm_w, w_in, gmlp_w_s, gmlp_b_s, gmlp_ln_w, gmlp_ln_b, gmlp_out_w, hgrn_lb_logits, hgrn_out_w, w_o, ffn_norm_w, dense_w_gate, dense_w_up, dense_w_down, router_w, expert_w_gate, expert_w_up, expert_w_down, final_norm_w):
    batch, seq, d = x.shape
    depth = w_in.shape[0]
    d_gmlp = gmlp_ln_w.shape[1]
    d_hgrn = hgrn_out_w.shape[1]
    assert d_gmlp == d_hgrn and seq % HGRN_C == 0 and seq % GMLP_CHUNK == 0

    lbs = jax.nn.softmax(hgrn_lb_logits.astype(F32), axis=0)
    lbs = jnp.cumsum(lbs, axis=0) - lbs[0:1]

    xt = x.reshape(batch * seq, d)
    for l in range(depth):
        z = in_proj(xt, mix_norm_w[l], w_in[l])
        y_a = gmlp_mixer(z, gmlp_w_s[l], gmlp_b_s[l], gmlp_ln_w[l], gmlp_ln_b[l], gmlp_out_w[l])
        y_b = hgrn2_mixer(z, lbs[l], hgrn_out_w[l], batch, col0=2 * d_gmlp // d_hgrn)
        last = l == depth - 1
        j = l // 2
        if l % 2 == 0:
            xt, h = out_proj(xt, y_a, y_b, w_o[l], ffn_norm_w[l])
            xt = dense_ffn(xt, h, dense_w_gate[j].astype(BF16), dense_w_up[j].astype(BF16),
                           dense_w_down[j].astype(BF16))
            if last:
                xt = final_norm(xt, final_norm_w)
        else:
            xt, h, route, counts = out_proj(xt, y_a, y_b, w_o[l], ffn_norm_w[l], router_w=router_w[j])
            xt = moe_ffn(xt, h, route, counts, expert_w_gate[j].astype(BF16), expert_w_up[j].astype(BF16),
                         expert_w_down[j].astype(BF16), final_norm_w, final_norm=last)
    return xt.reshape(batch, seq, d)
```

```python
import functools

import jax
import jax.numpy as jnp
from jax import lax
from jax.experimental import pallas as pl
from jax.experimental.pallas import tpu as pltpu

F32 = jnp.float32
BF16 = jnp.bfloat16

GMLP_HEAD = 128
GMLP_CHUNK = 128
HGRN_HEAD = 128
N_EXPERTS = 8
TOP_K = 2
EPS = 1e-6
F_MIN = 1e-6

LANES = 128
GMLP_GROUP = 8
HGRN_C = 128
HGRN_GROUP = 8
HGRN_DIAG = 8
HGRN_DIAG_REF = 3
VMEM_LIMIT = 56 * 1024 * 1024


def _cparams(sem, vmem=VMEM_LIMIT):
    return pltpu.CompilerParams(dimension_semantics=sem, vmem_limit_bytes=vmem)


def _silu(x):
    hx = 0.5 * x
    return hx + hx * jnp.tanh(hx)


GELU_C1 = 0.7978845608028654
GELU_C2 = GELU_C1 * 0.044715


def _gelu_tanh(x):
    hx = 0.5 * x
    return hx + hx * jnp.tanh(x * (GELU_C1 + GELU_C2 * (x * x)))


def _rms(x, w):
    return x * lax.rsqrt(jnp.mean(x * x, axis=-1, keepdims=True) + EPS) * w


def _inproj_kernel(x_ref, nw_ref, w_ref, z_ref, h_scr):
    @pl.when(pl.program_id(1) == 0)
    def _():
        h_scr[...] = _rms(x_ref[...], nw_ref[...]).astype(BF16)

    z_ref[...] = jnp.dot(h_scr[...], w_ref[...], preferred_element_type=F32)


def in_proj(x, norm_w, w, bm=1024, bn=1024):
    t, d = x.shape
    n = w.shape[1]
    bm = min(bm, t)
    return pl.pallas_call(
        _inproj_kernel,
        grid=(t // bm, n // bn),
        in_specs=[
            pl.BlockSpec((bm, d), lambda i, j: (i, 0)),
            pl.BlockSpec((1, d), lambda i, j: (0, 0)),
            pl.BlockSpec((d, bn), lambda i, j: (0, j)),
        ],
        out_specs=pl.BlockSpec((bm, bn), lambda i, j: (i, j)),
        out_shape=jax.ShapeDtypeStruct((t, n), F32),
        scratch_shapes=[pltpu.VMEM((bm, d), BF16)],
        compiler_params=_cparams(("parallel", "arbitrary")),
        name="in_proj",
    )(x, norm_w.reshape(1, d), w)


def _gmlp_kernel(zu_ref, zv_ref, ws_ref, bs_ref, lnw_ref, lnb_ref, ow_ref, y_ref, wc_scr, *, n_heads, n_chunks):
    c = GMLP_CHUNK
    row = lax.broadcasted_iota(jnp.int32, (c, c), 0)
    col = lax.broadcasted_iota(jnp.int32, (c, c), 1)
    causal = row >= col
    ones = jnp.ones((GMLP_HEAD, GMLP_HEAD), BF16)
    lane_mean = lambda x: jnp.dot(x.astype(BF16), ones, preferred_element_type=F32) * (1.0 / GMLP_HEAD)

    for h in range(n_heads):
        wc_scr[h] = jnp.where(causal, ws_ref[h], 0.0).astype(BF16)

    def chunk(ci, carry):
        r0 = pl.multiple_of(ci * c, c)
        for g0 in range(0, n_heads, GMLP_GROUP):
            heads = range(g0, g0 + GMLP_GROUP)
            sls = [slice(h * GMLP_HEAD, (h + 1) * GMLP_HEAD) for h in heads]
            v = [_gelu_tanh(zv_ref[pl.ds(r0, c), sl]) for sl in sls]
            mu = [lane_mean(x) for x in v]
            vc = [x - m for x, m in zip(v, mu)]
            var = [lane_mean(x * x) for x in vc]
            vn = [(x * lax.rsqrt(s + EPS) * lnw_ref[:, sl] + lnb_ref[:, sl]).astype(BF16)
                  for x, s, sl in zip(vc, var, sls)]
            sv = [jnp.dot(wc_scr[h], x, preferred_element_type=F32) + bs_ref[:, h:h + 1] for h, x in zip(heads, vn)]
            y = [_gelu_tanh(zu_ref[pl.ds(r0, c), sl]) * x for sl, x in zip(sls, sv)]
            ms = [lane_mean(x * x) for x in y]
            for x, s, sl in zip(y, ms, sls):
                y_ref[pl.ds(r0, c), sl] = (x * lax.rsqrt(s + EPS) * ow_ref[:, sl]).astype(BF16)
        return carry

    lax.fori_loop(0, n_chunks, chunk, 0)


def gmlp_mixer(z, w_s, b_s, ln_w, ln_b, out_w, rows=512):
    t = z.shape[0]
    n_heads = w_s.shape[0]
    dg = n_heads * GMLP_HEAD
    rows = min(rows, t)
    kern = functools.partial(_gmlp_kernel, n_heads=n_heads, n_chunks=rows // GMLP_CHUNK)
    vec = lambda a: a.reshape(1, dg)
    return pl.pallas_call(
        kern,
        grid=(t // rows,),
        in_specs=[
            pl.BlockSpec((rows, dg), lambda i: (i, 0)),
            pl.BlockSpec((rows, dg), lambda i: (i, 1)),
            pl.BlockSpec((n_heads, GMLP_CHUNK, GMLP_CHUNK), lambda i: (0, 0, 0)),
            pl.BlockSpec((GMLP_CHUNK, n_heads), lambda i: (0, 0)),
            pl.BlockSpec((1, dg), lambda i: (0, 0)),
            pl.BlockSpec((1, dg), lambda i: (0, 0)),
            pl.BlockSpec((1, dg), lambda i: (0, 0)),
        ],
        out_specs=pl.BlockSpec((rows, dg), lambda i: (i, 0)),
        out_shape=jax.ShapeDtypeStruct((t, dg), BF16),
        scratch_shapes=[pltpu.VMEM((n_heads, GMLP_CHUNK, GMLP_CHUNK), BF16)],
        compiler_params=_cparams(("parallel",)),
        name="gmlp_mixer",
    )(z, z, w_s, b_s.T, vec(ln_w), vec(ln_b), vec(out_w))


def _hgrn_level_terms(a, q, k, blk):
    c, d = a.shape
    half = blk // 2
    nb = c // blk
    lower = lambda v: v.reshape(nb, blk, d)[:, :half, :]
    upper = lambda v: v.reshape(nb, blk, d)[:, half:, :]
    ref = a.reshape(nb, blk, d)[:, half - 1:half, :]
    qh = upper(q) * jnp.exp(upper(a) - ref)
    kh = lower(k) * jnp.exp(ref - lower(a))
    zeros = jnp.zeros_like(qh)
    qh = jnp.concatenate([zeros, qh], axis=1).reshape(c, d)
    kh = jnp.concatenate([kh, zeros], axis=1).reshape(c, d)
    return qh.astype(BF16), kh.astype(BF16)


def _hgrn_kernel(zq_ref, zf_ref, zi_ref, zg_ref, lb_ref, ow_ref, y_ref, st_ref, *, n_heads, n_chunks):
    c, d = HGRN_C, HGRN_HEAD
    nt = (((1,), (1,)), ((), ()))
    tn = (((0,), (0,)), ((), ()))

    @pl.when(pl.program_id(1) == 0)
    def _():
        st_ref[...] = jnp.zeros_like(st_ref)

    row = lax.broadcasted_iota(jnp.int32, (c, c), 0)
    col = lax.broadcasted_iota(jnp.int32, (c, c), 1)
    tril = (row >= col).astype(BF16)
    diag_mask = (row >= col) & ((row // HGRN_DIAG) == (col // HGRN_DIAG))
    level_masks = []
    blk = 2 * HGRN_DIAG
    while blk <= c:
        level_masks.append((blk, None if blk == c else (row // blk) == (col // blk)))
        blk *= 2

    def nt_dot(x, y):
        return lax.dot_general(x, y, nt, preferred_element_type=F32)

    def chunk(ci, carry):
        r0 = pl.multiple_of(ci * c, c)
        for g0 in range(0, n_heads, HGRN_GROUP):
            heads = list(range(g0, g0 + HGRN_GROUP))
            sls = [slice(h * d, (h + 1) * d) for h in heads]
            f = [0.5 * (1.0 + lb_ref[:, sl]) + (0.5 * (1.0 - lb_ref[:, sl])) * jnp.tanh(0.5 * zf_ref[pl.ds(r0, c), sl])
                 for sl in sls]
            logf = [jnp.log(jnp.clip(x, F_MIN, 1.0)) for x in f]
            pieces = [jnp.concatenate(_split_bf16(x), axis=1) for x in logf]
            cs = [jnp.dot(tril, x, preferred_element_type=F32) for x in pieces]
            a = [x[:, :d] + x[:, d:] for x in cs]
            k = [1.0 - x for x in f]
            q = [_silu(zq_ref[pl.ds(r0, c), sl]) for sl in sls]
            iv = [zi_ref[pl.ds(r0, c), sl].astype(BF16) for sl in sls]

            dd = []
            for x in a:
                a3 = x.reshape(c // HGRN_DIAG, HGRN_DIAG, d)
                dd.append((a3 - a3[:, HGRN_DIAG_REF:HGRN_DIAG_REF + 1, :]).reshape(c, d))
            sd = [nt_dot((qx * jnp.exp(x)).astype(BF16), (kx * jnp.exp(-x)).astype(BF16))
                  for qx, kx, x in zip(q, k, dd)]
            scores = [jnp.where(diag_mask, x, 0.0) for x in sd]

            for blk, same_block in level_masks:
                terms = [_hgrn_level_terms(ax, qx, kx, blk) for ax, qx, kx in zip(a, q, k)]
                prods = [nt_dot(qh, kh) for qh, kh in terms]
                scores = [s + (p if same_block is None else jnp.where(same_block, p, 0.0))
                          for s, p in zip(scores, prods)]

            st = [st_ref[h] for h in heads]
            qe = [(qx * jnp.exp(ax)).astype(BF16) for qx, ax in zip(q, a)]
            o = [jnp.dot(s.astype(BF16), ivx, preferred_element_type=F32) + nt_dot(qx, sx.astype(BF16))
                 for s, ivx, qx, sx in zip(scores, iv, qe, st)]
            kd = [(kx * jnp.exp(ax[c - 1:c, :] - ax)).astype(BF16) for kx, ax in zip(k, a)]
            for h, ax, sx, ivx, kx in zip(heads, a, st, iv, kd):
                st_ref[h] = jnp.exp(ax[c - 1:c, :]) * sx + lax.dot_general(ivx, kx, tn, preferred_element_type=F32)
            for x, sl in zip(o, sls):
                y_ref[pl.ds(r0, c), sl] = (_rms(x, ow_ref[:, sl]) * _silu(zg_ref[pl.ds(r0, c), sl])).astype(BF16)
        return carry

    lax.fori_loop(0, n_chunks, chunk, 0)


def hgrn2_mixer(z, lb, out_w, batch, col0, rows=512):
    t = z.shape[0]
    s = t // batch
    dh = lb.shape[0]
    n_heads = dh // HGRN_HEAD
    rows = min(rows, s)
    spb = s // rows
    kern = functools.partial(_hgrn_kernel, n_heads=n_heads, n_chunks=rows // HGRN_C)
    zspec = lambda j: pl.BlockSpec((rows, dh), lambda b, i, j=j: (b * spb + i, col0 + j))
    return pl.pallas_call(
        kern,
        grid=(batch, spb),
        in_specs=[zspec(0), zspec(1), zspec(2), zspec(3),
                  pl.BlockSpec((1, dh), lambda b, i: (0, 0)),
                  pl.BlockSpec((1, dh), lambda b, i: (0, 0))],
        out_specs=pl.BlockSpec((rows, dh), lambda b, i: (b * spb + i, 0)),
        out_shape=jax.ShapeDtypeStruct((t, dh), BF16),
        scratch_shapes=[pltpu.VMEM((n_heads, HGRN_HEAD, HGRN_HEAD), F32)],
        compiler_params=_cparams(("parallel", "arbitrary")),
        name="hgrn2_mixer",
    )(z, z, z, z, lb.reshape(1, dh), out_w.reshape(1, dh))


ROUTE_E0, ROUTE_E1, ROUTE_W0, ROUTE_W1, ROUTE_R0, ROUTE_R1 = range(6)


def _split_bf16(v):
    hi = v.astype(BF16)
    return hi, (v - hi.astype(F32)).astype(BF16)


def _route_rows(h, rw_pieces, carry_ref):
    bm = h.shape[0]
    h_hi, h_lo = _split_bf16(h)
    hh_hl = jnp.dot(h_hi, rw_pieces, preferred_element_type=F32)
    logits = hh_hl[:, :LANES] + hh_hl[:, LANES:] + jnp.dot(h_lo, rw_pieces[:, :LANES], preferred_element_type=F32)
    lane = lax.broadcasted_iota(jnp.int32, (bm, LANES), 1)
    neg = jnp.float32(-jnp.inf)
    logits = jnp.where(lane < N_EXPERTS, logits, neg)
    v0 = jnp.max(logits, axis=-1, keepdims=True)
    e0 = jnp.min(jnp.where(logits == v0, lane, LANES), axis=-1, keepdims=True)
    rest = jnp.where(lane == e0, neg, logits)
    v1 = jnp.max(rest, axis=-1, keepdims=True)
    e1 = jnp.min(jnp.where(rest == v1, lane, LANES), axis=-1, keepdims=True)
    ex = jnp.exp(v1 - v0)
    w0 = 1.0 / (1.0 + ex)
    w1 = ex / (1.0 + ex)

    onehot = ((lane == e0) | (lane == e1)).astype(F32)
    row = lax.broadcasted_iota(jnp.int32, (bm, bm), 0)
    col = lax.broadcasted_iota(jnp.int32, (bm, bm), 1)
    before = (row > col).astype(BF16)
    excl = jnp.dot(before, onehot.astype(BF16), preferred_element_type=F32) + carry_ref[...]
    r0 = jnp.sum(jnp.where(lane == e0, excl, 0.0), axis=-1, keepdims=True)
    r1 = jnp.sum(jnp.where(lane == e1, excl, 0.0), axis=-1, keepdims=True)
    carry_ref[...] += jnp.sum(onehot, axis=0, keepdims=True)

    out = jnp.zeros((bm, LANES), F32)
    for idx, val in ((ROUTE_E0, e0.astype(F32)), (ROUTE_E1, e1.astype(F32)), (ROUTE_W0, w0), (ROUTE_W1, w1),
                     (ROUTE_R0, r0), (ROUTE_R1, r1)):
        out = jnp.where(lane == idx, val, out)
    return out


def _outproj_kernel(*refs, da, moe):
    if moe:
        (x_ref, ya_ref, yb_ref, wo_ref, nw_ref, rw_ref, xo_ref, h_ref, route_ref, cnt_ref,
         wo_scr, rw_scr, carry_scr) = refs
    else:
        x_ref, ya_ref, yb_ref, wo_ref, nw_ref, xo_ref, h_ref, wo_scr = refs

    @pl.when(pl.program_id(0) == 0)
    def _():
        wo_scr[...] = wo_ref[...].astype(BF16)
        if moe:
            rw_hi, rw_lo = _split_bf16(rw_ref[...])
            rw_scr[:, :LANES] = rw_hi
            rw_scr[:, LANES:] = rw_lo
            carry_scr[...] = jnp.zeros_like(carry_scr)

    acc = x_ref[...] + jnp.dot(ya_ref[...], wo_scr[:da, :], preferred_element_type=F32) \
        + jnp.dot(yb_ref[...], wo_scr[da:, :], preferred_element_type=F32)
    xo_ref[...] = acc
    h = _rms(acc, nw_ref[...])
    h_ref[...] = h.astype(h_ref.dtype)
    if moe:
        route_ref[...] = _route_rows(h, rw_scr[...], carry_scr)
        cnt_ref[...] = carry_scr[...]


def out_proj(x, y_a, y_b, w_o, layer, norm_w, router_w=None, bm=256):
    t, d = x.shape
    da, db = y_a.shape[1], y_b.shape[1]
    bm = min(bm, t)
    moe = router_w is not None
    row_spec = lambda w: pl.BlockSpec((bm, w), lambda i: (i, 0))
    const = lambda shape: pl.BlockSpec(shape, lambda i: (0,) * len(shape), pipeline_mode=pl.Buffered(1))
    w_spec = pl.BlockSpec((None, da + db, d), lambda i: (layer, 0, 0), pipeline_mode=pl.Buffered(1))
    in_specs = [row_spec(d), row_spec(da), row_spec(db), w_spec, const((1, d))]
    out_specs = [row_spec(d), row_spec(d)]
    out_shape = [jax.ShapeDtypeStruct((t, d), F32), jax.ShapeDtypeStruct((t, d), F32 if moe else BF16)]
    scratch = [pltpu.VMEM((da + db, d), BF16)]
    args = [x, y_a, y_b, w_o, norm_w.reshape(1, d)]
    if moe:
        in_specs.append(const((d, LANES)))
        args.append(jnp.zeros((d, LANES), F32).at[:, :N_EXPERTS].set(router_w))
        out_specs += [row_spec(LANES), pl.BlockSpec((1, LANES), lambda i: (0, 0))]
        out_shape += [jax.ShapeDtypeStruct((t, LANES), F32), jax.ShapeDtypeStruct((1, LANES), F32)]
        scratch += [pltpu.VMEM((d, 2 * LANES), BF16), pltpu.VMEM((1, LANES), F32)]
    return pl.pallas_call(
        functools.partial(_outproj_kernel, da=da, moe=moe),
        grid=(t // bm,),
        in_specs=in_specs,
        out_specs=out_specs,
        out_shape=out_shape,
        scratch_shapes=scratch,
        compiler_params=_cparams(("arbitrary",)),
        name="out_proj_route" if moe else "out_proj",
    )(*args)


def _ffn_kernel(x_ref, h_ref, wg_ref, wu_ref, wd_ref, o_ref):
    @pl.when(pl.program_id(1) == 0)
    def _():
        o_ref[...] = x_ref[...]

    hb = h_ref[...]
    g = jnp.dot(hb, wg_ref[...], preferred_element_type=F32)
    u = jnp.dot(hb, wu_ref[...], preferred_element_type=F32)
    act = (_silu(g) * u).astype(BF16)
    o_ref[...] += jnp.dot(act, wd_ref[...], preferred_element_type=F32)


def dense_ffn(x, h, wg, wu, wd, bm=512, bf=512):
    t, d = x.shape
    f = wg.shape[1]
    bm = min(bm, t)
    return pl.pallas_call(
        _ffn_kernel,
        grid=(t // bm, f // bf),
        in_specs=[
            pl.BlockSpec((bm, d), lambda i, j: (i, 0)),
            pl.BlockSpec((bm, d), lambda i, j: (i, 0)),
            pl.BlockSpec((d, bf), lambda i, j: (0, j)),
            pl.BlockSpec((d, bf), lambda i, j: (0, j)),
            pl.BlockSpec((bf, d), lambda i, j: (j, 0)),
        ],
        out_specs=pl.BlockSpec((bm, d), lambda i, j: (i, 0)),
        out_shape=jax.ShapeDtypeStruct((t, d), F32),
        compiler_params=_cparams(("parallel", "arbitrary")),
        name="dense_ffn",
    )(x, h, wg, wu, wd)


def _row_copy(src_ref, dst_ref, sem, src_row, dst_row):
    return pltpu.make_async_copy(src_ref.at[pl.ds(src_row, 1)], dst_ref.at[pl.ds(dst_row, 1)], sem)


ROW_DMA_UNROLL = 8


def _dispatch_kernel(zlo_ref, zhi_ref, pos_ref, h_ref, xs_ref, zero_scr, sem, zsem):
    bt = h_ref.shape[0]

    @pl.when(pl.program_id(0) == 0)
    def _():
        zero_scr[...] = jnp.zeros_like(zero_scr)
        for e in range(N_EXPERTS):
            def zstart(r, carry):
                _row_copy(zero_scr, xs_ref, zsem, 0, r).start()
                return carry
            lax.fori_loop(zlo_ref[e], zhi_ref[e], zstart, 0)
        for e in range(N_EXPERTS):
            def zwait(r, carry):
                _row_copy(zero_scr, xs_ref, zsem, 0, r).wait()
                return carry
            lax.fori_loop(zlo_ref[e], zhi_ref[e], zwait, 0)

    def start(r, carry):
        for kk in range(TOP_K):
            _row_copy(h_ref, xs_ref, sem, r, pos_ref[0, TOP_K * r + kk]).start()
        return carry

    lax.fori_loop(0, bt, start, 0, unroll=ROW_DMA_UNROLL)
    for kk in range(TOP_K):
        pltpu.make_async_copy(h_ref, xs_ref.at[pl.ds(0, bt)], sem).wait()


def moe_dispatch(h, pos, zero_lo, zero_hi, n_rows, bt=256):
    t, d = h.shape
    bt = min(bt, t)
    pos2 = pos.reshape(t // bt, 1, TOP_K * bt)
    return pl.pallas_call(
        _dispatch_kernel,
        grid_spec=pltpu.PrefetchScalarGridSpec(
            num_scalar_prefetch=2,
            grid=(t // bt,),
            in_specs=[
                pl.BlockSpec((None, 1, TOP_K * bt), lambda i, zl, zh: (i, 0, 0), memory_space=pltpu.SMEM),
                pl.BlockSpec((bt, d), lambda i, zl, zh: (i, 0)),
            ],
            out_specs=pl.BlockSpec(memory_space=pl.ANY),
            scratch_shapes=[pltpu.VMEM((8, d), F32), pltpu.SemaphoreType.DMA(()), pltpu.SemaphoreType.DMA(())],
        ),
        out_shape=jax.ShapeDtypeStruct((n_rows, d), F32),
        compiler_params=_cparams(("arbitrary",)),
        name="moe_dispatch",
    )(zero_lo, zero_hi, pos2, h)


def _gmm_kernel(te_ref, nv_ref, xs_ref, wg_ref, wu_ref, wd_ref, ys_ref, xb_scr):
    i, j = pl.program_id(0), pl.program_id(1)

    @pl.when(i < nv_ref[0])
    def _():
        @pl.when(j == 0)
        def _():
            xb_scr[...] = xs_ref[...].astype(BF16)
            ys_ref[...] = jnp.zeros_like(ys_ref)

        xb = xb_scr[...]
        g = jnp.dot(xb, wg_ref[...], preferred_element_type=F32)
        u = jnp.dot(xb, wu_ref[...], preferred_element_type=F32)
        act = (_silu(g) * u).astype(BF16)
        ys_ref[...] += jnp.dot(act, wd_ref[...], preferred_element_type=F32)

    @pl.when((i >= nv_ref[0]) & (j == 0))
    def _():
        ys_ref[...] = jnp.zeros_like(ys_ref)


def moe_grouped_ffn(xs, tile_expert, n_valid, wg, wu, wd, bm, bf=512):
    p, d = xs.shape
    f = wg.shape[2]
    nf = f // bf

    def row_map(i, j, te, nv):
        return (jnp.minimum(i, nv[0] - 1), 0)

    def fcol(i, j, nv):
        return jnp.where(i < nv[0], j, nf - 1)

    return pl.pallas_call(
        _gmm_kernel,
        grid_spec=pltpu.PrefetchScalarGridSpec(
            num_scalar_prefetch=2,
            grid=(p // bm, nf),
            in_specs=[
                pl.BlockSpec((bm, d), row_map),
                pl.BlockSpec((None, d, bf), lambda i, j, te, nv: (te[i], 0, fcol(i, j, nv))),
                pl.BlockSpec((None, d, bf), lambda i, j, te, nv: (te[i], 0, fcol(i, j, nv))),
                pl.BlockSpec((None, bf, d), lambda i, j, te, nv: (te[i], fcol(i, j, nv), 0)),
            ],
            out_specs=pl.BlockSpec((bm, d), lambda i, j, te, nv: (i, 0)),
            scratch_shapes=[pltpu.VMEM((bm, d), BF16)],
        ),
        out_shape=jax.ShapeDtypeStruct((p, d), F32),
        compiler_params=_cparams(("arbitrary", "arbitrary")),
        name="moe_grouped_ffn",
    )(tile_expert, n_valid, xs, wg, wu, wd)


def _combine_kernel(pos_ref, pos_next_ref, x_ref, route_ref, nw_ref, ys_ref, o_ref, y_scr, sem, *, final_norm):
    bt = x_ref.shape[0]
    i = pl.program_id(0)
    slot = i % 2

    def gather(p_ref, s):
        def start(r, carry):
            for kk in range(TOP_K):
                _row_copy(ys_ref, y_scr.at[s, kk], sem.at[s], p_ref[0, TOP_K * r + kk], r).start()
            return carry
        lax.fori_loop(0, bt, start, 0, unroll=ROW_DMA_UNROLL)

    @pl.when(i == 0)
    def _():
        gather(pos_ref, 0)

    @pl.when(i + 1 < pl.num_programs(0))
    def _():
        gather(pos_next_ref, 1 - slot)

    for kk in range(TOP_K):
        pltpu.make_async_copy(ys_ref.at[pl.ds(0, bt)], y_scr.at[slot, kk], sem.at[slot]).wait()
    w0 = route_ref[:, ROUTE_W0:ROUTE_W0 + 1]
    w1 = route_ref[:, ROUTE_W1:ROUTE_W1 + 1]
    out = x_ref[...] + (w0 * y_scr[slot, 0] + w1 * y_scr[slot, 1])
    if final_norm:
        out = _rms(out, nw_ref[...])
    o_ref[...] = out


def moe_combine(x, route, pos, ys, norm_w, final_norm, bt=256):
    t, d = x.shape
    bt = min(bt, t)
    n = t // bt
    pos2 = pos.reshape(n, 1, TOP_K * bt)
    pos_spec = lambda nxt: pl.BlockSpec((None, 1, TOP_K * bt), lambda i: (jnp.minimum(i + nxt, n - 1), 0, 0),
                                        memory_space=pltpu.SMEM)
    return pl.pallas_call(
        functools.partial(_combine_kernel, final_norm=final_norm),
        grid=(n,),
        in_specs=[
            pos_spec(0),
            pos_spec(1),
            pl.BlockSpec((bt, d), lambda i: (i, 0)),
            pl.BlockSpec((bt, LANES), lambda i: (i, 0)),
            pl.BlockSpec((1, d), lambda i: (0, 0)),
            pl.BlockSpec(memory_space=pl.ANY),
        ],
        out_specs=pl.BlockSpec((bt, d), lambda i: (i, 0)),
        out_shape=jax.ShapeDtypeStruct((t, d), F32),
        scratch_shapes=[pltpu.VMEM((2, TOP_K, bt, d), F32), pltpu.SemaphoreType.DMA((2,))],
        compiler_params=_cparams(("arbitrary",)),
        name="moe_combine",
    )(pos2, pos2, x, route, norm_w.reshape(1, d), ys)


def moe_ffn(x, h, route, counts, wg, wu, wd, norm_w, final_norm, bm=512):
    t, d = x.shape

    cnt = counts[0, :N_EXPERTS].astype(jnp.int32)
    tiles = (cnt + bm - 1) // bm
    tile_end = jnp.cumsum(tiles)
    offset = (tile_end - tiles) * bm
    n_tiles = (TOP_K * t) // bm + N_EXPERTS
    n_valid = tile_end[-1:].astype(jnp.int32)
    tile_id = jnp.minimum(jnp.arange(n_tiles, dtype=jnp.int32), n_valid[0] - 1)
    tile_expert = jnp.sum((tile_end[None, :] <= tile_id[:, None]).astype(jnp.int32), axis=1)
    experts = route[:, ROUTE_E0:ROUTE_E1 + 1].astype(jnp.int32)
    ranks = route[:, ROUTE_R0:ROUTE_R1 + 1].astype(jnp.int32)
    pos = offset[experts] + ranks
    zero_lo = offset + cnt
    zero_hi = jnp.concatenate([offset[1:], jnp.full((1,), n_tiles * bm, jnp.int32)])

    xs = moe_dispatch(h, pos, zero_lo, zero_hi, n_tiles * bm)
    ys = moe_grouped_ffn(xs, tile_expert, n_valid, wg, wu, wd, bm)
    return moe_combine(x, route, pos, ys, norm_w, final_norm)


def _norm_kernel(x_ref, nw_ref, o_ref):
    o_ref[...] = _rms(x_ref[...], nw_ref[...])


def final_norm(x, norm_w, bm=512):
    t, d = x.shape
    bm = min(bm, t)
    return pl.pallas_call(
        _norm_kernel,
        grid=(t // bm,),
        in_specs=[pl.BlockSpec((bm, d), lambda i: (i, 0)), pl.BlockSpec((1, d), lambda i: (0, 0))],
        out_specs=pl.BlockSpec((bm, d), lambda i: (i, 0)),
        out_shape=jax.ShapeDtypeStruct((t, d), F32),
        compiler_params=_cparams(("parallel",)),
        name="final_norm",
    )(x, norm_w.reshape(1, d))


def kernel(x, mix_norm_w, w_in, gmlp_w_s, gmlp_b_s, gmlp_ln_w, gmlp_ln_b, gmlp_out_w, hgrn_lb_logits, hgrn_out_w, w_o, ffn_norm_w, dense_w_gate, dense_w_up, dense_w_down, router_w, expert_w_gate, expert_w_up, expert_w_down, final_norm_w):
    batch, seq, d = x.shape
    depth = w_in.shape[0]
    d_gmlp = gmlp_ln_w.shape[1]
    d_hgrn = hgrn_out_w.shape[1]
    assert d_gmlp == d_hgrn and seq % HGRN_C == 0 and seq % GMLP_CHUNK == 0

    lbs = jax.nn.softmax(hgrn_lb_logits.astype(F32), axis=0)
    lbs = jnp.cumsum(lbs, axis=0) - lbs[0:1]

    xt = x.reshape(batch * seq, d)
    for l in range(depth):
        z = in_proj(xt, mix_norm_w[l], w_in[l].astype(BF16))
        y_a = gmlp_mixer(z, gmlp_w_s[l], gmlp_b_s[l], gmlp_ln_w[l], gmlp_ln_b[l], gmlp_out_w[l])
        y_b = hgrn2_mixer(z, lbs[l], hgrn_out_w[l], batch, col0=2 * d_gmlp // d_hgrn)
        last = l == depth - 1
        j = l // 2
        if l % 2 == 0:
            xt, h = out_proj(xt, y_a, y_b, w_o, l, ffn_norm_w[l])
            xt = dense_ffn(xt, h, dense_w_gate[j].astype(BF16), dense_w_up[j].astype(BF16),
                           dense_w_down[j].astype(BF16))
            if last:
                xt = final_norm(xt, final_norm_w)
        else:
            xt, h, route, counts = out_proj(xt, y_a, y_b, w_o, l, ffn_norm_w[l], router_w=router_w[j])
            xt = moe_ffn(xt, h, route, counts, expert_w_gate[j].astype(BF16), expert_w_up[j].astype(BF16),
                         expert_w_down[j].astype(BF16), final_norm_w, final_norm=last)
    return xt.reshape(batch, seq, d)
```

```python
import functools

import jax
import jax.numpy as jnp
from jax import lax
from jax.experimental import pallas as pl
from jax.experimental.pallas import tpu as pltpu

F32 = jnp.float32
BF16 = jnp.bfloat16

GMLP_HEAD = 128
GMLP_CHUNK = 128
HGRN_HEAD = 128
N_EXPERTS = 8
TOP_K = 2
EPS = 1e-6
F_MIN = 1e-6

LANES = 128
GMLP_GROUP = 8
HGRN_C = 128
HGRN_GROUP = 8
HGRN_DIAG = 8
HGRN_DIAG_REF = 3
VMEM_LIMIT = 56 * 1024 * 1024


def _cparams(sem, vmem=VMEM_LIMIT):
    return pltpu.CompilerParams(dimension_semantics=sem, vmem_limit_bytes=vmem)


def _silu(x):
    hx = 0.5 * x
    return hx + hx * jnp.tanh(hx)


GELU_C1 = 0.7978845608028654
GELU_C2 = GELU_C1 * 0.044715


def _gelu_tanh(x):
    hx = 0.5 * x
    return hx + hx * jnp.tanh(x * (GELU_C1 + GELU_C2 * (x * x)))


def _rms(x, w):
    return x * lax.rsqrt(jnp.mean(x * x, axis=-1, keepdims=True) + EPS) * w


def _inproj_kernel(x_ref, nw_ref, w_ref, z_ref, h_scr):
    @pl.when(pl.program_id(1) == 0)
    def _():
        h_scr[...] = _rms(x_ref[...], nw_ref[...]).astype(BF16)

    z_ref[...] = jnp.dot(h_scr[...], w_ref[...], preferred_element_type=F32)


def in_proj(x, norm_w, w, bm=1024, bn=1024):
    t, d = x.shape
    n = w.shape[1]
    bm = min(bm, t)
    return pl.pallas_call(
        _inproj_kernel,
        grid=(t // bm, n // bn),
        in_specs=[
            pl.BlockSpec((bm, d), lambda i, j: (i, 0)),
            pl.BlockSpec((1, d), lambda i, j: (0, 0)),
            pl.BlockSpec((d, bn), lambda i, j: (0, j)),
        ],
        out_specs=pl.BlockSpec((bm, bn), lambda i, j: (i, j)),
        out_shape=jax.ShapeDtypeStruct((t, n), F32),
        scratch_shapes=[pltpu.VMEM((bm, d), BF16)],
        compiler_params=_cparams(("parallel", "arbitrary")),
        name="in_proj",
    )(x, norm_w.reshape(1, d), w)


def _gmlp_kernel(zu_ref, zv_ref, ws_ref, bs_ref, lnw_ref, lnb_ref, ow_ref, y_ref, wc_scr, *, n_heads, n_chunks):
    c = GMLP_CHUNK
    row = lax.broadcasted_iota(jnp.int32, (c, c), 0)
    col = lax.broadcasted_iota(jnp.int32, (c, c), 1)
    causal = row >= col
    ones = jnp.ones((GMLP_HEAD, GMLP_HEAD), BF16)
    lane_mean = lambda x: jnp.dot(x.astype(BF16), ones, preferred_element_type=F32) * (1.0 / GMLP_HEAD)

    for h in range(n_heads):
        wc_scr[h] = jnp.where(causal, ws_ref[h], 0.0).astype(BF16)

    def chunk(ci, carry):
        r0 = pl.multiple_of(ci * c, c)
        for g0 in range(0, n_heads, GMLP_GROUP):
            heads = range(g0, g0 + GMLP_GROUP)
            sls = [slice(h * GMLP_HEAD, (h + 1) * GMLP_HEAD) for h in heads]
            v = [_gelu_tanh(zv_ref[pl.ds(r0, c), sl]) for sl in sls]
            mu = [lane_mean(x) for x in v]
            vc = [x - m for x, m in zip(v, mu)]
            var = [lane_mean(x * x) for x in vc]
            vn = [(x * lax.rsqrt(s + EPS) * lnw_ref[:, sl] + lnb_ref[:, sl]).astype(BF16)
                  for x, s, sl in zip(vc, var, sls)]
            sv = [jnp.dot(wc_scr[h], x, preferred_element_type=F32) + bs_ref[:, h:h + 1] for h, x in zip(heads, vn)]
            y = [_gelu_tanh(zu_ref[pl.ds(r0, c), sl]) * x for sl, x in zip(sls, sv)]
            ms = [lane_mean(x * x) for x in y]
            for x, s, sl in zip(y, ms, sls):
                y_ref[pl.ds(r0, c), sl] = (x * lax.rsqrt(s + EPS) * ow_ref[:, sl]).astype(BF16)
        return carry

    lax.fori_loop(0, n_chunks, chunk, 0)


def gmlp_mixer(z, w_s, b_s, ln_w, ln_b, out_w, rows=512):
    t = z.shape[0]
    n_heads = w_s.shape[0]
    dg = n_heads * GMLP_HEAD
    rows = min(rows, t)
    kern = functools.partial(_gmlp_kernel, n_heads=n_heads, n_chunks=rows // GMLP_CHUNK)
    vec = lambda a: a.reshape(1, dg)
    return pl.pallas_call(
        kern,
        grid=(t // rows,),
        in_specs=[
            pl.BlockSpec((rows, dg), lambda i: (i, 0)),
            pl.BlockSpec((rows, dg), lambda i: (i, 1)),
            pl.BlockSpec((n_heads, GMLP_CHUNK, GMLP_CHUNK), lambda i: (0, 0, 0)),
            pl.BlockSpec((GMLP_CHUNK, n_heads), lambda i: (0, 0)),
            pl.BlockSpec((1, dg), lambda i: (0, 0)),
            pl.BlockSpec((1, dg), lambda i: (0, 0)),
            pl.BlockSpec((1, dg), lambda i: (0, 0)),
        ],
        out_specs=pl.BlockSpec((rows, dg), lambda i: (i, 0)),
        out_shape=jax.ShapeDtypeStruct((t, dg), BF16),
        scratch_shapes=[pltpu.VMEM((n_heads, GMLP_CHUNK, GMLP_CHUNK), BF16)],
        compiler_params=_cparams(("parallel",)),
        name="gmlp_mixer",
    )(z, z, w_s, b_s.T, vec(ln_w), vec(ln_b), vec(out_w))


def _hgrn_level_terms(a, q, k, blk):
    c, d = a.shape
    half = blk // 2
    nb = c // blk
    lower = lambda v: v.reshape(nb, blk, d)[:, :half, :]
    upper = lambda v: v.reshape(nb, blk, d)[:, half:, :]
    ref = a.reshape(nb, blk, d)[:, half - 1:half, :]
    qh = upper(q) * jnp.exp(upper(a) - ref)
    kh = lower(k) * jnp.exp(ref - lower(a))
    zeros = jnp.zeros_like(qh)
    qh = jnp.concatenate([zeros, qh], axis=1).reshape(c, d)
    kh = jnp.concatenate([kh, zeros], axis=1).reshape(c, d)
    return qh.astype(BF16), kh.astype(BF16)


def _hgrn_kernel(zq_ref, zf_ref, zi_ref, zg_ref, lb_ref, ow_ref, y_ref, st_ref, *, n_heads, n_chunks):
    c, d = HGRN_C, HGRN_HEAD
    nt = (((1,), (1,)), ((), ()))
    tn = (((0,), (0,)), ((), ()))

    @pl.when(pl.program_id(1) == 0)
    def _():
        st_ref[...] = jnp.zeros_like(st_ref)

    row = lax.broadcasted_iota(jnp.int32, (c, c), 0)
    col = lax.broadcasted_iota(jnp.int32, (c, c), 1)
    tril = (row >= col).astype(BF16)
    diag_mask = (row >= col) & ((row // HGRN_DIAG) == (col // HGRN_DIAG))
    level_masks = []
    blk = 2 * HGRN_DIAG
    while blk <= c:
        level_masks.append((blk, None if blk == c else (row // blk) == (col // blk)))
        blk *= 2

    def nt_dot(x, y):
        return lax.dot_general(x, y, nt, preferred_element_type=F32)

    def chunk(ci, carry):
        r0 = pl.multiple_of(ci * c, c)
        for g0 in range(0, n_heads, HGRN_GROUP):
            heads = list(range(g0, g0 + HGRN_GROUP))
            sls = [slice(h * d, (h + 1) * d) for h in heads]
            f = [0.5 * (1.0 + lb_ref[:, sl]) + (0.5 * (1.0 - lb_ref[:, sl])) * jnp.tanh(0.5 * zf_ref[pl.ds(r0, c), sl])
                 for sl in sls]
            logf = [jnp.log(jnp.clip(x, F_MIN, 1.0)) for x in f]
            pieces = [jnp.concatenate(_split_bf16(x), axis=1) for x in logf]
            cs = [jnp.dot(tril, x, preferred_element_type=F32) for x in pieces]
            a = [x[:, :d] + x[:, d:] for x in cs]
            k = [1.0 - x for x in f]
            q = [_silu(zq_ref[pl.ds(r0, c), sl]) for sl in sls]
            iv = [zi_ref[pl.ds(r0, c), sl].astype(BF16) for sl in sls]

            dd = []
            for x in a:
                a3 = x.reshape(c // HGRN_DIAG, HGRN_DIAG, d)
                dd.append((a3 - a3[:, HGRN_DIAG_REF:HGRN_DIAG_REF + 1, :]).reshape(c, d))
            sd = [nt_dot((qx * jnp.exp(x)).astype(BF16), (kx * jnp.exp(-x)).astype(BF16))
                  for qx, kx, x in zip(q, k, dd)]
            scores = [jnp.where(diag_mask, x, 0.0) for x in sd]

            for blk, same_block in level_masks:
                terms = [_hgrn_level_terms(ax, qx, kx, blk) for ax, qx, kx in zip(a, q, k)]
                prods = [nt_dot(qh, kh) for qh, kh in terms]
                scores = [s + (p if same_block is None else jnp.where(same_block, p, 0.0))
                          for s, p in zip(scores, prods)]

            st = [st_ref[h] for h in heads]
            qe = [(qx * jnp.exp(ax)).astype(BF16) for qx, ax in zip(q, a)]
            o = [jnp.dot(s.astype(BF16), ivx, preferred_element_type=F32) + nt_dot(qx, sx.astype(BF16))
                 for s, ivx, qx, sx in zip(scores, iv, qe, st)]
            kd = [(kx * jnp.exp(ax[c - 1:c, :] - ax)).astype(BF16) for kx, ax in zip(k, a)]
            for h, ax, sx, ivx, kx in zip(heads, a, st, iv, kd):
                st_ref[h] = jnp.exp(ax[c - 1:c, :]) * sx + lax.dot_general(ivx, kx, tn, preferred_element_type=F32)
            for x, sl in zip(o, sls):
                y_ref[pl.ds(r0, c), sl] = (_rms(x, ow_ref[:, sl]) * _silu(zg_ref[pl.ds(r0, c), sl])).astype(BF16)
        return carry

    lax.fori_loop(0, n_chunks, chunk, 0)


def hgrn2_mixer(z, lb, out_w, batch, col0, rows=512):
    t = z.shape[0]
    s = t // batch
    dh = lb.shape[0]
    n_heads = dh // HGRN_HEAD
    rows = min(rows, s)
    spb = s // rows
    kern = functools.partial(_hgrn_kernel, n_heads=n_heads, n_chunks=rows // HGRN_C)
    zspec = lambda j: pl.BlockSpec((rows, dh), lambda b, i, j=j: (b * spb + i, col0 + j))
    return pl.pallas_call(
        kern,
        grid=(batch, spb),
        in_specs=[zspec(0), zspec(1), zspec(2), zspec(3),
                  pl.BlockSpec((1, dh), lambda b, i: (0, 0)),
                  pl.BlockSpec((1, dh), lambda b, i: (0, 0))],
        out_specs=pl.BlockSpec((rows, dh), lambda b, i: (b * spb + i, 0)),
        out_shape=jax.ShapeDtypeStruct((t, dh), BF16),
        scratch_shapes=[pltpu.VMEM((n_heads, HGRN_HEAD, HGRN_HEAD), F32)],
        compiler_params=_cparams(("parallel", "arbitrary")),
        name="hgrn2_mixer",
    )(z, z, z, z, lb.reshape(1, dh), out_w.reshape(1, dh))


ROUTE_E0, ROUTE_E1, ROUTE_W0, ROUTE_W1, ROUTE_R0, ROUTE_R1 = range(6)


def _split_bf16(v):
    hi = v.astype(BF16)
    return hi, (v - hi.astype(F32)).astype(BF16)


def _route_rows(h, rw_pieces, carry_ref):
    bm = h.shape[0]
    h_hi, h_lo = _split_bf16(h)
    hh_hl = jnp.dot(h_hi, rw_pieces, preferred_element_type=F32)
    logits = hh_hl[:, :LANES] + hh_hl[:, LANES:] + jnp.dot(h_lo, rw_pieces[:, :LANES], preferred_element_type=F32)
    lane = lax.broadcasted_iota(jnp.int32, (bm, LANES), 1)
    neg = jnp.float32(-jnp.inf)
    logits = jnp.where(lane < N_EXPERTS, logits, neg)
    v0 = jnp.max(logits, axis=-1, keepdims=True)
    e0 = jnp.min(jnp.where(logits == v0, lane, LANES), axis=-1, keepdims=True)
    rest = jnp.where(lane == e0, neg, logits)
    v1 = jnp.max(rest, axis=-1, keepdims=True)
    e1 = jnp.min(jnp.where(rest == v1, lane, LANES), axis=-1, keepdims=True)
    ex = jnp.exp(v1 - v0)
    w0 = 1.0 / (1.0 + ex)
    w1 = ex / (1.0 + ex)

    onehot = ((lane == e0) | (lane == e1)).astype(F32)
    row = lax.broadcasted_iota(jnp.int32, (bm, bm), 0)
    col = lax.broadcasted_iota(jnp.int32, (bm, bm), 1)
    before = (row > col).astype(BF16)
    excl = jnp.dot(before, onehot.astype(BF16), preferred_element_type=F32) + carry_ref[...]
    r0 = jnp.sum(jnp.where(lane == e0, excl, 0.0), axis=-1, keepdims=True)
    r1 = jnp.sum(jnp.where(lane == e1, excl, 0.0), axis=-1, keepdims=True)
    carry_ref[...] += jnp.sum(onehot, axis=0, keepdims=True)

    out = jnp.zeros((bm, LANES), F32)
    for idx, val in ((ROUTE_E0, e0.astype(F32)), (ROUTE_E1, e1.astype(F32)), (ROUTE_W0, w0), (ROUTE_W1, w1),
                     (ROUTE_R0, r0), (ROUTE_R1, r1)):
        out = jnp.where(lane == idx, val, out)
    return out


def _outproj_kernel(*refs, da, moe):
    if moe:
        (x_ref, ya_ref, yb_ref, wo_ref, nw_ref, rw_ref, xo_ref, h_ref, route_ref, cnt_ref,
         wo_scr, rw_scr, carry_scr) = refs
    else:
        x_ref, ya_ref, yb_ref, wo_ref, nw_ref, xo_ref, h_ref, wo_scr = refs

    @pl.when(pl.program_id(0) == 0)
    def _():
        wo_scr[...] = wo_ref[...].astype(BF16)
        if moe:
            rw_hi, rw_lo = _split_bf16(rw_ref[...])
            rw_scr[:, :LANES] = rw_hi
            rw_scr[:, LANES:] = rw_lo
            carry_scr[...] = jnp.zeros_like(carry_scr)

    acc = x_ref[...] + jnp.dot(ya_ref[...], wo_scr[:da, :], preferred_element_type=F32) \
        + jnp.dot(yb_ref[...], wo_scr[da:, :], preferred_element_type=F32)
    xo_ref[...] = acc
    h = _rms(acc, nw_ref[...])
    h_ref[...] = h.astype(h_ref.dtype)
    if moe:
        route_ref[...] = _route_rows(h, rw_scr[...], carry_scr)
        cnt_ref[...] = carry_scr[...]


def out_proj(x, y_a, y_b, w_o, layer, norm_w, router_w=None, bm=256):
    t, d = x.shape
    da, db = y_a.shape[1], y_b.shape[1]
    bm = min(bm, t)
    moe = router_w is not None
    row_spec = lambda w: pl.BlockSpec((bm, w), lambda i: (i, 0))
    const = lambda shape: pl.BlockSpec(shape, lambda i: (0,) * len(shape), pipeline_mode=pl.Buffered(1))
    w_spec = pl.BlockSpec((None, da + db, d), lambda i: (layer, 0, 0), pipeline_mode=pl.Buffered(1))
    in_specs = [row_spec(d), row_spec(da), row_spec(db), w_spec, const((1, d))]
    out_specs = [row_spec(d), row_spec(d)]
    out_shape = [jax.ShapeDtypeStruct((t, d), F32), jax.ShapeDtypeStruct((t, d), F32 if moe else BF16)]
    scratch = [pltpu.VMEM((da + db, d), BF16)]
    args = [x, y_a, y_b, w_o, norm_w.reshape(1, d)]
    if moe:
        in_specs.append(const((d, LANES)))
        args.append(jnp.zeros((d, LANES), F32).at[:, :N_EXPERTS].set(router_w))
        out_specs += [row_spec(LANES), pl.BlockSpec((1, LANES), lambda i: (0, 0))]
        out_shape += [jax.ShapeDtypeStruct((t, LANES), F32), jax.ShapeDtypeStruct((1, LANES), F32)]
        scratch += [pltpu.VMEM((d, 2 * LANES), BF16), pltpu.VMEM((1, LANES), F32)]
    return pl.pallas_call(
        functools.partial(_outproj_kernel, da=da, moe=moe),
        grid=(t // bm,),
        in_specs=in_specs,
        out_specs=out_specs,
        out_shape=out_shape,
        scratch_shapes=scratch,
        compiler_params=_cparams(("arbitrary",)),
        name="out_proj_route" if moe else "out_proj",
    )(*args)


def _ffn_kernel(x_ref, h_ref, wg_ref, wu_ref, wd_ref, *rest):
    o_ref = rest[len(rest) // 2]

    @pl.when(pl.program_id(1) == 0)
    def _():
        o_ref[...] = x_ref[...]

    hb = h_ref[...]
    g = jnp.dot(hb, wg_ref[...], preferred_element_type=F32)
    u = jnp.dot(hb, wu_ref[...], preferred_element_type=F32)
    act = (_silu(g) * u).astype(BF16)
    o_ref[...] += jnp.dot(act, wd_ref[...], preferred_element_type=F32)

    n_cast = len(rest) // 2
    for src_ref, dst_ref in zip(rest[:n_cast], rest[n_cast + 1:]):
        dst_ref[...] = src_ref[...].astype(BF16)


def dense_ffn(x, h, wg, wu, wd, cast=(), bm=512, bf=512):
    t, d = x.shape
    f = wg.shape[1]
    bm = min(bm, t)
    ni, nj = t // bm, f // bf
    cast_specs = []
    for w in cast:
        r, c = w.shape
        if c % nj == 0 and (c // nj) % LANES == 0 and r % ni == 0:
            cast_specs.append(pl.BlockSpec((r // ni, c // nj), lambda i, j: (i, j)))
        else:
            assert r % (ni * nj) == 0, (w.shape, ni, nj)
            cast_specs.append(pl.BlockSpec((r // (ni * nj), c), lambda i, j: (i * nj + j, 0)))
    outs = pl.pallas_call(
        _ffn_kernel,
        grid=(ni, nj),
        in_specs=[
            pl.BlockSpec((bm, d), lambda i, j: (i, 0)),
            pl.BlockSpec((bm, d), lambda i, j: (i, 0)),
            pl.BlockSpec((d, bf), lambda i, j: (0, j)),
            pl.BlockSpec((d, bf), lambda i, j: (0, j)),
            pl.BlockSpec((bf, d), lambda i, j: (j, 0)),
        ] + cast_specs,
        out_specs=[pl.BlockSpec((bm, d), lambda i, j: (i, 0))] + cast_specs,
        out_shape=[jax.ShapeDtypeStruct((t, d), F32)] + [jax.ShapeDtypeStruct(w.shape, BF16) for w in cast],
        compiler_params=_cparams(("parallel", "arbitrary")),
        name="dense_ffn",
    )(x, h, wg, wu, wd, *cast)
    return outs[0], tuple(outs[1:])


def _row_copy(src_ref, dst_ref, sem, src_row, dst_row):
    return pltpu.make_async_copy(src_ref.at[pl.ds(src_row, 1)], dst_ref.at[pl.ds(dst_row, 1)], sem)


ROW_DMA_UNROLL = 8


def _dispatch_kernel(zlo_ref, zhi_ref, pos_ref, h_ref, xs_ref, zero_scr, sem, zsem):
    bt = h_ref.shape[0]

    @pl.when(pl.program_id(0) == 0)
    def _():
        zero_scr[...] = jnp.zeros_like(zero_scr)
        for e in range(N_EXPERTS):
            def zstart(r, carry):
                _row_copy(zero_scr, xs_ref, zsem, 0, r).start()
                return carry
            lax.fori_loop(zlo_ref[e], zhi_ref[e], zstart, 0)
        for e in range(N_EXPERTS):
            def zwait(r, carry):
                _row_copy(zero_scr, xs_ref, zsem, 0, r).wait()
                return carry
            lax.fori_loop(zlo_ref[e], zhi_ref[e], zwait, 0)

    def start(r, carry):
        for kk in range(TOP_K):
            _row_copy(h_ref, xs_ref, sem, r, pos_ref[0, TOP_K * r + kk]).start()
        return carry

    lax.fori_loop(0, bt, start, 0, unroll=ROW_DMA_UNROLL)
    for kk in range(TOP_K):
        pltpu.make_async_copy(h_ref, xs_ref.at[pl.ds(0, bt)], sem).wait()


def moe_dispatch(h, pos, zero_lo, zero_hi, n_rows, bt=256):
    t, d = h.shape
    bt = min(bt, t)
    pos2 = pos.reshape(t // bt, 1, TOP_K * bt)
    return pl.pallas_call(
        _dispatch_kernel,
        grid_spec=pltpu.PrefetchScalarGridSpec(
            num_scalar_prefetch=2,
            grid=(t // bt,),
            in_specs=[
                pl.BlockSpec((None, 1, TOP_K * bt), lambda i, zl, zh: (i, 0, 0), memory_space=pltpu.SMEM),
                pl.BlockSpec((bt, d), lambda i, zl, zh: (i, 0)),
            ],
            out_specs=pl.BlockSpec(memory_space=pl.ANY),
            scratch_shapes=[pltpu.VMEM((8, d), F32), pltpu.SemaphoreType.DMA(()), pltpu.SemaphoreType.DMA(())],
        ),
        out_shape=jax.ShapeDtypeStruct((n_rows, d), F32),
        compiler_params=_cparams(("arbitrary",)),
        name="moe_dispatch",
    )(zero_lo, zero_hi, pos2, h)


def _gmm_kernel(te_ref, tr_ref, nv_ref, xs_ref, wg_ref, wu_ref, wd_ref, ys_ref, xb_scr):
    i, j = pl.program_id(0), pl.program_id(1)
    bm = xs_ref.shape[0]
    rows = tr_ref[i]

    def ffn(n):
        @pl.when(j == 0)
        def _():
            xb_scr[:n, :] = xs_ref[:n, :].astype(BF16)
            ys_ref[...] = jnp.zeros_like(ys_ref)

        xb = xb_scr[:n, :]
        g = jnp.dot(xb, wg_ref[...], preferred_element_type=F32)
        u = jnp.dot(xb, wu_ref[...], preferred_element_type=F32)
        act = (_silu(g) * u).astype(BF16)
        ys_ref[:n, :] += jnp.dot(act, wd_ref[...], preferred_element_type=F32)

    @pl.when(rows > bm // 2)
    def _():
        ffn(bm)

    @pl.when((rows > 0) & (rows <= bm // 2))
    def _():
        ffn(bm // 2)

    @pl.when((rows == 0) & (j == 0))
    def _():
        ys_ref[...] = jnp.zeros_like(ys_ref)


def moe_grouped_ffn(xs, tile_expert, tile_rows, n_valid, wg, wu, wd, bm, bf=512):
    p, d = xs.shape
    f = wg.shape[2]
    nf = f // bf

    def row_map(i, j, te, tr, nv):
        return (jnp.minimum(i, nv[0] - 1), 0)

    def fcol(i, j, nv):
        return jnp.where(i < nv[0], j, nf - 1)

    return pl.pallas_call(
        _gmm_kernel,
        grid_spec=pltpu.PrefetchScalarGridSpec(
            num_scalar_prefetch=3,
            grid=(p // bm, nf),
            in_specs=[
                pl.BlockSpec((bm, d), row_map),
                pl.BlockSpec((None, d, bf), lambda i, j, te, tr, nv: (te[i], 0, fcol(i, j, nv))),
                pl.BlockSpec((None, d, bf), lambda i, j, te, tr, nv: (te[i], 0, fcol(i, j, nv))),
                pl.BlockSpec((None, bf, d), lambda i, j, te, tr, nv: (te[i], fcol(i, j, nv), 0)),
            ],
            out_specs=pl.BlockSpec((bm, d), lambda i, j, te, tr, nv: (i, 0)),
            scratch_shapes=[pltpu.VMEM((bm, d), BF16)],
        ),
        out_shape=jax.ShapeDtypeStruct((p, d), F32),
        compiler_params=_cparams(("arbitrary", "arbitrary")),
        name="moe_grouped_ffn",
    )(tile_expert, tile_rows, n_valid, xs, wg, wu, wd)


def _combine_kernel(pos_ref, pos_next_ref, x_ref, route_ref, nw_ref, ys_ref, o_ref, y_scr, sem, *, final_norm):
    bt = x_ref.shape[0]
    i = pl.program_id(0)
    slot = i % 2

    def gather(p_ref, s, unroll):
        def start(r, carry):
            for kk in range(TOP_K):
                _row_copy(ys_ref, y_scr.at[s, kk], sem.at[s], p_ref[0, TOP_K * r + kk], r).start()
            return carry
        lax.fori_loop(0, bt, start, 0, unroll=unroll)

    def wait(s):
        for kk in range(TOP_K):
            pltpu.make_async_copy(ys_ref.at[pl.ds(0, bt)], y_scr.at[s, kk], sem.at[s]).wait()

    @pl.when(i == 0)
    def _():
        gather(pos_ref, 0, ROW_DMA_UNROLL)

    wait(slot)
    gather(pos_next_ref, 1 - slot, True)
    w0 = route_ref[:, ROUTE_W0:ROUTE_W0 + 1]
    w1 = route_ref[:, ROUTE_W1:ROUTE_W1 + 1]
    out = x_ref[...] + (w0 * y_scr[slot, 0] + w1 * y_scr[slot, 1])
    if final_norm:
        out = _rms(out, nw_ref[...])
    o_ref[...] = out

    @pl.when(i + 1 == pl.num_programs(0))
    def _():
        wait(1 - slot)


def moe_combine(x, route, pos, ys, norm_w, final_norm, bt=256):
    t, d = x.shape
    bt = min(bt, t)
    n = t // bt
    pos2 = pos.reshape(n, 1, TOP_K * bt)
    pos_spec = lambda nxt: pl.BlockSpec((None, 1, TOP_K * bt), lambda i: (jnp.minimum(i + nxt, n - 1), 0, 0),
                                        memory_space=pltpu.SMEM)
    return pl.pallas_call(
        functools.partial(_combine_kernel, final_norm=final_norm),
        grid=(n,),
        in_specs=[
            pos_spec(0),
            pos_spec(1),
            pl.BlockSpec((bt, d), lambda i: (i, 0)),
            pl.BlockSpec((bt, LANES), lambda i: (i, 0)),
            pl.BlockSpec((1, d), lambda i: (0, 0)),
            pl.BlockSpec(memory_space=pl.ANY),
        ],
        out_specs=pl.BlockSpec((bt, d), lambda i: (i, 0)),
        out_shape=jax.ShapeDtypeStruct((t, d), F32),
        scratch_shapes=[pltpu.VMEM((2, TOP_K, bt, d), F32), pltpu.SemaphoreType.DMA((2,))],
        compiler_params=_cparams(("arbitrary",)),
        name="moe_combine",
    )(pos2, pos2, x, route, norm_w.reshape(1, d), ys)


def moe_ffn(x, h, route, counts, wg, wu, wd, norm_w, final_norm, bm=512):
    t, d = x.shape

    cnt = counts[0, :N_EXPERTS].astype(jnp.int32)
    tiles = (cnt + bm - 1) // bm
    tile_end = jnp.cumsum(tiles)
    offset = (tile_end - tiles) * bm
    n_tiles = (TOP_K * t) // bm + N_EXPERTS
    n_valid = tile_end[-1:].astype(jnp.int32)
    tile_id = jnp.minimum(jnp.arange(n_tiles, dtype=jnp.int32), n_valid[0] - 1)
    tile_expert = jnp.sum((tile_end[None, :] <= tile_id[:, None]).astype(jnp.int32), axis=1)
    tile_in_group = tile_id - (tile_end - tiles)[tile_expert]
    tile_rows = jnp.clip(cnt[tile_expert] - tile_in_group * bm, 0, bm)
    tile_rows = jnp.where(jnp.arange(n_tiles) < n_valid[0], tile_rows, 0).astype(jnp.int32)
    experts = route[:, ROUTE_E0:ROUTE_E1 + 1].astype(jnp.int32)
    ranks = route[:, ROUTE_R0:ROUTE_R1 + 1].astype(jnp.int32)
    pos = offset[experts] + ranks
    zero_lo = offset + cnt
    zero_hi = jnp.concatenate([offset[1:], jnp.full((1,), n_tiles * bm, jnp.int32)])

    xs = moe_dispatch(h, pos, zero_lo, zero_hi, n_tiles * bm)
    ys = moe_grouped_ffn(xs, tile_expert, tile_rows, n_valid, wg, wu, wd, bm)
    return moe_combine(x, route, pos, ys, norm_w, final_norm)


def _norm_kernel(x_ref, nw_ref, o_ref):
    o_ref[...] = _rms(x_ref[...], nw_ref[...])


def final_norm(x, norm_w, bm=512):
    t, d = x.shape
    bm = min(bm, t)
    return pl.pallas_call(
        _norm_kernel,
        grid=(t // bm,),
        in_specs=[pl.BlockSpec((bm, d), lambda i: (i, 0)), pl.BlockSpec((1, d), lambda i: (0, 0))],
        out_specs=pl.BlockSpec((bm, d), lambda i: (i, 0)),
        out_shape=jax.ShapeDtypeStruct((t, d), F32),
        compiler_params=_cparams(("parallel",)),
        name="final_norm",
    )(x, norm_w.reshape(1, d))


def kernel(x, mix_norm_w, w_in, gmlp_w_s, gmlp_b_s, gmlp_ln_w, gmlp_ln_b, gmlp_out_w, hgrn_lb_logits, hgrn_out_w, w_o, ffn_norm_w, dense_w_gate, dense_w_up, dense_w_down, router_w, expert_w_gate, expert_w_up, expert_w_down, final_norm_w):
    batch, seq, d = x.shape
    depth = w_in.shape[0]
    d_gmlp = gmlp_ln_w.shape[1]
    d_hgrn = hgrn_out_w.shape[1]
    assert d_gmlp == d_hgrn and seq % HGRN_C == 0 and seq % GMLP_CHUNK == 0

    lbs = jax.nn.softmax(hgrn_lb_logits.astype(F32), axis=0)
    lbs = jnp.cumsum(lbs, axis=0) - lbs[0:1]

    xt = x.reshape(batch * seq, d)
    for l in range(depth):
        z = in_proj(xt, mix_norm_w[l], w_in[l].astype(BF16))
        y_a = gmlp_mixer(z, gmlp_w_s[l], gmlp_b_s[l], gmlp_ln_w[l], gmlp_ln_b[l], gmlp_out_w[l])
        y_b = hgrn2_mixer(z, lbs[l], hgrn_out_w[l], batch, col0=2 * d_gmlp // d_hgrn)
        last = l == depth - 1
        j = l // 2
        if l % 2 == 0:
            xt, h = out_proj(xt, y_a, y_b, w_o, l, ffn_norm_w[l])
            experts = () if last else tuple(w[j].reshape(-1, w.shape[-1])
                                            for w in (expert_w_gate, expert_w_up, expert_w_down))
            xt, experts_bf16 = dense_ffn(xt, h, dense_w_gate[j].astype(BF16), dense_w_up[j].astype(BF16),
                                         dense_w_down[j].astype(BF16), cast=experts)
            if last:
                xt = final_norm(xt, final_norm_w)
        else:
            xt, h, route, counts = out_proj(xt, y_a, y_b, w_o, l, ffn_norm_w[l], router_w=router_w[j])
            wg, wu, wd = (w.reshape(s.shape[1:]) for w, s in
                          zip(experts_bf16, (expert_w_gate, expert_w_up, expert_w_down)))
            xt = moe_ffn(xt, h, route, counts, wg, wu, wd, final_norm_w, final_norm=last)
    return xt.reshape(batch, seq, d)
```

```python
import functools

import jax
import jax.numpy as jnp
from jax import lax
from jax.experimental import pallas as pl
from jax.experimental.pallas import tpu as pltpu

F32 = jnp.float32
BF16 = jnp.bfloat16

GMLP_HEAD = 128
GMLP_CHUNK = 128
HGRN_HEAD = 128
N_EXPERTS = 8
TOP_K = 2
EPS = 1e-6
F_MIN = 1e-6

LANES = 128
GMLP_GROUP = 8
HGRN_C = 128
HGRN_GROUP = 8
HGRN_DIAG = 8
HGRN_DIAG_REF = 3
VMEM_LIMIT = 56 * 1024 * 1024


def _cparams(sem, vmem=VMEM_LIMIT):
    return pltpu.CompilerParams(dimension_semantics=sem, vmem_limit_bytes=vmem)


def _silu(x):
    hx = 0.5 * x
    return hx + hx * jnp.tanh(hx)


GELU_C1 = 0.7978845608028654
GELU_C2 = GELU_C1 * 0.044715


def _gelu_tanh(x):
    hx = 0.5 * x
    return hx + hx * jnp.tanh(x * (GELU_C1 + GELU_C2 * (x * x)))


def _rms(x, w):
    return x * lax.rsqrt(jnp.mean(x * x, axis=-1, keepdims=True) + EPS) * w


def _inproj_kernel(x_ref, nw_ref, w_ref, z_ref, h_scr):
    @pl.when(pl.program_id(1) == 0)
    def _():
        h_scr[...] = _rms(x_ref[...], nw_ref[...]).astype(BF16)

    z_ref[...] = jnp.dot(h_scr[...], w_ref[...], preferred_element_type=F32)


def in_proj(x, norm_w, w, bm=1024, bn=1024):
    t, d = x.shape
    n = w.shape[1]
    bm = min(bm, t)
    return pl.pallas_call(
        _inproj_kernel,
        grid=(t // bm, n // bn),
        in_specs=[
            pl.BlockSpec((bm, d), lambda i, j: (i, 0)),
            pl.BlockSpec((1, d), lambda i, j: (0, 0)),
            pl.BlockSpec((d, bn), lambda i, j: (0, j)),
        ],
        out_specs=pl.BlockSpec((bm, bn), lambda i, j: (i, j)),
        out_shape=jax.ShapeDtypeStruct((t, n), F32),
        scratch_shapes=[pltpu.VMEM((bm, d), BF16)],
        compiler_params=_cparams(("parallel", "arbitrary")),
        name="in_proj",
    )(x, norm_w.reshape(1, d), w)


def _gmlp_kernel(zu_ref, zv_ref, ws_ref, bs_ref, lnw_ref, lnb_ref, ow_ref, y_ref, wc_scr, *, n_heads, n_chunks):
    c = GMLP_CHUNK
    row = lax.broadcasted_iota(jnp.int32, (c, c), 0)
    col = lax.broadcasted_iota(jnp.int32, (c, c), 1)
    causal = row >= col
    ones = jnp.ones((GMLP_HEAD, GMLP_HEAD), BF16)
    lane_mean = lambda x: jnp.dot(x.astype(BF16), ones, preferred_element_type=F32) * (1.0 / GMLP_HEAD)

    for h in range(n_heads):
        wc_scr[h] = jnp.where(causal, ws_ref[h], 0.0).astype(BF16)

    def chunk(ci, carry):
        r0 = pl.multiple_of(ci * c, c)
        for g0 in range(0, n_heads, GMLP_GROUP):
            heads = range(g0, g0 + GMLP_GROUP)
            sls = [slice(h * GMLP_HEAD, (h + 1) * GMLP_HEAD) for h in heads]
            v = [_gelu_tanh(zv_ref[pl.ds(r0, c), sl]) for sl in sls]
            mu = [lane_mean(x) for x in v]
            vc = [x - m for x, m in zip(v, mu)]
            var = [lane_mean(x * x) for x in vc]
            vn = [(x * lax.rsqrt(s + EPS) * lnw_ref[:, sl] + lnb_ref[:, sl]).astype(BF16)
                  for x, s, sl in zip(vc, var, sls)]
            sv = [jnp.dot(wc_scr[h], x, preferred_element_type=F32) + bs_ref[:, h:h + 1] for h, x in zip(heads, vn)]
            y = [_gelu_tanh(zu_ref[pl.ds(r0, c), sl]) * x for sl, x in zip(sls, sv)]
            ms = [lane_mean(x * x) for x in y]
            for x, s, sl in zip(y, ms, sls):
                y_ref[pl.ds(r0, c), sl] = (x * lax.rsqrt(s + EPS) * ow_ref[:, sl]).astype(BF16)
        return carry

    lax.fori_loop(0, n_chunks, chunk, 0)


def gmlp_mixer(z, w_s, b_s, ln_w, ln_b, out_w, rows=512):
    t = z.shape[0]
    n_heads = w_s.shape[0]
    dg = n_heads * GMLP_HEAD
    rows = min(rows, t)
    kern = functools.partial(_gmlp_kernel, n_heads=n_heads, n_chunks=rows // GMLP_CHUNK)
    vec = lambda a: a.reshape(1, dg)
    return pl.pallas_call(
        kern,
        grid=(t // rows,),
        in_specs=[
            pl.BlockSpec((rows, dg), lambda i: (i, 0)),
            pl.BlockSpec((rows, dg), lambda i: (i, 1)),
            pl.BlockSpec((n_heads, GMLP_CHUNK, GMLP_CHUNK), lambda i: (0, 0, 0)),
            pl.BlockSpec((GMLP_CHUNK, n_heads), lambda i: (0, 0)),
            pl.BlockSpec((1, dg), lambda i: (0, 0)),
            pl.BlockSpec((1, dg), lambda i: (0, 0)),
            pl.BlockSpec((1, dg), lambda i: (0, 0)),
        ],
        out_specs=pl.BlockSpec((rows, dg), lambda i: (i, 0)),
        out_shape=jax.ShapeDtypeStruct((t, dg), BF16),
        scratch_shapes=[pltpu.VMEM((n_heads, GMLP_CHUNK, GMLP_CHUNK), BF16)],
        compiler_params=_cparams(("parallel",)),
        name="gmlp_mixer",
    )(z, z, w_s, b_s.T, vec(ln_w), vec(ln_b), vec(out_w))


def _hgrn_level_terms(a, q, k, blk):
    c, d = a.shape
    half = blk // 2
    nb = c // blk
    lower = lambda v: v.reshape(nb, blk, d)[:, :half, :]
    upper = lambda v: v.reshape(nb, blk, d)[:, half:, :]
    ref = a.reshape(nb, blk, d)[:, half - 1:half, :]
    qh = upper(q) * jnp.exp(upper(a) - ref)
    kh = lower(k) * jnp.exp(ref - lower(a))
    zeros = jnp.zeros_like(qh)
    qh = jnp.concatenate([zeros, qh], axis=1).reshape(c, d)
    kh = jnp.concatenate([kh, zeros], axis=1).reshape(c, d)
    return qh.astype(BF16), kh.astype(BF16)


def _hgrn_kernel(zq_ref, zf_ref, zi_ref, zg_ref, lb_ref, ow_ref, y_ref, st_ref, *, n_heads, n_chunks):
    c, d = HGRN_C, HGRN_HEAD
    nt = (((1,), (1,)), ((), ()))
    tn = (((0,), (0,)), ((), ()))

    @pl.when(pl.program_id(1) == 0)
    def _():
        st_ref[...] = jnp.zeros_like(st_ref)

    row = lax.broadcasted_iota(jnp.int32, (c, c), 0)
    col = lax.broadcasted_iota(jnp.int32, (c, c), 1)
    tril = (row >= col).astype(BF16)
    diag_mask = (row >= col) & ((row // HGRN_DIAG) == (col // HGRN_DIAG))
    level_masks = []
    blk = 2 * HGRN_DIAG
    while blk <= c:
        level_masks.append((blk, None if blk == c else (row // blk) == (col // blk)))
        blk *= 2

    def nt_dot(x, y):
        return lax.dot_general(x, y, nt, preferred_element_type=F32)

    def chunk(ci, carry):
        r0 = pl.multiple_of(ci * c, c)
        for g0 in range(0, n_heads, HGRN_GROUP):
            heads = list(range(g0, g0 + HGRN_GROUP))
            sls = [slice(h * d, (h + 1) * d) for h in heads]
            f = [0.5 * (1.0 + lb_ref[:, sl]) + (0.5 * (1.0 - lb_ref[:, sl])) * jnp.tanh(0.5 * zf_ref[pl.ds(r0, c), sl])
                 for sl in sls]
            logf = [jnp.log(jnp.clip(x, F_MIN, 1.0)) for x in f]
            pieces = [jnp.concatenate(_split_bf16(x), axis=1) for x in logf]
            cs = [jnp.dot(tril, x, preferred_element_type=F32) for x in pieces]
            a = [x[:, :d] + x[:, d:] for x in cs]
            k = [1.0 - x for x in f]
            q = [_silu(zq_ref[pl.ds(r0, c), sl]) for sl in sls]
            iv = [zi_ref[pl.ds(r0, c), sl].astype(BF16) for sl in sls]

            dd = []
            for x in a:
                a3 = x.reshape(c // HGRN_DIAG, HGRN_DIAG, d)
                dd.append((a3 - a3[:, HGRN_DIAG_REF:HGRN_DIAG_REF + 1, :]).reshape(c, d))
            sd = [nt_dot((qx * jnp.exp(x)).astype(BF16), (kx * jnp.exp(-x)).astype(BF16))
                  for qx, kx, x in zip(q, k, dd)]

            scores = None
            for blk, same_block in reversed(level_masks):
                terms = [_hgrn_level_terms(ax, qx, kx, blk) for ax, qx, kx in zip(a, q, k)]
                prods = [nt_dot(qh, kh) for qh, kh in terms]
                scores = prods if scores is None else [jnp.where(same_block, p, s) for s, p in zip(scores, prods)]
            scores = [jnp.where(diag_mask, x, s) for x, s in zip(sd, scores)]

            st = [st_ref[h] for h in heads]
            qe = [(qx * jnp.exp(ax)).astype(BF16) for qx, ax in zip(q, a)]
            o = [jnp.dot(s.astype(BF16), ivx, preferred_element_type=F32) + nt_dot(qx, sx.astype(BF16))
                 for s, ivx, qx, sx in zip(scores, iv, qe, st)]
            kd = [(kx * jnp.exp(ax[c - 1:c, :] - ax)).astype(BF16) for kx, ax in zip(k, a)]
            for h, ax, sx, ivx, kx in zip(heads, a, st, iv, kd):
                st_ref[h] = jnp.exp(ax[c - 1:c, :]) * sx + lax.dot_general(ivx, kx, tn, preferred_element_type=F32)
            for x, sl in zip(o, sls):
                y_ref[pl.ds(r0, c), sl] = (_rms(x, ow_ref[:, sl]) * _silu(zg_ref[pl.ds(r0, c), sl])).astype(BF16)
        return carry

    lax.fori_loop(0, n_chunks, chunk, 0)


def hgrn2_mixer(z, lb, out_w, batch, col0, rows=512):
    t = z.shape[0]
    s = t // batch
    dh = lb.shape[0]
    n_heads = dh // HGRN_HEAD
    rows = min(rows, s)
    spb = s // rows
    kern = functools.partial(_hgrn_kernel, n_heads=n_heads, n_chunks=rows // HGRN_C)
    zspec = lambda j: pl.BlockSpec((rows, dh), lambda b, i, j=j: (b * spb + i, col0 + j))
    return pl.pallas_call(
        kern,
        grid=(batch, spb),
        in_specs=[zspec(0), zspec(1), zspec(2), zspec(3),
                  pl.BlockSpec((1, dh), lambda b, i: (0, 0)),
                  pl.BlockSpec((1, dh), lambda b, i: (0, 0))],
        out_specs=pl.BlockSpec((rows, dh), lambda b, i: (b * spb + i, 0)),
        out_shape=jax.ShapeDtypeStruct((t, dh), BF16),
        scratch_shapes=[pltpu.VMEM((n_heads, HGRN_HEAD, HGRN_HEAD), F32)],
        compiler_params=_cparams(("parallel", "arbitrary")),
        name="hgrn2_mixer",
    )(z, z, z, z, lb.reshape(1, dh), out_w.reshape(1, dh))


ROUTE_E0, ROUTE_E1, ROUTE_W0, ROUTE_W1, ROUTE_R0, ROUTE_R1 = range(6)


def _split_bf16(v):
    hi = v.astype(BF16)
    return hi, (v - hi.astype(F32)).astype(BF16)


def _route_rows(h, rw_pieces, carry_ref):
    bm = h.shape[0]
    h_hi, h_lo = _split_bf16(h)
    hh_hl = jnp.dot(h_hi, rw_pieces, preferred_element_type=F32)
    logits = hh_hl[:, :LANES] + hh_hl[:, LANES:] + jnp.dot(h_lo, rw_pieces[:, :LANES], preferred_element_type=F32)
    lane = lax.broadcasted_iota(jnp.int32, (bm, LANES), 1)
    neg = jnp.float32(-jnp.inf)
    logits = jnp.where(lane < N_EXPERTS, logits, neg)
    v0 = jnp.max(logits, axis=-1, keepdims=True)
    e0 = jnp.min(jnp.where(logits == v0, lane, LANES), axis=-1, keepdims=True)
    rest = jnp.where(lane == e0, neg, logits)
    v1 = jnp.max(rest, axis=-1, keepdims=True)
    e1 = jnp.min(jnp.where(rest == v1, lane, LANES), axis=-1, keepdims=True)
    ex = jnp.exp(v1 - v0)
    w0 = 1.0 / (1.0 + ex)
    w1 = ex / (1.0 + ex)

    onehot = ((lane == e0) | (lane == e1)).astype(F32)
    row = lax.broadcasted_iota(jnp.int32, (bm, bm), 0)
    col = lax.broadcasted_iota(jnp.int32, (bm, bm), 1)
    before = (row > col).astype(BF16)
    excl = jnp.dot(before, onehot.astype(BF16), preferred_element_type=F32) + carry_ref[...]
    r0 = jnp.sum(jnp.where(lane == e0, excl, 0.0), axis=-1, keepdims=True)
    r1 = jnp.sum(jnp.where(lane == e1, excl, 0.0), axis=-1, keepdims=True)
    carry_ref[...] += jnp.sum(onehot, axis=0, keepdims=True)

    out = jnp.zeros((bm, LANES), F32)
    for idx, val in ((ROUTE_E0, e0.astype(F32)), (ROUTE_E1, e1.astype(F32)), (ROUTE_W0, w0), (ROUTE_W1, w1),
                     (ROUTE_R0, r0), (ROUTE_R1, r1)):
        out = jnp.where(lane == idx, val, out)
    return out


def _outproj_kernel(*refs, da, moe):
    if moe:
        (x_ref, ya_ref, yb_ref, wo_ref, nw_ref, rw_ref, xo_ref, h_ref, route_ref, cnt_ref,
         wo_scr, rw_scr, carry_scr) = refs
    else:
        x_ref, ya_ref, yb_ref, wo_ref, nw_ref, xo_ref, h_ref, wo_scr = refs

    @pl.when(pl.program_id(0) == 0)
    def _():
        wo_scr[...] = wo_ref[...].astype(BF16)
        if moe:
            rw_hi, rw_lo = _split_bf16(rw_ref[...])
            rw_scr[:, :LANES] = rw_hi
            rw_scr[:, LANES:] = rw_lo
            carry_scr[...] = jnp.zeros_like(carry_scr)

    acc = x_ref[...] + jnp.dot(ya_ref[...], wo_scr[:da, :], preferred_element_type=F32) \
        + jnp.dot(yb_ref[...], wo_scr[da:, :], preferred_element_type=F32)
    xo_ref[...] = acc
    h = _rms(acc, nw_ref[...])
    h_ref[...] = h.astype(h_ref.dtype)
    if moe:
        route_ref[...] = _route_rows(h, rw_scr[...], carry_scr)
        cnt_ref[...] = carry_scr[...]


def out_proj(x, y_a, y_b, w_o, layer, norm_w, router_w=None, bm=256):
    t, d = x.shape
    da, db = y_a.shape[1], y_b.shape[1]
    bm = min(bm, t)
    moe = router_w is not None
    row_spec = lambda w: pl.BlockSpec((bm, w), lambda i: (i, 0))
    const = lambda shape: pl.BlockSpec(shape, lambda i: (0,) * len(shape), pipeline_mode=pl.Buffered(1))
    w_spec = pl.BlockSpec((None, da + db, d), lambda i: (layer, 0, 0), pipeline_mode=pl.Buffered(1))
    in_specs = [row_spec(d), row_spec(da), row_spec(db), w_spec, const((1, d))]
    out_specs = [row_spec(d), row_spec(d)]
    out_shape = [jax.ShapeDtypeStruct((t, d), F32), jax.ShapeDtypeStruct((t, d), F32 if moe else BF16)]
    scratch = [pltpu.VMEM((da + db, d), BF16)]
    args = [x, y_a, y_b, w_o, norm_w.reshape(1, d)]
    if moe:
        in_specs.append(const((d, LANES)))
        args.append(jnp.zeros((d, LANES), F32).at[:, :N_EXPERTS].set(router_w))
        out_specs += [row_spec(LANES), pl.BlockSpec((1, LANES), lambda i: (0, 0))]
        out_shape += [jax.ShapeDtypeStruct((t, LANES), F32), jax.ShapeDtypeStruct((1, LANES), F32)]
        scratch += [pltpu.VMEM((d, 2 * LANES), BF16), pltpu.VMEM((1, LANES), F32)]
    return pl.pallas_call(
        functools.partial(_outproj_kernel, da=da, moe=moe),
        grid=(t // bm,),
        in_specs=in_specs,
        out_specs=out_specs,
        out_shape=out_shape,
        scratch_shapes=scratch,
        compiler_params=_cparams(("arbitrary",)),
        name="out_proj_route" if moe else "out_proj",
    )(*args)


def _ffn_kernel(x_ref, h_ref, wg_ref, wu_ref, wd_ref, *rest):
    o_ref = rest[len(rest) // 2]

    @pl.when(pl.program_id(1) == 0)
    def _():
        o_ref[...] = x_ref[...]

    hb = h_ref[...]
    g = jnp.dot(hb, wg_ref[...], preferred_element_type=F32)
    u = jnp.dot(hb, wu_ref[...], preferred_element_type=F32)
    act = (_silu(g) * u).astype(BF16)
    o_ref[...] += jnp.dot(act, wd_ref[...], preferred_element_type=F32)

    n_cast = len(rest) // 2
    for src_ref, dst_ref in zip(rest[:n_cast], rest[n_cast + 1:]):
        dst_ref[...] = src_ref[...].astype(BF16)


def dense_ffn(x, h, wg, wu, wd, cast=(), bm=512, bf=512):
    t, d = x.shape
    f = wg.shape[1]
    bm = min(bm, t)
    ni, nj = t // bm, f // bf
    cast_specs = []
    for w in cast:
        r, c = w.shape
        if c % nj == 0 and (c // nj) % LANES == 0 and r % ni == 0:
            cast_specs.append(pl.BlockSpec((r // ni, c // nj), lambda i, j: (i, j)))
        else:
            assert r % (ni * nj) == 0, (w.shape, ni, nj)
            cast_specs.append(pl.BlockSpec((r // (ni * nj), c), lambda i, j: (i * nj + j, 0)))
    outs = pl.pallas_call(
        _ffn_kernel,
        grid=(ni, nj),
        in_specs=[
            pl.BlockSpec((bm, d), lambda i, j: (i, 0)),
            pl.BlockSpec((bm, d), lambda i, j: (i, 0)),
            pl.BlockSpec((d, bf), lambda i, j: (0, j)),
            pl.BlockSpec((d, bf), lambda i, j: (0, j)),
            pl.BlockSpec((bf, d), lambda i, j: (j, 0)),
        ] + cast_specs,
        out_specs=[pl.BlockSpec((bm, d), lambda i, j: (i, 0))] + cast_specs,
        out_shape=[jax.ShapeDtypeStruct((t, d), F32)] + [jax.ShapeDtypeStruct(w.shape, BF16) for w in cast],
        compiler_params=_cparams(("parallel", "arbitrary")),
        name="dense_ffn",
    )(x, h, wg, wu, wd, *cast)
    return outs[0], tuple(outs[1:])


def _row_copy(src_ref, dst_ref, sem, src_row, dst_row):
    return pltpu.make_async_copy(src_ref.at[pl.ds(src_row, 1)], dst_ref.at[pl.ds(dst_row, 1)], sem)


ROW_DMA_UNROLL = 8
GMM_TILE = 1024
GMM_SUBTILES = 2


def _dispatch_kernel(zlo_ref, zhi_ref, pos_ref, h_ref, xs_ref, zero_scr, sem, zsem):
    bt = h_ref.shape[0]

    @pl.when(pl.program_id(0) == 0)
    def _():
        zero_scr[...] = jnp.zeros_like(zero_scr)
        for e in range(N_EXPERTS):
            def zstart(r, carry):
                _row_copy(zero_scr, xs_ref, zsem, 0, r).start()
                return carry
            lax.fori_loop(zlo_ref[e], zhi_ref[e], zstart, 0)
        for e in range(N_EXPERTS):
            def zwait(r, carry):
                _row_copy(zero_scr, xs_ref, zsem, 0, r).wait()
                return carry
            lax.fori_loop(zlo_ref[e], zhi_ref[e], zwait, 0)

    def start(r, carry):
        for kk in range(TOP_K):
            _row_copy(h_ref, xs_ref, sem, r, pos_ref[0, TOP_K * r + kk]).start()
        return carry

    lax.fori_loop(0, bt, start, 0, unroll=ROW_DMA_UNROLL)
    for kk in range(TOP_K):
        pltpu.make_async_copy(h_ref, xs_ref.at[pl.ds(0, bt)], sem).wait()


def moe_dispatch(h, pos, zero_lo, zero_hi, n_rows, bt=256):
    t, d = h.shape
    bt = min(bt, t)
    pos2 = pos.reshape(t // bt, 1, TOP_K * bt)
    return pl.pallas_call(
        _dispatch_kernel,
        grid_spec=pltpu.PrefetchScalarGridSpec(
            num_scalar_prefetch=2,
            grid=(t // bt,),
            in_specs=[
                pl.BlockSpec((None, 1, TOP_K * bt), lambda i, zl, zh: (i, 0, 0), memory_space=pltpu.SMEM),
                pl.BlockSpec((bt, d), lambda i, zl, zh: (i, 0)),
            ],
            out_specs=pl.BlockSpec(memory_space=pl.ANY),
            scratch_shapes=[pltpu.VMEM((8, d), F32), pltpu.SemaphoreType.DMA(()), pltpu.SemaphoreType.DMA(())],
        ),
        out_shape=jax.ShapeDtypeStruct((n_rows, d), F32),
        compiler_params=_cparams(("arbitrary",)),
        name="moe_dispatch",
    )(zero_lo, zero_hi, pos2, h)


def _gmm_kernel(te_ref, tr_ref, nv_ref, xs_ref, wg_ref, wu_ref, wd_ref, ys_ref, xb_scr):
    i, j = pl.program_id(0), pl.program_id(1)
    bm = xs_ref.shape[0]
    rows = tr_ref[i]
    sub = bm // GMM_SUBTILES

    @pl.when(j == 0)
    def _():
        ys_ref[...] = jnp.zeros_like(ys_ref)

    def ffn(r0, n):
        @pl.when(j == 0)
        def _():
            xb_scr[r0:r0 + n, :] = xs_ref[r0:r0 + n, :].astype(BF16)

        xb = xb_scr[r0:r0 + n, :]
        g = jnp.dot(xb, wg_ref[...], preferred_element_type=F32)
        u = jnp.dot(xb, wu_ref[...], preferred_element_type=F32)
        act = (_silu(g) * u).astype(BF16)
        ys_ref[r0:r0 + n, :] += jnp.dot(act, wd_ref[...], preferred_element_type=F32)

    for s in range(GMM_SUBTILES):
        left = rows - s * sub

        @pl.when(left > sub // 2)
        def _():
            ffn(s * sub, sub)

        @pl.when((left > 0) & (left <= sub // 2))
        def _():
            ffn(s * sub, sub // 2)


def moe_grouped_ffn(xs, tile_expert, tile_rows, n_valid, wg, wu, wd, bm, bf=512):
    p, d = xs.shape
    f = wg.shape[2]
    nf = f // bf

    def row_map(i, j, te, tr, nv):
        return (jnp.minimum(i, nv[0] - 1), 0)

    def fcol(i, j, nv):
        return jnp.where(i < nv[0], j, nf - 1)

    return pl.pallas_call(
        _gmm_kernel,
        grid_spec=pltpu.PrefetchScalarGridSpec(
            num_scalar_prefetch=3,
            grid=(p // bm, nf),
            in_specs=[
                pl.BlockSpec((bm, d), row_map),
                pl.BlockSpec((None, d, bf), lambda i, j, te, tr, nv: (te[i], 0, fcol(i, j, nv))),
                pl.BlockSpec((None, d, bf), lambda i, j, te, tr, nv: (te[i], 0, fcol(i, j, nv))),
                pl.BlockSpec((None, bf, d), lambda i, j, te, tr, nv: (te[i], fcol(i, j, nv), 0)),
            ],
            out_specs=pl.BlockSpec((bm, d), lambda i, j, te, tr, nv: (i, 0)),
            scratch_shapes=[pltpu.VMEM((bm, d), BF16)],
        ),
        out_shape=jax.ShapeDtypeStruct((p, d), F32),
        compiler_params=_cparams(("arbitrary", "arbitrary")),
        name="moe_grouped_ffn",
    )(tile_expert, tile_rows, n_valid, xs, wg, wu, wd)


def _combine_kernel(pos_ref, pos_next_ref, x_ref, route_ref, nw_ref, ys_ref, o_ref, y_scr, sem, *, final_norm):
    bt = x_ref.shape[0]
    i = pl.program_id(0)
    slot = i % 2

    def gather(p_ref, s, unroll):
        def start(r, carry):
            for kk in range(TOP_K):
                _row_copy(ys_ref, y_scr.at[s, kk], sem.at[s], p_ref[0, TOP_K * r + kk], r).start()
            return carry
        lax.fori_loop(0, bt, start, 0, unroll=unroll)

    def wait(s):
        for kk in range(TOP_K):
            pltpu.make_async_copy(ys_ref.at[pl.ds(0, bt)], y_scr.at[s, kk], sem.at[s]).wait()

    @pl.when(i == 0)
    def _():
        gather(pos_ref, 0, ROW_DMA_UNROLL)

    wait(slot)
    gather(pos_next_ref, 1 - slot, True)
    w0 = route_ref[:, ROUTE_W0:ROUTE_W0 + 1]
    w1 = route_ref[:, ROUTE_W1:ROUTE_W1 + 1]
    out = x_ref[...] + (w0 * y_scr[slot, 0] + w1 * y_scr[slot, 1])
    if final_norm:
        out = _rms(out, nw_ref[...])
    o_ref[...] = out

    @pl.when(i + 1 == pl.num_programs(0))
    def _():
        wait(1 - slot)


def moe_combine(x, route, pos, ys, norm_w, final_norm, bt=256):
    t, d = x.shape
    bt = min(bt, t)
    n = t // bt
    pos2 = pos.reshape(n, 1, TOP_K * bt)
    pos_spec = lambda nxt: pl.BlockSpec((None, 1, TOP_K * bt), lambda i: (jnp.minimum(i + nxt, n - 1), 0, 0),
                                        memory_space=pltpu.SMEM)
    return pl.pallas_call(
        functools.partial(_combine_kernel, final_norm=final_norm),
        grid=(n,),
        in_specs=[
            pos_spec(0),
            pos_spec(1),
            pl.BlockSpec((bt, d), lambda i: (i, 0)),
            pl.BlockSpec((bt, LANES), lambda i: (i, 0)),
            pl.BlockSpec((1, d), lambda i: (0, 0)),
            pl.BlockSpec(memory_space=pl.ANY),
        ],
        out_specs=pl.BlockSpec((bt, d), lambda i: (i, 0)),
        out_shape=jax.ShapeDtypeStruct((t, d), F32),
        scratch_shapes=[pltpu.VMEM((2, TOP_K, bt, d), F32), pltpu.SemaphoreType.DMA((2,))],
        compiler_params=_cparams(("arbitrary",)),
        name="moe_combine",
    )(pos2, pos2, x, route, norm_w.reshape(1, d), ys)


def moe_ffn(x, h, route, counts, wg, wu, wd, norm_w, final_norm, bm=GMM_TILE):
    t, d = x.shape

    cnt = counts[0, :N_EXPERTS].astype(jnp.int32)
    tiles = (cnt + bm - 1) // bm
    tile_end = jnp.cumsum(tiles)
    offset = (tile_end - tiles) * bm
    n_tiles = (TOP_K * t) // bm + N_EXPERTS
    n_valid = tile_end[-1:].astype(jnp.int32)
    tile_id = jnp.minimum(jnp.arange(n_tiles, dtype=jnp.int32), n_valid[0] - 1)
    tile_expert = jnp.sum((tile_end[None, :] <= tile_id[:, None]).astype(jnp.int32), axis=1)
    tile_in_group = tile_id - (tile_end - tiles)[tile_expert]
    tile_rows = jnp.clip(cnt[tile_expert] - tile_in_group * bm, 0, bm)
    tile_rows = jnp.where(jnp.arange(n_tiles) < n_valid[0], tile_rows, 0).astype(jnp.int32)
    experts = route[:, ROUTE_E0:ROUTE_E1 + 1].astype(jnp.int32)
    ranks = route[:, ROUTE_R0:ROUTE_R1 + 1].astype(jnp.int32)
    pos = offset[experts] + ranks
    zero_lo = offset + cnt
    zero_hi = jnp.concatenate([offset[1:], jnp.full((1,), n_tiles * bm, jnp.int32)])

    xs = moe_dispatch(h, pos, zero_lo, zero_hi, n_tiles * bm)
    ys = moe_grouped_ffn(xs, tile_expert, tile_rows, n_valid, wg, wu, wd, bm)
    return moe_combine(x, route, pos, ys, norm_w, final_norm)


def _norm_kernel(x_ref, nw_ref, o_ref):
    o_ref[...] = _rms(x_ref[...], nw_ref[...])


def final_norm(x, norm_w, bm=512):
    t, d = x.shape
    bm = min(bm, t)
    return pl.pallas_call(
        _norm_kernel,
        grid=(t // bm,),
        in_specs=[pl.BlockSpec((bm, d), lambda i: (i, 0)), pl.BlockSpec((1, d), lambda i: (0, 0))],
        out_specs=pl.BlockSpec((bm, d), lambda i: (i, 0)),
        out_shape=jax.ShapeDtypeStruct((t, d), F32),
        compiler_params=_cparams(("parallel",)),
        name="final_norm",
    )(x, norm_w.reshape(1, d))


def kernel(x, mix_norm_w, w_in, gmlp_w_s, gmlp_b_s, gmlp_ln_w, gmlp_ln_b, gmlp_out_w, hgrn_lb_logits, hgrn_out_w, w_o, ffn_norm_w, dense_w_gate, dense_w_up, dense_w_down, router_w, expert_w_gate, expert_w_up, expert_w_down, final_norm_w):
    batch, seq, d = x.shape
    depth = w_in.shape[0]
    d_gmlp = gmlp_ln_w.shape[1]
    d_hgrn = hgrn_out_w.shape[1]
    assert d_gmlp == d_hgrn and seq % HGRN_C == 0 and seq % GMLP_CHUNK == 0

    lbs = jax.nn.softmax(hgrn_lb_logits.astype(F32), axis=0)
    lbs = jnp.cumsum(lbs, axis=0) - lbs[0:1]

    xt = x.reshape(batch * seq, d)
    for l in range(depth):
        z = in_proj(xt, mix_norm_w[l], w_in[l].astype(BF16))
        y_a = gmlp_mixer(z, gmlp_w_s[l], gmlp_b_s[l], gmlp_ln_w[l], gmlp_ln_b[l], gmlp_out_w[l])
        y_b = hgrn2_mixer(z, lbs[l], hgrn_out_w[l], batch, col0=2 * d_gmlp // d_hgrn)
        last = l == depth - 1
        j = l // 2
        if l % 2 == 0:
            xt, h = out_proj(xt, y_a, y_b, w_o, l, ffn_norm_w[l])
            experts = () if last else tuple(w[j].reshape(-1, w.shape[-1])
                                            for w in (expert_w_gate, expert_w_up, expert_w_down))
            xt, experts_bf16 = dense_ffn(xt, h, dense_w_gate[j].astype(BF16), dense_w_up[j].astype(BF16),
                                         dense_w_down[j].astype(BF16), cast=experts)
            if last:
                xt = final_norm(xt, final_norm_w)
        else:
            xt, h, route, counts = out_proj(xt, y_a, y_b, w_o, l, ffn_norm_w[l], router_w=router_w[j])
            wg, wu, wd = (w.reshape(s.shape[1:]) for w, s in
                          zip(experts_bf16, (expert_w_gate, expert_w_up, expert_w_down)))
            xt = moe_ffn(xt, h, route, counts, wg, wu, wd, final_norm_w, final_norm=last)
    return xt.reshape(batch, seq, d)
```

```python
import functools

import jax
import jax.numpy as jnp
from jax import lax
from jax.experimental import pallas as pl
from jax.experimental.pallas import tpu as pltpu

F32 = jnp.float32
BF16 = jnp.bfloat16

GMLP_HEAD = 128
GMLP_CHUNK = 128
HGRN_HEAD = 128
N_EXPERTS = 8
TOP_K = 2
EPS = 1e-6
F_MIN = 1e-6

LANES = 128
GMLP_GROUP = 8
HGRN_C = 128
HGRN_GROUP = 8
HGRN_DIAG = 8
HGRN_DIAG_REF = 3
VMEM_LIMIT = 56 * 1024 * 1024


def _cparams(sem, vmem=VMEM_LIMIT):
    return pltpu.CompilerParams(dimension_semantics=sem, vmem_limit_bytes=vmem)


def _silu(x):
    hx = 0.5 * x
    return hx + hx * jnp.tanh(hx)


GELU_C1 = 0.7978845608028654
GELU_C2 = GELU_C1 * 0.044715


def _gelu_tanh(x):
    hx = 0.5 * x
    return hx + hx * jnp.tanh(x * (GELU_C1 + GELU_C2 * (x * x)))


def _rms(x, w):
    return x * lax.rsqrt(jnp.mean(x * x, axis=-1, keepdims=True) + EPS) * w


def _inproj_kernel(x_ref, nw_ref, w_ref, z_ref, h_scr):
    @pl.when(pl.program_id(1) == 0)
    def _():
        h_scr[...] = _rms(x_ref[...], nw_ref[...]).astype(BF16)

    z_ref[...] = jnp.dot(h_scr[...], w_ref[...], preferred_element_type=F32)


def in_proj(x, norm_w, w, bm=1024, bn=1024):
    t, d = x.shape
    n = w.shape[1]
    bm = min(bm, t)
    return pl.pallas_call(
        _inproj_kernel,
        grid=(t // bm, n // bn),
        in_specs=[
            pl.BlockSpec((bm, d), lambda i, j: (i, 0)),
            pl.BlockSpec((1, d), lambda i, j: (0, 0)),
            pl.BlockSpec((d, bn), lambda i, j: (0, j)),
        ],
        out_specs=pl.BlockSpec((bm, bn), lambda i, j: (i, j)),
        out_shape=jax.ShapeDtypeStruct((t, n), F32),
        scratch_shapes=[pltpu.VMEM((bm, d), BF16)],
        compiler_params=_cparams(("parallel", "arbitrary")),
        name="in_proj",
    )(x, norm_w.reshape(1, d), w)


def _gmlp_kernel(zu_ref, zv_ref, ws_ref, bs_ref, lnw_ref, lnb_ref, ow_ref, y_ref, wc_scr, *, n_heads, n_chunks):
    c = GMLP_CHUNK
    row = lax.broadcasted_iota(jnp.int32, (c, c), 0)
    col = lax.broadcasted_iota(jnp.int32, (c, c), 1)
    causal = row >= col
    ones = jnp.ones((GMLP_HEAD, GMLP_HEAD), BF16)
    lane_mean = lambda x: jnp.dot(x.astype(BF16), ones, preferred_element_type=F32) * (1.0 / GMLP_HEAD)

    for h in range(n_heads):
        wc_scr[h] = jnp.where(causal, ws_ref[h], 0.0).astype(BF16)

    def chunk(ci, carry):
        r0 = pl.multiple_of(ci * c, c)
        for g0 in range(0, n_heads, GMLP_GROUP):
            heads = range(g0, g0 + GMLP_GROUP)
            sls = [slice(h * GMLP_HEAD, (h + 1) * GMLP_HEAD) for h in heads]
            v = [_gelu_tanh(zv_ref[pl.ds(r0, c), sl]) for sl in sls]
            mu = [lane_mean(x) for x in v]
            vc = [x - m for x, m in zip(v, mu)]
            var = [lane_mean(x * x) for x in vc]
            vn = [(x * lax.rsqrt(s + EPS) * lnw_ref[:, sl] + lnb_ref[:, sl]).astype(BF16)
                  for x, s, sl in zip(vc, var, sls)]
            sv = [jnp.dot(wc_scr[h], x, preferred_element_type=F32) + bs_ref[:, h:h + 1] for h, x in zip(heads, vn)]
            y = [_gelu_tanh(zu_ref[pl.ds(r0, c), sl]) * x for sl, x in zip(sls, sv)]
            ms = [lane_mean(x * x) for x in y]
            for x, s, sl in zip(y, ms, sls):
                y_ref[pl.ds(r0, c), sl] = (x * lax.rsqrt(s + EPS) * ow_ref[:, sl]).astype(BF16)
        return carry

    lax.fori_loop(0, n_chunks, chunk, 0)


def gmlp_mixer(z, w_s, b_s, ln_w, ln_b, out_w, rows=512):
    t = z.shape[0]
    n_heads = w_s.shape[0]
    dg = n_heads * GMLP_HEAD
    rows = min(rows, t)
    kern = functools.partial(_gmlp_kernel, n_heads=n_heads, n_chunks=rows // GMLP_CHUNK)
    vec = lambda a: a.reshape(1, dg)
    return pl.pallas_call(
        kern,
        grid=(t // rows,),
        in_specs=[
            pl.BlockSpec((rows, dg), lambda i: (i, 0)),
            pl.BlockSpec((rows, dg), lambda i: (i, 1)),
            pl.BlockSpec((n_heads, GMLP_CHUNK, GMLP_CHUNK), lambda i: (0, 0, 0)),
            pl.BlockSpec((GMLP_CHUNK, n_heads), lambda i: (0, 0)),
            pl.BlockSpec((1, dg), lambda i: (0, 0)),
            pl.BlockSpec((1, dg), lambda i: (0, 0)),
            pl.BlockSpec((1, dg), lambda i: (0, 0)),
        ],
        out_specs=pl.BlockSpec((rows, dg), lambda i: (i, 0)),
        out_shape=jax.ShapeDtypeStruct((t, dg), BF16),
        scratch_shapes=[pltpu.VMEM((n_heads, GMLP_CHUNK, GMLP_CHUNK), BF16)],
        compiler_params=_cparams(("parallel",)),
        name="gmlp_mixer",
    )(z, z, w_s, b_s.T, vec(ln_w), vec(ln_b), vec(out_w))


def _hgrn_level_terms(a, q, k, blk):
    c, d = a.shape
    half = blk // 2
    nb = c // blk
    lower = lambda v: v.reshape(nb, blk, d)[:, :half, :]
    upper = lambda v: v.reshape(nb, blk, d)[:, half:, :]
    ref = a.reshape(nb, blk, d)[:, half - 1:half, :]
    qh = upper(q) * jnp.exp(upper(a) - ref)
    kh = lower(k) * jnp.exp(ref - lower(a))
    zeros = jnp.zeros_like(qh)
    qh = jnp.concatenate([zeros, qh], axis=1).reshape(c, d)
    kh = jnp.concatenate([kh, zeros], axis=1).reshape(c, d)
    return qh.astype(BF16), kh.astype(BF16)


def _hgrn_kernel(zq_ref, zf_ref, zi_ref, zg_ref, lb_ref, ow_ref, y_ref, st_ref, *, n_heads, n_chunks):
    c, d = HGRN_C, HGRN_HEAD
    nt = (((1,), (1,)), ((), ()))
    tn = (((0,), (0,)), ((), ()))

    @pl.when(pl.program_id(1) == 0)
    def _():
        st_ref[...] = jnp.zeros_like(st_ref)

    row = lax.broadcasted_iota(jnp.int32, (c, c), 0)
    col = lax.broadcasted_iota(jnp.int32, (c, c), 1)
    tril = (row >= col).astype(BF16)
    diag_mask = (row >= col) & ((row // HGRN_DIAG) == (col // HGRN_DIAG))
    level_masks = []
    blk = 2 * HGRN_DIAG
    while blk <= c:
        level_masks.append((blk, None if blk == c else (row // blk) == (col // blk)))
        blk *= 2

    def nt_dot(x, y):
        return lax.dot_general(x, y, nt, preferred_element_type=F32)

    def chunk(ci, carry):
        r0 = pl.multiple_of(ci * c, c)
        for g0 in range(0, n_heads, HGRN_GROUP):
            heads = list(range(g0, g0 + HGRN_GROUP))
            sls = [slice(h * d, (h + 1) * d) for h in heads]
            f = [0.5 * (1.0 + lb_ref[:, sl]) + (0.5 * (1.0 - lb_ref[:, sl])) * jnp.tanh(0.5 * zf_ref[pl.ds(r0, c), sl])
                 for sl in sls]
            logf = [jnp.log(jnp.clip(x, F_MIN, 1.0)) for x in f]
            pieces = [jnp.concatenate(_split_bf16(x), axis=1) for x in logf]
            cs = [jnp.dot(tril, x, preferred_element_type=F32) for x in pieces]
            a = [x[:, :d] + x[:, d:] for x in cs]
            k = [1.0 - x for x in f]
            q = [_silu(zq_ref[pl.ds(r0, c), sl]) for sl in sls]
            iv = [zi_ref[pl.ds(r0, c), sl].astype(BF16) for sl in sls]

            dd = []
            for x in a:
                a3 = x.reshape(c // HGRN_DIAG, HGRN_DIAG, d)
                dd.append((a3 - a3[:, HGRN_DIAG_REF:HGRN_DIAG_REF + 1, :]).reshape(c, d))
            sd = [nt_dot((qx * jnp.exp(x)).astype(BF16), (kx * jnp.exp(-x)).astype(BF16))
                  for qx, kx, x in zip(q, k, dd)]

            scores = None
            for blk, same_block in reversed(level_masks):
                terms = [_hgrn_level_terms(ax, qx, kx, blk) for ax, qx, kx in zip(a, q, k)]
                prods = [nt_dot(qh, kh) for qh, kh in terms]
                scores = prods if scores is None else [jnp.where(same_block, p, s) for s, p in zip(scores, prods)]
            scores = [jnp.where(diag_mask, x, s) for x, s in zip(sd, scores)]

            st = [st_ref[h] for h in heads]
            qe = [(qx * jnp.exp(ax)).astype(BF16) for qx, ax in zip(q, a)]
            o = [jnp.dot(s.astype(BF16), ivx, preferred_element_type=F32) + nt_dot(qx, sx.astype(BF16))
                 for s, ivx, qx, sx in zip(scores, iv, qe, st)]
            kd = [(kx * jnp.exp(ax[c - 1:c, :] - ax)).astype(BF16) for kx, ax in zip(k, a)]
            for h, ax, sx, ivx, kx in zip(heads, a, st, iv, kd):
                st_ref[h] = jnp.exp(ax[c - 1:c, :]) * sx + lax.dot_general(ivx, kx, tn, preferred_element_type=F32)
            for x, sl in zip(o, sls):
                y_ref[pl.ds(r0, c), sl] = (_rms(x, ow_ref[:, sl]) * _silu(zg_ref[pl.ds(r0, c), sl])).astype(BF16)
        return carry

    lax.fori_loop(0, n_chunks, chunk, 0)


def hgrn2_mixer(z, lb, out_w, batch, col0, rows=512):
    t = z.shape[0]
    s = t // batch
    dh = lb.shape[0]
    n_heads = dh // HGRN_HEAD
    rows = min(rows, s)
    spb = s // rows
    kern = functools.partial(_hgrn_kernel, n_heads=n_heads, n_chunks=rows // HGRN_C)
    zspec = lambda j: pl.BlockSpec((rows, dh), lambda b, i, j=j: (b * spb + i, col0 + j))
    return pl.pallas_call(
        kern,
        grid=(batch, spb),
        in_specs=[zspec(0), zspec(1), zspec(2), zspec(3),
                  pl.BlockSpec((1, dh), lambda b, i: (0, 0)),
                  pl.BlockSpec((1, dh), lambda b, i: (0, 0))],
        out_specs=pl.BlockSpec((rows, dh), lambda b, i: (b * spb + i, 0)),
        out_shape=jax.ShapeDtypeStruct((t, dh), BF16),
        scratch_shapes=[pltpu.VMEM((n_heads, HGRN_HEAD, HGRN_HEAD), F32)],
        compiler_params=_cparams(("parallel", "arbitrary")),
        name="hgrn2_mixer",
    )(z, z, z, z, lb.reshape(1, dh), out_w.reshape(1, dh))


ROUTE_E0, ROUTE_E1, ROUTE_W0, ROUTE_W1, ROUTE_R0, ROUTE_R1 = range(6)


def _split_bf16(v):
    hi = v.astype(BF16)
    return hi, (v - hi.astype(F32)).astype(BF16)


def _route_rows(h, rw_pieces, carry_ref):
    bm = h.shape[0]
    h_hi, h_lo = _split_bf16(h)
    hh_hl = jnp.dot(h_hi, rw_pieces, preferred_element_type=F32)
    logits = hh_hl[:, :LANES] + hh_hl[:, LANES:] + jnp.dot(h_lo, rw_pieces[:, :LANES], preferred_element_type=F32)
    lane = lax.broadcasted_iota(jnp.int32, (bm, LANES), 1)
    neg = jnp.float32(-jnp.inf)
    logits = jnp.where(lane < N_EXPERTS, logits, neg)
    v0 = jnp.max(logits, axis=-1, keepdims=True)
    e0 = jnp.min(jnp.where(logits == v0, lane, LANES), axis=-1, keepdims=True)
    rest = jnp.where(lane == e0, neg, logits)
    v1 = jnp.max(rest, axis=-1, keepdims=True)
    e1 = jnp.min(jnp.where(rest == v1, lane, LANES), axis=-1, keepdims=True)
    ex = jnp.exp(v1 - v0)
    w0 = 1.0 / (1.0 + ex)
    w1 = ex / (1.0 + ex)

    onehot = ((lane == e0) | (lane == e1)).astype(F32)
    row = lax.broadcasted_iota(jnp.int32, (bm, bm), 0)
    col = lax.broadcasted_iota(jnp.int32, (bm, bm), 1)
    before = (row > col).astype(BF16)
    excl = jnp.dot(before, onehot.astype(BF16), preferred_element_type=F32) + carry_ref[...]
    r0 = jnp.sum(jnp.where(lane == e0, excl, 0.0), axis=-1, keepdims=True)
    r1 = jnp.sum(jnp.where(lane == e1, excl, 0.0), axis=-1, keepdims=True)
    carry_ref[...] += jnp.sum(onehot, axis=0, keepdims=True)

    out = jnp.zeros((bm, LANES), F32)
    for idx, val in ((ROUTE_E0, e0.astype(F32)), (ROUTE_E1, e1.astype(F32)), (ROUTE_W0, w0), (ROUTE_W1, w1),
                     (ROUTE_R0, r0), (ROUTE_R1, r1)):
        out = jnp.where(lane == idx, val, out)
    return out


def _outproj_kernel(*refs, da, moe):
    if moe:
        (x_ref, ya_ref, yb_ref, wo_ref, nw_ref, rw_ref, xo_ref, h_ref, route_ref, cnt_ref,
         wo_scr, rw_scr, carry_scr) = refs
    else:
        x_ref, ya_ref, yb_ref, wo_ref, nw_ref, xo_ref, h_ref, wo_scr = refs

    @pl.when(pl.program_id(0) == 0)
    def _():
        wo_scr[...] = wo_ref[...].astype(BF16)
        if moe:
            rw_hi, rw_lo = _split_bf16(rw_ref[...])
            rw_scr[:, :LANES] = rw_hi
            rw_scr[:, LANES:] = rw_lo
            carry_scr[...] = jnp.zeros_like(carry_scr)

    acc = x_ref[...] + jnp.dot(ya_ref[...], wo_scr[:da, :], preferred_element_type=F32) \
        + jnp.dot(yb_ref[...], wo_scr[da:, :], preferred_element_type=F32)
    xo_ref[...] = acc
    h = _rms(acc, nw_ref[...])
    h_ref[...] = h.astype(h_ref.dtype)
    if moe:
        route_ref[...] = _route_rows(h, rw_scr[...], carry_scr)
        cnt_ref[...] = carry_scr[...]


def out_proj(x, y_a, y_b, w_o, layer, norm_w, router_w=None, bm=256):
    t, d = x.shape
    da, db = y_a.shape[1], y_b.shape[1]
    bm = min(bm, t)
    moe = router_w is not None
    row_spec = lambda w: pl.BlockSpec((bm, w), lambda i: (i, 0))
    const = lambda shape: pl.BlockSpec(shape, lambda i: (0,) * len(shape), pipeline_mode=pl.Buffered(1))
    w_spec = pl.BlockSpec((None, da + db, d), lambda i: (layer, 0, 0), pipeline_mode=pl.Buffered(1))
    in_specs = [row_spec(d), row_spec(da), row_spec(db), w_spec, const((1, d))]
    out_specs = [row_spec(d), row_spec(d)]
    out_shape = [jax.ShapeDtypeStruct((t, d), F32), jax.ShapeDtypeStruct((t, d), F32 if moe else BF16)]
    scratch = [pltpu.VMEM((da + db, d), BF16)]
    args = [x, y_a, y_b, w_o, norm_w.reshape(1, d)]
    if moe:
        in_specs.append(const((d, LANES)))
        args.append(jnp.zeros((d, LANES), F32).at[:, :N_EXPERTS].set(router_w))
        out_specs += [row_spec(LANES), pl.BlockSpec((1, LANES), lambda i: (0, 0))]
        out_shape += [jax.ShapeDtypeStruct((t, LANES), F32), jax.ShapeDtypeStruct((1, LANES), F32)]
        scratch += [pltpu.VMEM((d, 2 * LANES), BF16), pltpu.VMEM((1, LANES), F32)]
    return pl.pallas_call(
        functools.partial(_outproj_kernel, da=da, moe=moe),
        grid=(t // bm,),
        in_specs=in_specs,
        out_specs=out_specs,
        out_shape=out_shape,
        scratch_shapes=scratch,
        compiler_params=_cparams(("arbitrary",)),
        name="out_proj_route" if moe else "out_proj",
    )(*args)


def _ffn_kernel(x_hbm, h_ref, wg_ref, wu_ref, wd_ref, *rest, n_pass):
    n_cast = (len(rest) - 2) // 2
    o_ref, sem = rest[n_cast], rest[-1]
    i, j, k = pl.program_id(0), pl.program_id(1), pl.program_id(2)
    bm = o_ref.shape[0]
    sub = bm // n_pass

    x_copy = pltpu.make_async_copy(x_hbm.at[pl.ds(pl.multiple_of(i * bm, bm), bm)], o_ref, sem)
    first = (j == 0) & (k == 0)

    @pl.when(first)
    def _():
        x_copy.start()

    rows = pl.ds(pl.multiple_of(k * sub, sub), sub)
    hb = h_ref[rows, :]
    g = jnp.dot(hb, wg_ref[...], preferred_element_type=F32)
    u = jnp.dot(hb, wu_ref[...], preferred_element_type=F32)
    act = (_silu(g) * u).astype(BF16)

    @pl.when(first)
    def _():
        x_copy.wait()

    o_ref[rows, :] += jnp.dot(act, wd_ref[...], preferred_element_type=F32)

    for src_ref, dst_ref in zip(rest[:n_cast], rest[n_cast + 1:-1]):
        dst_ref[...] = src_ref[...].astype(BF16)


def dense_ffn(x, h, wg, wu, wd, cast=(), bm=1024, bf=512, n_pass=2):
    t, d = x.shape
    f = wg.shape[1]
    bm = min(bm, t)
    ni, nj = t // bm, f // bf
    cast_specs = []
    for w in cast:
        r, c = w.shape
        if c % nj == 0 and (c // nj) % LANES == 0 and r % (ni * n_pass) == 0:
            cast_specs.append(pl.BlockSpec((r // (ni * n_pass), c // nj), lambda i, j, k: (i * n_pass + k, j)))
        else:
            assert r % (ni * nj * n_pass) == 0, (w.shape, ni, nj, n_pass)
            cast_specs.append(pl.BlockSpec((r // (ni * nj * n_pass), c),
                                           lambda i, j, k: ((i * nj + j) * n_pass + k, 0)))
    outs = pl.pallas_call(
        functools.partial(_ffn_kernel, n_pass=n_pass),
        grid=(ni, nj, n_pass),
        in_specs=[
            pl.BlockSpec(memory_space=pl.ANY),
            pl.BlockSpec((bm, d), lambda i, j, k: (i, 0)),
            pl.BlockSpec((d, bf), lambda i, j, k: (0, j)),
            pl.BlockSpec((d, bf), lambda i, j, k: (0, j)),
            pl.BlockSpec((bf, d), lambda i, j, k: (j, 0)),
        ] + cast_specs,
        out_specs=[pl.BlockSpec((bm, d), lambda i, j, k: (i, 0))] + cast_specs,
        out_shape=[jax.ShapeDtypeStruct((t, d), F32)] + [jax.ShapeDtypeStruct(w.shape, BF16) for w in cast],
        scratch_shapes=[pltpu.SemaphoreType.DMA(())],
        compiler_params=_cparams(("parallel", "arbitrary", "arbitrary")),
        name="dense_ffn",
    )(x, h, wg, wu, wd, *cast)
    return outs[0], tuple(outs[1:])


def _row_copy(src_ref, dst_ref, sem, src_row, dst_row):
    return pltpu.make_async_copy(src_ref.at[pl.ds(src_row, 1)], dst_ref.at[pl.ds(dst_row, 1)], sem)


ROW_DMA_UNROLL = 8
GMM_TILE = 1024
GMM_SUBTILES = 2


def _dispatch_kernel(zlo_ref, zhi_ref, pos_ref, h_ref, xs_ref, zero_scr, sem, zsem):
    bt = h_ref.shape[0]

    @pl.when(pl.program_id(0) == 0)
    def _():
        zero_scr[...] = jnp.zeros_like(zero_scr)
        zr = zero_scr.shape[0]

        def zero_rows(e, begin):
            lo, hi = zlo_ref[e], zhi_ref[e]
            mid = jnp.minimum(hi, (lo + zr - 1) // zr * zr)

            def one(r, carry):
                cp = _row_copy(zero_scr, xs_ref, zsem, 0, r)
                cp.start() if begin else cp.wait()
                return carry

            def slab(b, carry):
                cp = pltpu.make_async_copy(zero_scr, xs_ref.at[pl.ds(pl.multiple_of(b * zr, zr), zr)], zsem)
                cp.start() if begin else cp.wait()
                return carry

            lax.fori_loop(lo, mid, one, 0)
            lax.fori_loop(mid // zr, hi // zr, slab, 0)

        for e in range(N_EXPERTS):
            zero_rows(e, True)
        for e in range(N_EXPERTS):
            zero_rows(e, False)

    def start(r, carry):
        for kk in range(TOP_K):
            _row_copy(h_ref, xs_ref, sem, r, pos_ref[0, TOP_K * r + kk]).start()
        return carry

    lax.fori_loop(0, bt, start, 0, unroll=ROW_DMA_UNROLL)
    for kk in range(TOP_K):
        pltpu.make_async_copy(h_ref, xs_ref.at[pl.ds(0, bt)], sem).wait()


def moe_dispatch(h, pos, zero_lo, zero_hi, n_rows, bt=256):
    t, d = h.shape
    bt = min(bt, t)
    pos2 = pos.reshape(t // bt, 1, TOP_K * bt)
    return pl.pallas_call(
        _dispatch_kernel,
        grid_spec=pltpu.PrefetchScalarGridSpec(
            num_scalar_prefetch=2,
            grid=(t // bt,),
            in_specs=[
                pl.BlockSpec((None, 1, TOP_K * bt), lambda i, zl, zh: (i, 0, 0), memory_space=pltpu.SMEM),
                pl.BlockSpec((bt, d), lambda i, zl, zh: (i, 0)),
            ],
            out_specs=pl.BlockSpec(memory_space=pl.ANY),
            scratch_shapes=[pltpu.VMEM((8, d), F32), pltpu.SemaphoreType.DMA(()), pltpu.SemaphoreType.DMA(())],
        ),
        out_shape=jax.ShapeDtypeStruct((n_rows, d), F32),
        compiler_params=_cparams(("arbitrary",)),
        name="moe_dispatch",
    )(zero_lo, zero_hi, pos2, h)


def _gmm_kernel(te_ref, tr_ref, nv_ref, xs_ref, wg_ref, wu_ref, wd_ref, ys_ref, xb_scr):
    i, j = pl.program_id(0), pl.program_id(1)
    bm = xs_ref.shape[0]
    rows = tr_ref[i]
    sub = bm // GMM_SUBTILES

    @pl.when(j == 0)
    def _():
        xb_scr[...] = xs_ref[...].astype(BF16)
        ys_ref[...] = jnp.zeros_like(ys_ref)

    def ffn(r0, n):
        xb = xb_scr[r0:r0 + n, :]
        g = jnp.dot(xb, wg_ref[...], preferred_element_type=F32)
        u = jnp.dot(xb, wu_ref[...], preferred_element_type=F32)
        act = (_silu(g) * u).astype(BF16)
        ys_ref[r0:r0 + n, :] += jnp.dot(act, wd_ref[...], preferred_element_type=F32)

    full = rows == bm

    @pl.when(full)
    def _():
        for s in range(GMM_SUBTILES):
            ffn(s * sub, sub)

    for s in range(GMM_SUBTILES):
        left = rows - s * sub

        @pl.when(jnp.logical_not(full) & (left > sub // 2))
        def _():
            ffn(s * sub, sub)

        @pl.when((left > 0) & (left <= sub // 2))
        def _():
            ffn(s * sub, sub // 2)


def moe_grouped_ffn(xs, tile_expert, tile_rows, n_valid, wg, wu, wd, bm, bf=512):
    p, d = xs.shape
    f = wg.shape[2]
    nf = f // bf

    def row_map(i, j, te, tr, nv):
        return (jnp.minimum(i, nv[0] - 1), 0)

    def fcol(i, j, nv):
        return jnp.where(i < nv[0], j, nf - 1)

    return pl.pallas_call(
        _gmm_kernel,
        grid_spec=pltpu.PrefetchScalarGridSpec(
            num_scalar_prefetch=3,
            grid=(p // bm, nf),
            in_specs=[
                pl.BlockSpec((bm, d), row_map),
                pl.BlockSpec((None, d, bf), lambda i, j, te, tr, nv: (te[i], 0, fcol(i, j, nv))),
                pl.BlockSpec((None, d, bf), lambda i, j, te, tr, nv: (te[i], 0, fcol(i, j, nv))),
                pl.BlockSpec((None, bf, d), lambda i, j, te, tr, nv: (te[i], fcol(i, j, nv), 0)),
            ],
            out_specs=pl.BlockSpec((bm, d), lambda i, j, te, tr, nv: (i, 0)),
            scratch_shapes=[pltpu.VMEM((bm, d), BF16)],
        ),
        out_shape=jax.ShapeDtypeStruct((p, d), F32),
        compiler_params=_cparams(("arbitrary", "arbitrary")),
        name="moe_grouped_ffn",
    )(tile_expert, tile_rows, n_valid, xs, wg, wu, wd)


def _combine_kernel(pos_ref, pos_next_ref, x_ref, route_ref, nw_ref, ys_ref, o_ref, y_scr, sem, *, final_norm):
    bt = x_ref.shape[0]
    i = pl.program_id(0)
    slot = i % 2

    def gather(p_ref, s, unroll):
        def start(r, carry):
            for kk in range(TOP_K):
                _row_copy(ys_ref, y_scr.at[s, kk], sem.at[s], p_ref[0, TOP_K * r + kk], r).start()
            return carry
        lax.fori_loop(0, bt, start, 0, unroll=unroll)

    def wait(s):
        for kk in range(TOP_K):
            pltpu.make_async_copy(ys_ref.at[pl.ds(0, bt)], y_scr.at[s, kk], sem.at[s]).wait()

    @pl.when(i == 0)
    def _():
        gather(pos_ref, 0, ROW_DMA_UNROLL)

    wait(slot)
    gather(pos_next_ref, 1 - slot, True)
    w0 = route_ref[:, ROUTE_W0:ROUTE_W0 + 1]
    w1 = route_ref[:, ROUTE_W1:ROUTE_W1 + 1]
    out = x_ref[...] + (w0 * y_scr[slot, 0] + w1 * y_scr[slot, 1])
    if final_norm:
        out = _rms(out, nw_ref[...])
    o_ref[...] = out

    @pl.when(i + 1 == pl.num_programs(0))
    def _():
        wait(1 - slot)


def moe_combine(x, route, pos, ys, norm_w, final_norm, bt=256):
    t, d = x.shape
    bt = min(bt, t)
    n = t // bt
    pos2 = pos.reshape(n, 1, TOP_K * bt)
    pos_spec = lambda nxt: pl.BlockSpec((None, 1, TOP_K * bt), lambda i: (jnp.minimum(i + nxt, n - 1), 0, 0),
                                        memory_space=pltpu.SMEM)
    return pl.pallas_call(
        functools.partial(_combine_kernel, final_norm=final_norm),
        grid=(n,),
        in_specs=[
            pos_spec(0),
            pos_spec(1),
            pl.BlockSpec((bt, d), lambda i: (i, 0)),
            pl.BlockSpec((bt, LANES), lambda i: (i, 0)),
            pl.BlockSpec((1, d), lambda i: (0, 0)),
            pl.BlockSpec(memory_space=pl.ANY),
        ],
        out_specs=pl.BlockSpec((bt, d), lambda i: (i, 0)),
        out_shape=jax.ShapeDtypeStruct((t, d), F32),
        scratch_shapes=[pltpu.VMEM((2, TOP_K, bt, d), F32), pltpu.SemaphoreType.DMA((2,))],
        compiler_params=_cparams(("arbitrary",)),
        name="moe_combine",
    )(pos2, pos2, x, route, norm_w.reshape(1, d), ys)


def moe_ffn(x, h, route, counts, wg, wu, wd, norm_w, final_norm, bm=GMM_TILE):
    t, d = x.shape

    cnt = counts[0, :N_EXPERTS].astype(jnp.int32)
    tiles = (cnt + bm - 1) // bm
    tile_end = jnp.cumsum(tiles)
    offset = (tile_end - tiles) * bm
    n_tiles = (TOP_K * t) // bm + N_EXPERTS
    n_valid = tile_end[-1:].astype(jnp.int32)
    tile_id = jnp.minimum(jnp.arange(n_tiles, dtype=jnp.int32), n_valid[0] - 1)
    tile_expert = jnp.sum((tile_end[None, :] <= tile_id[:, None]).astype(jnp.int32), axis=1)
    tile_in_group = tile_id - (tile_end - tiles)[tile_expert]
    tile_rows = jnp.clip(cnt[tile_expert] - tile_in_group * bm, 0, bm)
    tile_rows = jnp.where(jnp.arange(n_tiles) < n_valid[0], tile_rows, 0).astype(jnp.int32)
    experts = route[:, ROUTE_E0:ROUTE_E1 + 1].astype(jnp.int32)
    ranks = route[:, ROUTE_R0:ROUTE_R1 + 1].astype(jnp.int32)
    pos = offset[experts] + ranks
    zero_lo = offset + cnt
    zero_hi = jnp.concatenate([offset[1:], jnp.full((1,), n_tiles * bm, jnp.int32)])

    xs = moe_dispatch(h, pos, zero_lo, zero_hi, n_tiles * bm)
    ys = moe_grouped_ffn(xs, tile_expert, tile_rows, n_valid, wg, wu, wd, bm)
    return moe_combine(x, route, pos, ys, norm_w, final_norm)


def _norm_kernel(x_ref, nw_ref, o_ref):
    o_ref[...] = _rms(x_ref[...], nw_ref[...])


def final_norm(x, norm_w, bm=512):
    t, d = x.shape
    bm = min(bm, t)
    return pl.pallas_call(
        _norm_kernel,
        grid=(t // bm,),
        in_specs=[pl.BlockSpec((bm, d), lambda i: (i, 0)), pl.BlockSpec((1, d), lambda i: (0, 0))],
        out_specs=pl.BlockSpec((bm, d), lambda i: (i, 0)),
        out_shape=jax.ShapeDtypeStruct((t, d), F32),
        compiler_params=_cparams(("parallel",)),
        name="final_norm",
    )(x, norm_w.reshape(1, d))


def kernel(x, mix_norm_w, w_in, gmlp_w_s, gmlp_b_s, gmlp_ln_w, gmlp_ln_b, gmlp_out_w, hgrn_lb_logits, hgrn_out_w, w_o, ffn_norm_w, dense_w_gate, dense_w_up, dense_w_down, router_w, expert_w_gate, expert_w_up, expert_w_down, final_norm_w):
    batch, seq, d = x.shape
    depth = w_in.shape[0]
    d_gmlp = gmlp_ln_w.shape[1]
    d_hgrn = hgrn_out_w.shape[1]
    assert d_gmlp == d_hgrn and seq % HGRN_C == 0 and seq % GMLP_CHUNK == 0

    lbs = jax.nn.softmax(hgrn_lb_logits.astype(F32), axis=0)
    lbs = jnp.cumsum(lbs, axis=0) - lbs[0:1]

    xt = x.reshape(batch * seq, d)
    for l in range(depth):
        z = in_proj(xt, mix_norm_w[l], w_in[l].astype(BF16))
        y_a = gmlp_mixer(z, gmlp_w_s[l], gmlp_b_s[l], gmlp_ln_w[l], gmlp_ln_b[l], gmlp_out_w[l])
        y_b = hgrn2_mixer(z, lbs[l], hgrn_out_w[l], batch, col0=2 * d_gmlp // d_hgrn)
        last = l == depth - 1
        j = l // 2
        if l % 2 == 0:
            xt, h = out_proj(xt, y_a, y_b, w_o, l, ffn_norm_w[l])
            experts = () if last else tuple(w[j].reshape(-1, w.shape[-1])
                                            for w in (expert_w_gate, expert_w_up, expert_w_down))
            xt, experts_bf16 = dense_ffn(xt, h, dense_w_gate[j].astype(BF16), dense_w_up[j].astype(BF16),
                                         dense_w_down[j].astype(BF16), cast=experts)
            if last:
                xt = final_norm(xt, final_norm_w)
        else:
            xt, h, route, counts = out_proj(xt, y_a, y_b, w_o, l, ffn_norm_w[l], router_w=router_w[j])
            wg, wu, wd = (w.reshape(s.shape[1:]) for w, s in
                          zip(experts_bf16, (expert_w_gate, expert_w_up, expert_w_down)))
            xt = moe_ffn(xt, h, route, counts, wg, wu, wd, final_norm_w, final_norm=last)
    return xt.reshape(batch, seq, d)
```

```python
import functools

import jax
import jax.numpy as jnp
from jax import lax
from jax.experimental import pallas as pl
from jax.experimental.pallas import tpu as pltpu

F32 = jnp.float32
BF16 = jnp.bfloat16

GMLP_HEAD = 128
GMLP_CHUNK = 128
HGRN_HEAD = 128
N_EXPERTS = 8
TOP_K = 2
EPS = 1e-6
F_MIN = 1e-6

LANES = 128
GMLP_GROUP = 8
HGRN_C = 128
HGRN_GROUP = 8
HGRN_DIAG = 8
HGRN_DIAG_REF = 3
VMEM_LIMIT = 56 * 1024 * 1024


def _cparams(sem, vmem=VMEM_LIMIT):
    return pltpu.CompilerParams(dimension_semantics=sem, vmem_limit_bytes=vmem)


def _silu(x):
    hx = 0.5 * x
    return hx + hx * jnp.tanh(hx)


GELU_C1 = 0.7978845608028654
GELU_C2 = GELU_C1 * 0.044715


def _gelu_tanh(x):
    hx = 0.5 * x
    return hx + hx * jnp.tanh(x * (GELU_C1 + GELU_C2 * (x * x)))


def _rms(x, w):
    return x * lax.rsqrt(jnp.mean(x * x, axis=-1, keepdims=True) + EPS) * w


def _inproj_kernel(x_ref, nw_ref, w_ref, z_ref, h_scr):
    @pl.when(pl.program_id(1) == 0)
    def _():
        h_scr[...] = _rms(x_ref[...], nw_ref[...]).astype(BF16)

    z_ref[...] = jnp.dot(h_scr[...], w_ref[...], preferred_element_type=F32)


def in_proj(x, norm_w, w, bm=1024, bn=1536):
    t, d = x.shape
    n = w.shape[1]
    bm = min(bm, t)
    return pl.pallas_call(
        _inproj_kernel,
        grid=(t // bm, n // bn),
        in_specs=[
            pl.BlockSpec((bm, d), lambda i, j: (i, 0)),
            pl.BlockSpec((1, d), lambda i, j: (0, 0)),
            pl.BlockSpec((d, bn), lambda i, j: (0, j)),
        ],
        out_specs=pl.BlockSpec((bm, bn), lambda i, j: (i, j)),
        out_shape=jax.ShapeDtypeStruct((t, n), F32),
        scratch_shapes=[pltpu.VMEM((bm, d), BF16)],
        compiler_params=_cparams(("parallel", "arbitrary")),
        name="in_proj",
    )(x, norm_w.reshape(1, d), w)


def _gmlp_kernel(zu_ref, zv_ref, ws_ref, bs_ref, lnw_ref, lnb_ref, ow_ref, y_ref, wc_scr, *, n_heads, n_chunks):
    c = GMLP_CHUNK
    row = lax.broadcasted_iota(jnp.int32, (c, c), 0)
    col = lax.broadcasted_iota(jnp.int32, (c, c), 1)
    causal = row >= col
    ones = jnp.ones((GMLP_HEAD, GMLP_HEAD), BF16)
    lane_mean = lambda x: jnp.dot(x.astype(BF16), ones, preferred_element_type=F32) * (1.0 / GMLP_HEAD)

    for h in range(n_heads):
        wc_scr[h] = jnp.where(causal, ws_ref[h], 0.0).astype(BF16)

    def chunk(ci, carry):
        r0 = pl.multiple_of(ci * c, c)
        for g0 in range(0, n_heads, GMLP_GROUP):
            heads = range(g0, g0 + GMLP_GROUP)
            sls = [slice(h * GMLP_HEAD, (h + 1) * GMLP_HEAD) for h in heads]
            v = [_gelu_tanh(zv_ref[pl.ds(r0, c), sl]) for sl in sls]
            mu = [lane_mean(x) for x in v]
            vc = [x - m for x, m in zip(v, mu)]
            var = [lane_mean(x * x) for x in vc]
            vn = [(x * lax.rsqrt(s + EPS) * lnw_ref[:, sl] + lnb_ref[:, sl]).astype(BF16)
                  for x, s, sl in zip(vc, var, sls)]
            sv = [jnp.dot(wc_scr[h], x, preferred_element_type=F32) + bs_ref[:, h:h + 1] for h, x in zip(heads, vn)]
            y = [_gelu_tanh(zu_ref[pl.ds(r0, c), sl]) * x for sl, x in zip(sls, sv)]
            ms = [lane_mean(x * x) for x in y]
            for x, s, sl in zip(y, ms, sls):
                y_ref[pl.ds(r0, c), sl] = (x * lax.rsqrt(s + EPS) * ow_ref[:, sl]).astype(BF16)
        return carry

    lax.fori_loop(0, n_chunks, chunk, 0)


def gmlp_mixer(z, w_s, b_s, ln_w, ln_b, out_w, rows=512):
    t = z.shape[0]
    n_heads = w_s.shape[0]
    dg = n_heads * GMLP_HEAD
    rows = min(rows, t)
    kern = functools.partial(_gmlp_kernel, n_heads=n_heads, n_chunks=rows // GMLP_CHUNK)
    vec = lambda a: a.reshape(1, dg)
    return pl.pallas_call(
        kern,
        grid=(t // rows,),
        in_specs=[
            pl.BlockSpec((rows, dg), lambda i: (i, 0)),
            pl.BlockSpec((rows, dg), lambda i: (i, 1)),
            pl.BlockSpec((n_heads, GMLP_CHUNK, GMLP_CHUNK), lambda i: (0, 0, 0)),
            pl.BlockSpec((GMLP_CHUNK, n_heads), lambda i: (0, 0)),
            pl.BlockSpec((1, dg), lambda i: (0, 0)),
            pl.BlockSpec((1, dg), lambda i: (0, 0)),
            pl.BlockSpec((1, dg), lambda i: (0, 0)),
        ],
        out_specs=pl.BlockSpec((rows, dg), lambda i: (i, 0)),
        out_shape=jax.ShapeDtypeStruct((t, dg), BF16),
        scratch_shapes=[pltpu.VMEM((n_heads, GMLP_CHUNK, GMLP_CHUNK), BF16)],
        compiler_params=_cparams(("parallel",)),
        name="gmlp_mixer",
    )(z, z, w_s, b_s.T, vec(ln_w), vec(ln_b), vec(out_w))


def _hgrn_level_terms(a, q, k, blk):
    c, d = a.shape
    half = blk // 2
    nb = c // blk
    lower = lambda v: v.reshape(nb, blk, d)[:, :half, :]
    upper = lambda v: v.reshape(nb, blk, d)[:, half:, :]
    ref = a.reshape(nb, blk, d)[:, half - 1:half, :]
    qh = upper(q) * jnp.exp(upper(a) - ref)
    kh = lower(k) * jnp.exp(ref - lower(a))
    zeros = jnp.zeros_like(qh)
    qh = jnp.concatenate([zeros, qh], axis=1).reshape(c, d)
    kh = jnp.concatenate([kh, zeros], axis=1).reshape(c, d)
    return qh.astype(BF16), kh.astype(BF16)


def _hgrn_kernel(zq_ref, zf_ref, zi_ref, zg_ref, lb_ref, ow_ref, y_ref, st_ref, *, n_heads, n_chunks):
    c, d = HGRN_C, HGRN_HEAD
    nt = (((1,), (1,)), ((), ()))
    tn = (((0,), (0,)), ((), ()))

    @pl.when(pl.program_id(1) == 0)
    def _():
        st_ref[...] = jnp.zeros_like(st_ref)

    row = lax.broadcasted_iota(jnp.int32, (c, c), 0)
    col = lax.broadcasted_iota(jnp.int32, (c, c), 1)
    tril = (row >= col).astype(BF16)
    diag_mask = (row >= col) & ((row // HGRN_DIAG) == (col // HGRN_DIAG))
    level_masks = []
    blk = 2 * HGRN_DIAG
    while blk <= c:
        level_masks.append((blk, None if blk == c else (row // blk) == (col // blk)))
        blk *= 2

    def nt_dot(x, y):
        return lax.dot_general(x, y, nt, preferred_element_type=F32)

    def chunk(ci, carry):
        r0 = pl.multiple_of(ci * c, c)
        for g0 in range(0, n_heads, HGRN_GROUP):
            heads = list(range(g0, g0 + HGRN_GROUP))
            sls = [slice(h * d, (h + 1) * d) for h in heads]
            f = [0.5 * (1.0 + lb_ref[:, sl]) + (0.5 * (1.0 - lb_ref[:, sl])) * jnp.tanh(0.5 * zf_ref[pl.ds(r0, c), sl])
                 for sl in sls]
            logf = [jnp.log(jnp.clip(x, F_MIN, 1.0)) for x in f]
            pieces = [jnp.concatenate(_split_bf16(x), axis=1) for x in logf]
            cs = [jnp.dot(tril, x, preferred_element_type=F32) for x in pieces]
            a = [x[:, :d] + x[:, d:] for x in cs]
            k = [1.0 - x for x in f]
            q = [_silu(zq_ref[pl.ds(r0, c), sl]) for sl in sls]
            iv = [zi_ref[pl.ds(r0, c), sl].astype(BF16) for sl in sls]

            dd = []
            for x in a:
                a3 = x.reshape(c // HGRN_DIAG, HGRN_DIAG, d)
                dd.append((a3 - a3[:, HGRN_DIAG_REF:HGRN_DIAG_REF + 1, :]).reshape(c, d))
            sd = [nt_dot((qx * jnp.exp(x)).astype(BF16), (kx * jnp.exp(-x)).astype(BF16))
                  for qx, kx, x in zip(q, k, dd)]

            scores = None
            for blk, same_block in reversed(level_masks):
                terms = [_hgrn_level_terms(ax, qx, kx, blk) for ax, qx, kx in zip(a, q, k)]
                prods = [nt_dot(qh, kh) for qh, kh in terms]
                scores = prods if scores is None else [jnp.where(same_block, p, s) for s, p in zip(scores, prods)]
            scores = [jnp.where(diag_mask, x, s) for x, s in zip(sd, scores)]

            st = [st_ref[h] for h in heads]
            qe = [(qx * jnp.exp(ax)).astype(BF16) for qx, ax in zip(q, a)]
            o = [jnp.dot(s.astype(BF16), ivx, preferred_element_type=F32) + nt_dot(qx, sx.astype(BF16))
                 for s, ivx, qx, sx in zip(scores, iv, qe, st)]
            kd = [(kx * jnp.exp(ax[c - 1:c, :] - ax)).astype(BF16) for kx, ax in zip(k, a)]
            for h, ax, sx, ivx, kx in zip(heads, a, st, iv, kd):
                st_ref[h] = jnp.exp(ax[c - 1:c, :]) * sx + lax.dot_general(ivx, kx, tn, preferred_element_type=F32)
            for x, sl in zip(o, sls):
                y_ref[pl.ds(r0, c), sl] = (_rms(x, ow_ref[:, sl]) * _silu(zg_ref[pl.ds(r0, c), sl])).astype(BF16)
        return carry

    lax.fori_loop(0, n_chunks, chunk, 0)


def hgrn2_mixer(z, lb, out_w, batch, col0, rows=512):
    t = z.shape[0]
    s = t // batch
    dh = lb.shape[0]
    n_heads = dh // HGRN_HEAD
    rows = min(rows, s)
    spb = s // rows
    kern = functools.partial(_hgrn_kernel, n_heads=n_heads, n_chunks=rows // HGRN_C)
    zspec = lambda j: pl.BlockSpec((rows, dh), lambda b, i, j=j: (b * spb + i, col0 + j))
    return pl.pallas_call(
        kern,
        grid=(batch, spb),
        in_specs=[zspec(0), zspec(1), zspec(2), zspec(3),
                  pl.BlockSpec((1, dh), lambda b, i: (0, 0)),
                  pl.BlockSpec((1, dh), lambda b, i: (0, 0))],
        out_specs=pl.BlockSpec((rows, dh), lambda b, i: (b * spb + i, 0)),
        out_shape=jax.ShapeDtypeStruct((t, dh), BF16),
        scratch_shapes=[pltpu.VMEM((n_heads, HGRN_HEAD, HGRN_HEAD), F32)],
        compiler_params=_cparams(("parallel", "arbitrary")),
        name="hgrn2_mixer",
    )(z, z, z, z, lb.reshape(1, dh), out_w.reshape(1, dh))


ROUTE_E0, ROUTE_E1, ROUTE_W0, ROUTE_W1, ROUTE_R0, ROUTE_R1 = range(6)


def _split_bf16(v):
    hi = v.astype(BF16)
    return hi, (v - hi.astype(F32)).astype(BF16)


def _route_rows(h, rw_pieces, carry_ref):
    bm = h.shape[0]
    h_hi, h_lo = _split_bf16(h)
    hh_hl = jnp.dot(h_hi, rw_pieces, preferred_element_type=F32)
    logits = hh_hl[:, :LANES] + hh_hl[:, LANES:] + jnp.dot(h_lo, rw_pieces[:, :LANES], preferred_element_type=F32)
    lane = lax.broadcasted_iota(jnp.int32, (bm, LANES), 1)
    neg = jnp.float32(-jnp.inf)
    logits = jnp.where(lane < N_EXPERTS, logits, neg)
    v0 = jnp.max(logits, axis=-1, keepdims=True)
    e0 = jnp.min(jnp.where(logits == v0, lane, LANES), axis=-1, keepdims=True)
    rest = jnp.where(lane == e0, neg, logits)
    v1 = jnp.max(rest, axis=-1, keepdims=True)
    e1 = jnp.min(jnp.where(rest == v1, lane, LANES), axis=-1, keepdims=True)
    ex = jnp.exp(v1 - v0)
    w0 = 1.0 / (1.0 + ex)
    w1 = ex / (1.0 + ex)

    onehot = ((lane == e0) | (lane == e1)).astype(F32)
    row = lax.broadcasted_iota(jnp.int32, (bm, bm), 0)
    col = lax.broadcasted_iota(jnp.int32, (bm, bm), 1)
    before = (row > col).astype(BF16)
    excl = jnp.dot(before, onehot.astype(BF16), preferred_element_type=F32) + carry_ref[...]
    r0 = jnp.sum(jnp.where(lane == e0, excl, 0.0), axis=-1, keepdims=True)
    r1 = jnp.sum(jnp.where(lane == e1, excl, 0.0), axis=-1, keepdims=True)
    carry_ref[...] += jnp.sum(onehot, axis=0, keepdims=True)

    out = jnp.zeros((bm, LANES), F32)
    for idx, val in ((ROUTE_E0, e0.astype(F32)), (ROUTE_E1, e1.astype(F32)), (ROUTE_W0, w0), (ROUTE_W1, w1),
                     (ROUTE_R0, r0), (ROUTE_R1, r1)):
        out = jnp.where(lane == idx, val, out)
    return out


def _outproj_kernel(*refs, da, moe):
    if moe:
        (x_ref, ya_ref, yb_ref, wo_ref, nw_ref, rw_ref, xo_ref, h_ref, route_ref, cnt_ref,
         wo_scr, rw_scr, carry_scr) = refs
    else:
        x_ref, ya_ref, yb_ref, wo_ref, nw_ref, xo_ref, h_ref, wo_scr = refs

    @pl.when(pl.program_id(0) == 0)
    def _():
        wo_scr[...] = wo_ref[...].astype(BF16)
        if moe:
            rw_hi, rw_lo = _split_bf16(rw_ref[...])
            rw_scr[:, :LANES] = rw_hi
            rw_scr[:, LANES:] = rw_lo
            carry_scr[...] = jnp.zeros_like(carry_scr)

    acc = x_ref[...] + jnp.dot(ya_ref[...], wo_scr[:da, :], preferred_element_type=F32) \
        + jnp.dot(yb_ref[...], wo_scr[da:, :], preferred_element_type=F32)
    xo_ref[...] = acc
    h = _rms(acc, nw_ref[...])
    h_ref[...] = h.astype(h_ref.dtype)
    if moe:
        route_ref[...] = _route_rows(h, rw_scr[...], carry_scr)
        cnt_ref[...] = carry_scr[...]


def out_proj(x, y_a, y_b, w_o, layer, norm_w, router_w=None, bm=None):
    t, d = x.shape
    da, db = y_a.shape[1], y_b.shape[1]
    moe = router_w is not None
    bm = min(bm or (256 if moe else 512), t)
    row_spec = lambda w: pl.BlockSpec((bm, w), lambda i: (i, 0))
    const = lambda shape: pl.BlockSpec(shape, lambda i: (0,) * len(shape), pipeline_mode=pl.Buffered(1))
    w_spec = pl.BlockSpec((None, da + db, d), lambda i: (layer, 0, 0), pipeline_mode=pl.Buffered(1))
    in_specs = [row_spec(d), row_spec(da), row_spec(db), w_spec, const((1, d))]
    out_specs = [row_spec(d), row_spec(d)]
    out_shape = [jax.ShapeDtypeStruct((t, d), F32), jax.ShapeDtypeStruct((t, d), F32 if moe else BF16)]
    scratch = [pltpu.VMEM((da + db, d), BF16)]
    args = [x, y_a, y_b, w_o, norm_w.reshape(1, d)]
    if moe:
        in_specs.append(const((d, LANES)))
        args.append(jnp.zeros((d, LANES), F32).at[:, :N_EXPERTS].set(router_w))
        out_specs += [row_spec(LANES), pl.BlockSpec((1, LANES), lambda i: (0, 0))]
        out_shape += [jax.ShapeDtypeStruct((t, LANES), F32), jax.ShapeDtypeStruct((1, LANES), F32)]
        scratch += [pltpu.VMEM((d, 2 * LANES), BF16), pltpu.VMEM((1, LANES), F32)]
    return pl.pallas_call(
        functools.partial(_outproj_kernel, da=da, moe=moe),
        grid=(t // bm,),
        in_specs=in_specs,
        out_specs=out_specs,
        out_shape=out_shape,
        scratch_shapes=scratch,
        compiler_params=_cparams(("arbitrary",)),
        name="out_proj_route" if moe else "out_proj",
    )(*args)


def _ffn_kernel(x_hbm, h_ref, wg_ref, wu_ref, wd_ref, *rest, n_pass):
    n_cast = (len(rest) - 2) // 2
    o_ref, sem = rest[n_cast], rest[-1]
    i, j, k = pl.program_id(0), pl.program_id(1), pl.program_id(2)
    bm = o_ref.shape[0]
    sub = bm // n_pass

    def x_copy(p):
        src = x_hbm.at[pl.ds(pl.multiple_of(i * bm + p * sub, sub), sub)]
        return pltpu.make_async_copy(src, o_ref.at[pl.ds(p * sub, sub)], sem.at[p])

    @pl.when((j == 0) & (k == 0))
    def _():
        for p in range(n_pass):
            x_copy(p).start()

    for p in range(n_pass):
        @pl.when((j == 0) & (k == p))
        def _():
            x_copy(p).wait()

    for src_ref, dst_ref in zip(rest[:n_cast], rest[n_cast + 1:-1]):
        dst_ref[...] = src_ref[...].astype(BF16)

    rows = pl.ds(pl.multiple_of(k * sub, sub), sub)
    hb = h_ref[rows, :]
    g = jnp.dot(hb, wg_ref[...], preferred_element_type=F32)
    u = jnp.dot(hb, wu_ref[...], preferred_element_type=F32)
    act = (_silu(g) * u).astype(BF16)
    o_ref[rows, :] += jnp.dot(act, wd_ref[...], preferred_element_type=F32)


def dense_ffn(x, h, wg, wu, wd, cast=(), bm=1024, bf=512, n_pass=2):
    t, d = x.shape
    f = wg.shape[1]
    bm = min(bm, t)
    ni, nj = t // bm, f // bf
    cast_specs = []
    for w in cast:
        r, c = w.shape
        if c % nj == 0 and (c // nj) % LANES == 0 and r % (ni * n_pass) == 0:
            cast_specs.append(pl.BlockSpec((r // (ni * n_pass), c // nj), lambda i, j, k: (i * n_pass + k, j)))
        else:
            assert r % (ni * nj * n_pass) == 0, (w.shape, ni, nj, n_pass)
            cast_specs.append(pl.BlockSpec((r // (ni * nj * n_pass), c),
                                           lambda i, j, k: ((i * nj + j) * n_pass + k, 0)))
    outs = pl.pallas_call(
        functools.partial(_ffn_kernel, n_pass=n_pass),
        grid=(ni, nj, n_pass),
        in_specs=[
            pl.BlockSpec(memory_space=pl.ANY),
            pl.BlockSpec((bm, d), lambda i, j, k: (i, 0)),
            pl.BlockSpec((d, bf), lambda i, j, k: (0, j)),
            pl.BlockSpec((d, bf), lambda i, j, k: (0, j)),
            pl.BlockSpec((bf, d), lambda i, j, k: (j, 0)),
        ] + cast_specs,
        out_specs=[pl.BlockSpec((bm, d), lambda i, j, k: (i, 0))] + cast_specs,
        out_shape=[jax.ShapeDtypeStruct((t, d), F32)] + [jax.ShapeDtypeStruct(w.shape, BF16) for w in cast],
        scratch_shapes=[pltpu.SemaphoreType.DMA((n_pass,))],
        compiler_params=_cparams(("parallel", "arbitrary", "arbitrary")),
        name="dense_ffn",
    )(x, h, wg, wu, wd, *cast)
    return outs[0], tuple(outs[1:])


def _row_copy(src_ref, dst_ref, sem, src_row, dst_row):
    return pltpu.make_async_copy(src_ref.at[pl.ds(src_row, 1)], dst_ref.at[pl.ds(dst_row, 1)], sem)


ROW_DMA_UNROLL = 8
GMM_TILE = 1024
GMM_SUBTILES = 2


def _dispatch_kernel(zlo_ref, zhi_ref, pos_ref, h_ref, xs_ref, zero_scr, sem, zsem):
    bt = h_ref.shape[0]

    @pl.when(pl.program_id(0) == 0)
    def _():
        zero_scr[...] = jnp.zeros_like(zero_scr)
        zr = zero_scr.shape[0]

        def zero_rows(e, begin):
            lo, hi = zlo_ref[e], zhi_ref[e]
            mid = jnp.minimum(hi, (lo + zr - 1) // zr * zr)

            def one(r, carry):
                cp = _row_copy(zero_scr, xs_ref, zsem, 0, r)
                cp.start() if begin else cp.wait()
                return carry

            def slab(b, carry):
                cp = pltpu.make_async_copy(zero_scr, xs_ref.at[pl.ds(pl.multiple_of(b * zr, zr), zr)], zsem)
                cp.start() if begin else cp.wait()
                return carry

            lax.fori_loop(lo, mid, one, 0)
            lax.fori_loop(mid // zr, hi // zr, slab, 0)

        for e in range(N_EXPERTS):
            zero_rows(e, True)
        for e in range(N_EXPERTS):
            zero_rows(e, False)

    def start(r, carry):
        for kk in range(TOP_K):
            _row_copy(h_ref, xs_ref, sem, r, pos_ref[0, TOP_K * r + kk]).start()
        return carry

    lax.fori_loop(0, bt, start, 0, unroll=ROW_DMA_UNROLL)
    for kk in range(TOP_K):
        pltpu.make_async_copy(h_ref, xs_ref.at[pl.ds(0, bt)], sem).wait()


def moe_dispatch(h, pos, zero_lo, zero_hi, n_rows, bt=256):
    t, d = h.shape
    bt = min(bt, t)
    pos2 = pos.reshape(t // bt, 1, TOP_K * bt)
    return pl.pallas_call(
        _dispatch_kernel,
        grid_spec=pltpu.PrefetchScalarGridSpec(
            num_scalar_prefetch=2,
            grid=(t // bt,),
            in_specs=[
                pl.BlockSpec((None, 1, TOP_K * bt), lambda i, zl, zh: (i, 0, 0), memory_space=pltpu.SMEM),
                pl.BlockSpec((bt, d), lambda i, zl, zh: (i, 0)),
            ],
            out_specs=pl.BlockSpec(memory_space=pl.ANY),
            scratch_shapes=[pltpu.VMEM((8, d), F32), pltpu.SemaphoreType.DMA(()), pltpu.SemaphoreType.DMA(())],
        ),
        out_shape=jax.ShapeDtypeStruct((n_rows, d), F32),
        compiler_params=_cparams(("arbitrary",)),
        name="moe_dispatch",
    )(zero_lo, zero_hi, pos2, h)


def _gmm_kernel(te_ref, tr_ref, nv_ref, xs_ref, wg_ref, wu_ref, wd_ref, ys_ref, xb_scr):
    i, j = pl.program_id(0), pl.program_id(1)
    bm = xs_ref.shape[0]
    rows = tr_ref[i]
    sub = bm // GMM_SUBTILES

    @pl.when(j == 0)
    def _():
        xb_scr[...] = xs_ref[...].astype(BF16)
        ys_ref[...] = jnp.zeros_like(ys_ref)

    def ffn(r0, n):
        xb = xb_scr[r0:r0 + n, :]
        g = jnp.dot(xb, wg_ref[...], preferred_element_type=F32)
        u = jnp.dot(xb, wu_ref[...], preferred_element_type=F32)
        act = (_silu(g) * u).astype(BF16)
        ys_ref[r0:r0 + n, :] += jnp.dot(act, wd_ref[...], preferred_element_type=F32)

    full = rows == bm

    @pl.when(full)
    def _():
        for s in range(GMM_SUBTILES):
            ffn(s * sub, sub)

    for s in range(GMM_SUBTILES):
        left = rows - s * sub

        @pl.when(jnp.logical_not(full) & (left > sub // 2))
        def _():
            ffn(s * sub, sub)

        @pl.when((left > 0) & (left <= sub // 2))
        def _():
            ffn(s * sub, sub // 2)


def moe_grouped_ffn(xs, tile_expert, tile_rows, n_valid, wg, wu, wd, bm, bf=512):
    p, d = xs.shape
    f = wg.shape[2]
    nf = f // bf

    def row_map(i, j, te, tr, nv):
        return (jnp.minimum(i, nv[0] - 1), 0)

    def fcol(i, j, nv):
        return jnp.where(i < nv[0], j, nf - 1)

    return pl.pallas_call(
        _gmm_kernel,
        grid_spec=pltpu.PrefetchScalarGridSpec(
            num_scalar_prefetch=3,
            grid=(p // bm, nf),
            in_specs=[
                pl.BlockSpec((bm, d), row_map),
                pl.BlockSpec((None, d, bf), lambda i, j, te, tr, nv: (te[i], 0, fcol(i, j, nv))),
                pl.BlockSpec((None, d, bf), lambda i, j, te, tr, nv: (te[i], 0, fcol(i, j, nv))),
                pl.BlockSpec((None, bf, d), lambda i, j, te, tr, nv: (te[i], fcol(i, j, nv), 0)),
            ],
            out_specs=pl.BlockSpec((bm, d), lambda i, j, te, tr, nv: (i, 0)),
            scratch_shapes=[pltpu.VMEM((bm, d), BF16)],
        ),
        out_shape=jax.ShapeDtypeStruct((p, d), F32),
        compiler_params=_cparams(("arbitrary", "arbitrary")),
        name="moe_grouped_ffn",
    )(tile_expert, tile_rows, n_valid, xs, wg, wu, wd)


def _combine_kernel(pos_ref, pos_next_ref, x_ref, route_ref, nw_ref, ys_ref, o_ref, y_scr, sem, *, final_norm):
    bt = x_ref.shape[0]
    i = pl.program_id(0)
    slot = i % 2

    def gather(p_ref, s, unroll):
        def start(r, carry):
            for kk in range(TOP_K):
                _row_copy(ys_ref, y_scr.at[s, kk], sem.at[s], p_ref[0, TOP_K * r + kk], r).start()
            return carry
        lax.fori_loop(0, bt, start, 0, unroll=unroll)

    def wait(s):
        for kk in range(TOP_K):
            pltpu.make_async_copy(ys_ref.at[pl.ds(0, bt)], y_scr.at[s, kk], sem.at[s]).wait()

    @pl.when(i == 0)
    def _():
        gather(pos_ref, 0, ROW_DMA_UNROLL)

    wait(slot)
    gather(pos_next_ref, 1 - slot, True)
    w0 = route_ref[:, ROUTE_W0:ROUTE_W0 + 1]
    w1 = route_ref[:, ROUTE_W1:ROUTE_W1 + 1]
    out = x_ref[...] + (w0 * y_scr[slot, 0] + w1 * y_scr[slot, 1])
    if final_norm:
        out = _rms(out, nw_ref[...])
    o_ref[...] = out

    @pl.when(i + 1 == pl.num_programs(0))
    def _():
        wait(1 - slot)


def moe_combine(x, route, pos, ys, norm_w, final_norm, bt=256):
    t, d = x.shape
    bt = min(bt, t)
    n = t // bt
    pos2 = pos.reshape(n, 1, TOP_K * bt)
    pos_spec = lambda nxt: pl.BlockSpec((None, 1, TOP_K * bt), lambda i: (jnp.minimum(i + nxt, n - 1), 0, 0),
                                        memory_space=pltpu.SMEM)
    return pl.pallas_call(
        functools.partial(_combine_kernel, final_norm=final_norm),
        grid=(n,),
        in_specs=[
            pos_spec(0),
            pos_spec(1),
            pl.BlockSpec((bt, d), lambda i: (i, 0)),
            pl.BlockSpec((bt, LANES), lambda i: (i, 0)),
            pl.BlockSpec((1, d), lambda i: (0, 0)),
            pl.BlockSpec(memory_space=pl.ANY),
        ],
        out_specs=pl.BlockSpec((bt, d), lambda i: (i, 0)),
        out_shape=jax.ShapeDtypeStruct((t, d), F32),
        scratch_shapes=[pltpu.VMEM((2, TOP_K, bt, d), F32), pltpu.SemaphoreType.DMA((2,))],
        compiler_params=_cparams(("arbitrary",)),
        name="moe_combine",
    )(pos2, pos2, x, route, norm_w.reshape(1, d), ys)


def moe_ffn(x, h, route, counts, wg, wu, wd, norm_w, final_norm, bm=GMM_TILE):
    t, d = x.shape

    cnt = counts[0, :N_EXPERTS].astype(jnp.int32)
    tiles = (cnt + bm - 1) // bm
    tile_end = jnp.cumsum(tiles)
    offset = (tile_end - tiles) * bm
    n_tiles = (TOP_K * t) // bm + N_EXPERTS
    n_valid = tile_end[-1:].astype(jnp.int32)
    tile_id = jnp.minimum(jnp.arange(n_tiles, dtype=jnp.int32), n_valid[0] - 1)
    tile_expert = jnp.sum((tile_end[None, :] <= tile_id[:, None]).astype(jnp.int32), axis=1)
    tile_in_group = tile_id - (tile_end - tiles)[tile_expert]
    tile_rows = jnp.clip(cnt[tile_expert] - tile_in_group * bm, 0, bm)
    tile_rows = jnp.where(jnp.arange(n_tiles) < n_valid[0], tile_rows, 0).astype(jnp.int32)
    experts = route[:, ROUTE_E0:ROUTE_E1 + 1].astype(jnp.int32)
    ranks = route[:, ROUTE_R0:ROUTE_R1 + 1].astype(jnp.int32)
    pos = offset[experts] + ranks
    zero_lo = offset + cnt
    zero_hi = jnp.concatenate([offset[1:], jnp.full((1,), n_tiles * bm, jnp.int32)])

    xs = moe_dispatch(h, pos, zero_lo, zero_hi, n_tiles * bm)
    ys = moe_grouped_ffn(xs, tile_expert, tile_rows, n_valid, wg, wu, wd, bm)
    return moe_combine(x, route, pos, ys, norm_w, final_norm)


def _norm_kernel(x_ref, nw_ref, o_ref):
    o_ref[...] = _rms(x_ref[...], nw_ref[...])


def final_norm(x, norm_w, bm=512):
    t, d = x.shape
    bm = min(bm, t)
    return pl.pallas_call(
        _norm_kernel,
        grid=(t // bm,),
        in_specs=[pl.BlockSpec((bm, d), lambda i: (i, 0)), pl.BlockSpec((1, d), lambda i: (0, 0))],
        out_specs=pl.BlockSpec((bm, d), lambda i: (i, 0)),
        out_shape=jax.ShapeDtypeStruct((t, d), F32),
        compiler_params=_cparams(("parallel",)),
        name="final_norm",
    )(x, norm_w.reshape(1, d))


def kernel(x, mix_norm_w, w_in, gmlp_w_s, gmlp_b_s, gmlp_ln_w, gmlp_ln_b, gmlp_out_w, hgrn_lb_logits, hgrn_out_w, w_o, ffn_norm_w, dense_w_gate, dense_w_up, dense_w_down, router_w, expert_w_gate, expert_w_up, expert_w_down, final_norm_w):
    batch, seq, d = x.shape
    depth = w_in.shape[0]
    d_gmlp = gmlp_ln_w.shape[1]
    d_hgrn = hgrn_out_w.shape[1]
    assert d_gmlp == d_hgrn and seq % HGRN_C == 0 and seq % GMLP_CHUNK == 0

    lbs = jax.nn.softmax(hgrn_lb_logits.astype(F32), axis=0)
    lbs = jnp.cumsum(lbs, axis=0) - lbs[0:1]

    xt = x.reshape(batch * seq, d)
    for l in range(depth):
        z = in_proj(xt, mix_norm_w[l], w_in[l].astype(BF16))
        y_a = gmlp_mixer(z, gmlp_w_s[l], gmlp_b_s[l], gmlp_ln_w[l], gmlp_ln_b[l], gmlp_out_w[l])
        y_b = hgrn2_mixer(z, lbs[l], hgrn_out_w[l], batch, col0=2 * d_gmlp // d_hgrn)
        last = l == depth - 1
        j = l // 2
        if l % 2 == 0:
            xt, h = out_proj(xt, y_a, y_b, w_o, l, ffn_norm_w[l])
            experts = () if last else tuple(w[j].reshape(-1, w.shape[-1])
                                            for w in (expert_w_gate, expert_w_up, expert_w_down))
            xt, experts_bf16 = dense_ffn(xt, h, dense_w_gate[j].astype(BF16), dense_w_up[j].astype(BF16),
                                         dense_w_down[j].astype(BF16), cast=experts)
            if last:
                xt = final_norm(xt, final_norm_w)
        else:
            xt, h, route, counts = out_proj(xt, y_a, y_b, w_o, l, ffn_norm_w[l], router_w=router_w[j])
            wg, wu, wd = (w.reshape(s.shape[1:]) for w, s in
                          zip(experts_bf16, (expert_w_gate, expert_w_up, expert_w_down)))
            xt = moe_ffn(xt, h, route, counts, wg, wu, wd, final_norm_w, final_norm=last)
    return xt.reshape(batch, seq, d)
```

```python
import functools

import jax
import jax.numpy as jnp
from jax import lax
from jax.experimental import pallas as pl
from jax.experimental.pallas import tpu as pltpu

F32 = jnp.float32
BF16 = jnp.bfloat16

GMLP_HEAD = 128
GMLP_CHUNK = 128
HGRN_HEAD = 128
N_EXPERTS = 8
TOP_K = 2
EPS = 1e-6
F_MIN = 1e-6

LANES = 128
GMLP_GROUP = 8
HGRN_C = 128
HGRN_GROUP = 8
HGRN_DIAG = 8
HGRN_DIAG_REF = 3
VMEM_LIMIT = 56 * 1024 * 1024


def _cparams(sem, vmem=VMEM_LIMIT):
    return pltpu.CompilerParams(dimension_semantics=sem, vmem_limit_bytes=vmem)


def _silu(x):
    hx = 0.5 * x
    return hx + hx * jnp.tanh(hx)


GELU_C1 = 0.7978845608028654
GELU_C2 = GELU_C1 * 0.044715


def _gelu_tanh(x):
    hx = 0.5 * x
    return hx + hx * jnp.tanh(x * (GELU_C1 + GELU_C2 * (x * x)))


def _rms(x, w):
    return x * lax.rsqrt(jnp.mean(x * x, axis=-1, keepdims=True) + EPS) * w


def _inproj_kernel(x_ref, nw_ref, w_ref, z_ref, h_scr):
    @pl.when(pl.program_id(1) == 0)
    def _():
        h_scr[...] = _rms(x_ref[...], nw_ref[...]).astype(BF16)

    z_ref[...] = jnp.dot(h_scr[...], w_ref[...], preferred_element_type=F32)


def in_proj(x, norm_w, w, bm=1024, bn=1536):
    t, d = x.shape
    n = w.shape[1]
    bm = min(bm, t)
    return pl.pallas_call(
        _inproj_kernel,
        grid=(t // bm, n // bn),
        in_specs=[
            pl.BlockSpec((bm, d), lambda i, j: (i, 0)),
            pl.BlockSpec((1, d), lambda i, j: (0, 0)),
            pl.BlockSpec((d, bn), lambda i, j: (0, j)),
        ],
        out_specs=pl.BlockSpec((bm, bn), lambda i, j: (i, j)),
        out_shape=jax.ShapeDtypeStruct((t, n), F32),
        scratch_shapes=[pltpu.VMEM((bm, d), BF16)],
        compiler_params=_cparams(("parallel", "arbitrary")),
        name="in_proj",
    )(x, norm_w.reshape(1, d), w)


def _gmlp_kernel(zu_ref, zv_ref, ws_ref, bs_ref, lnw_ref, lnb_ref, ow_ref, y_ref, wc_scr, *, n_heads, n_chunks):
    c = GMLP_CHUNK
    row = lax.broadcasted_iota(jnp.int32, (c, c), 0)
    col = lax.broadcasted_iota(jnp.int32, (c, c), 1)
    causal = row >= col
    ones = jnp.ones((GMLP_HEAD, GMLP_HEAD), BF16)
    lane_mean = lambda x: jnp.dot(x.astype(BF16), ones, preferred_element_type=F32) * (1.0 / GMLP_HEAD)

    for h in range(n_heads):
        wc_scr[h] = jnp.where(causal, ws_ref[h], 0.0).astype(BF16)

    def chunk(ci, carry):
        r0 = pl.multiple_of(ci * c, c)
        for g0 in range(0, n_heads, GMLP_GROUP):
            heads = range(g0, g0 + GMLP_GROUP)
            sls = [slice(h * GMLP_HEAD, (h + 1) * GMLP_HEAD) for h in heads]
            v = [_gelu_tanh(zv_ref[pl.ds(r0, c), sl]) for sl in sls]
            mu = [lane_mean(x) for x in v]
            vc = [x - m for x, m in zip(v, mu)]
            var = [lane_mean(x * x) for x in vc]
            vn = [(x * lax.rsqrt(s + EPS) * lnw_ref[:, sl] + lnb_ref[:, sl]).astype(BF16)
                  for x, s, sl in zip(vc, var, sls)]
            sv = [jnp.dot(wc_scr[h], x, preferred_element_type=F32) + bs_ref[:, h:h + 1] for h, x in zip(heads, vn)]
            y = [_gelu_tanh(zu_ref[pl.ds(r0, c), sl]) * x for sl, x in zip(sls, sv)]
            ms = [lane_mean(x * x) for x in y]
            for x, s, sl in zip(y, ms, sls):
                y_ref[pl.ds(r0, c), sl] = (x * lax.rsqrt(s + EPS) * ow_ref[:, sl]).astype(BF16)
        return carry

    lax.fori_loop(0, n_chunks, chunk, 0)


def gmlp_mixer(z, w_s, b_s, ln_w, ln_b, out_w, rows=512):
    t = z.shape[0]
    n_heads = w_s.shape[0]
    dg = n_heads * GMLP_HEAD
    rows = min(rows, t)
    kern = functools.partial(_gmlp_kernel, n_heads=n_heads, n_chunks=rows // GMLP_CHUNK)
    vec = lambda a: a.reshape(1, dg)
    return pl.pallas_call(
        kern,
        grid=(t // rows,),
        in_specs=[
            pl.BlockSpec((rows, dg), lambda i: (i, 0)),
            pl.BlockSpec((rows, dg), lambda i: (i, 1)),
            pl.BlockSpec((n_heads, GMLP_CHUNK, GMLP_CHUNK), lambda i: (0, 0, 0)),
            pl.BlockSpec((GMLP_CHUNK, n_heads), lambda i: (0, 0)),
            pl.BlockSpec((1, dg), lambda i: (0, 0)),
            pl.BlockSpec((1, dg), lambda i: (0, 0)),
            pl.BlockSpec((1, dg), lambda i: (0, 0)),
        ],
        out_specs=pl.BlockSpec((rows, dg), lambda i: (i, 0)),
        out_shape=jax.ShapeDtypeStruct((t, dg), BF16),
        scratch_shapes=[pltpu.VMEM((n_heads, GMLP_CHUNK, GMLP_CHUNK), BF16)],
        compiler_params=_cparams(("parallel",)),
        name="gmlp_mixer",
    )(z, z, w_s, b_s.T, vec(ln_w), vec(ln_b), vec(out_w))


def _hgrn_level_terms(a, q, k, blk):
    c, d = a.shape
    half = blk // 2
    nb = c // blk
    lower = lambda v: v.reshape(nb, blk, d)[:, :half, :]
    upper = lambda v: v.reshape(nb, blk, d)[:, half:, :]
    ref = a.reshape(nb, blk, d)[:, half - 1:half, :]
    qh = upper(q) * jnp.exp2(upper(a) - ref)
    kh = lower(k) * jnp.exp2(ref - lower(a))
    zeros = jnp.zeros_like(qh)
    qh = jnp.concatenate([zeros, qh], axis=1).reshape(c, d)
    kh = jnp.concatenate([kh, zeros], axis=1).reshape(c, d)
    return qh.astype(BF16), kh.astype(BF16)


def _hgrn_kernel(zq_ref, zf_ref, zi_ref, zg_ref, lb_ref, ow_ref, y_ref, st_ref, *, n_heads, n_chunks):
    c, d = HGRN_C, HGRN_HEAD
    nt = (((1,), (1,)), ((), ()))
    tn = (((0,), (0,)), ((), ()))

    @pl.when(pl.program_id(1) == 0)
    def _():
        st_ref[...] = jnp.zeros_like(st_ref)

    row = lax.broadcasted_iota(jnp.int32, (c, c), 0)
    col = lax.broadcasted_iota(jnp.int32, (c, c), 1)
    tril = (row >= col).astype(BF16)
    diag_mask = (row >= col) & ((row // HGRN_DIAG) == (col // HGRN_DIAG))
    level_masks = []
    blk = 2 * HGRN_DIAG
    while blk <= c:
        level_masks.append((blk, None if blk == c else (row // blk) == (col // blk)))
        blk *= 2

    def nt_dot(x, y):
        return lax.dot_general(x, y, nt, preferred_element_type=F32)

    def chunk(ci, carry):
        r0 = pl.multiple_of(ci * c, c)
        for g0 in range(0, n_heads, HGRN_GROUP):
            heads = list(range(g0, g0 + HGRN_GROUP))
            sls = [slice(h * d, (h + 1) * d) for h in heads]
            f = [0.5 * (1.0 + lb_ref[:, sl]) + (0.5 * (1.0 - lb_ref[:, sl])) * jnp.tanh(0.5 * zf_ref[pl.ds(r0, c), sl])
                 for sl in sls]
            logf = [jnp.log2(jnp.clip(x, F_MIN, 1.0)) for x in f]
            pieces = [jnp.concatenate(_split_bf16(x), axis=1) for x in logf]
            cs = [jnp.dot(tril, x, preferred_element_type=F32) for x in pieces]
            a = [x[:, :d] + x[:, d:] for x in cs]
            k = [1.0 - x for x in f]
            q = [_silu(zq_ref[pl.ds(r0, c), sl]) for sl in sls]
            iv = [zi_ref[pl.ds(r0, c), sl].astype(BF16) for sl in sls]

            dd = []
            for x in a:
                a3 = x.reshape(c // HGRN_DIAG, HGRN_DIAG, d)
                dd.append((a3 - a3[:, HGRN_DIAG_REF:HGRN_DIAG_REF + 1, :]).reshape(c, d))
            sd = [nt_dot((qx * jnp.exp2(x)).astype(BF16), (kx * jnp.exp2(-x)).astype(BF16))
                  for qx, kx, x in zip(q, k, dd)]

            scores = None
            for blk, same_block in reversed(level_masks):
                terms = [_hgrn_level_terms(ax, qx, kx, blk) for ax, qx, kx in zip(a, q, k)]
                prods = [nt_dot(qh, kh) for qh, kh in terms]
                scores = prods if scores is None else [jnp.where(same_block, p, s) for s, p in zip(scores, prods)]
            scores = [jnp.where(diag_mask, x, s) for x, s in zip(sd, scores)]

            st = [st_ref[h] for h in heads]
            qe = [(qx * jnp.exp2(ax)).astype(BF16) for qx, ax in zip(q, a)]
            o = [jnp.dot(s.astype(BF16), ivx, preferred_element_type=F32) + nt_dot(qx, sx.astype(BF16))
                 for s, ivx, qx, sx in zip(scores, iv, qe, st)]
            kd = [(kx * jnp.exp2(ax[c - 1:c, :] - ax)).astype(BF16) for kx, ax in zip(k, a)]
            for h, ax, sx, ivx, kx in zip(heads, a, st, iv, kd):
                st_ref[h] = jnp.exp2(ax[c - 1:c, :]) * sx + lax.dot_general(ivx, kx, tn, preferred_element_type=F32)
            for x, sl in zip(o, sls):
                y_ref[pl.ds(r0, c), sl] = (_rms(x, ow_ref[:, sl]) * _silu(zg_ref[pl.ds(r0, c), sl])).astype(BF16)
        return carry

    lax.fori_loop(0, n_chunks, chunk, 0)


def hgrn2_mixer(z, lb, out_w, batch, col0, rows=512):
    t = z.shape[0]
    s = t // batch
    dh = lb.shape[0]
    n_heads = dh // HGRN_HEAD
    rows = min(rows, s)
    spb = s // rows
    kern = functools.partial(_hgrn_kernel, n_heads=n_heads, n_chunks=rows // HGRN_C)
    zspec = lambda j: pl.BlockSpec((rows, dh), lambda b, i, j=j: (b * spb + i, col0 + j))
    return pl.pallas_call(
        kern,
        grid=(batch, spb),
        in_specs=[zspec(0), zspec(1), zspec(2), zspec(3),
                  pl.BlockSpec((1, dh), lambda b, i: (0, 0)),
                  pl.BlockSpec((1, dh), lambda b, i: (0, 0))],
        out_specs=pl.BlockSpec((rows, dh), lambda b, i: (b * spb + i, 0)),
        out_shape=jax.ShapeDtypeStruct((t, dh), BF16),
        scratch_shapes=[pltpu.VMEM((n_heads, HGRN_HEAD, HGRN_HEAD), F32)],
        compiler_params=_cparams(("parallel", "arbitrary")),
        name="hgrn2_mixer",
    )(z, z, z, z, lb.reshape(1, dh), out_w.reshape(1, dh))


ROUTE_E0, ROUTE_E1, ROUTE_W0, ROUTE_W1, ROUTE_R0, ROUTE_R1 = range(6)


def _split_bf16(v):
    hi = v.astype(BF16)
    return hi, (v - hi.astype(F32)).astype(BF16)


def _route_rows(h, rw_pieces, carry_ref):
    bm = h.shape[0]
    h_hi, h_lo = _split_bf16(h)
    hh_hl = jnp.dot(h_hi, rw_pieces, preferred_element_type=F32)
    logits = hh_hl[:, :LANES] + hh_hl[:, LANES:] + jnp.dot(h_lo, rw_pieces[:, :LANES], preferred_element_type=F32)
    lane = lax.broadcasted_iota(jnp.int32, (bm, LANES), 1)
    neg = jnp.float32(-jnp.inf)
    logits = jnp.where(lane < N_EXPERTS, logits, neg)
    v0 = jnp.max(logits, axis=-1, keepdims=True)
    e0 = jnp.min(jnp.where(logits == v0, lane, LANES), axis=-1, keepdims=True)
    rest = jnp.where(lane == e0, neg, logits)
    v1 = jnp.max(rest, axis=-1, keepdims=True)
    e1 = jnp.min(jnp.where(rest == v1, lane, LANES), axis=-1, keepdims=True)
    ex = jnp.exp(v1 - v0)
    w0 = 1.0 / (1.0 + ex)
    w1 = ex / (1.0 + ex)

    onehot = ((lane == e0) | (lane == e1)).astype(F32)
    row = lax.broadcasted_iota(jnp.int32, (bm, bm), 0)
    col = lax.broadcasted_iota(jnp.int32, (bm, bm), 1)
    before = (row > col).astype(BF16)
    excl = jnp.dot(before, onehot.astype(BF16), preferred_element_type=F32) + carry_ref[...]
    r0 = jnp.sum(jnp.where(lane == e0, excl, 0.0), axis=-1, keepdims=True)
    r1 = jnp.sum(jnp.where(lane == e1, excl, 0.0), axis=-1, keepdims=True)
    carry_ref[...] += jnp.sum(onehot, axis=0, keepdims=True)

    out = jnp.zeros((bm, LANES), F32)
    for idx, val in ((ROUTE_E0, e0.astype(F32)), (ROUTE_E1, e1.astype(F32)), (ROUTE_W0, w0), (ROUTE_W1, w1),
                     (ROUTE_R0, r0), (ROUTE_R1, r1)):
        out = jnp.where(lane == idx, val, out)
    return out


def _outproj_kernel(*refs, da, moe):
    if moe:
        (x_ref, ya_ref, yb_ref, wo_ref, nw_ref, rw_ref, xo_ref, h_ref, route_ref, cnt_ref,
         wo_scr, rw_scr, carry_scr) = refs
    else:
        x_ref, ya_ref, yb_ref, wo_ref, nw_ref, xo_ref, h_ref, wo_scr = refs

    @pl.when(pl.program_id(0) == 0)
    def _():
        wo_scr[...] = wo_ref[...].astype(BF16)
        if moe:
            rw_hi, rw_lo = _split_bf16(rw_ref[...])
            rw_scr[:, :LANES] = rw_hi
            rw_scr[:, LANES:] = rw_lo
            carry_scr[...] = jnp.zeros_like(carry_scr)

    acc = x_ref[...] + jnp.dot(ya_ref[...], wo_scr[:da, :], preferred_element_type=F32) \
        + jnp.dot(yb_ref[...], wo_scr[da:, :], preferred_element_type=F32)
    xo_ref[...] = acc
    h = _rms(acc, nw_ref[...])
    h_ref[...] = h.astype(h_ref.dtype)
    if moe:
        route_ref[...] = _route_rows(h, rw_scr[...], carry_scr)
        cnt_ref[...] = carry_scr[...]


def out_proj(x, y_a, y_b, w_o, layer, norm_w, router_w=None, bm=None):
    t, d = x.shape
    da, db = y_a.shape[1], y_b.shape[1]
    moe = router_w is not None
    bm = min(bm or (256 if moe else 512), t)
    row_spec = lambda w: pl.BlockSpec((bm, w), lambda i: (i, 0))
    const = lambda shape: pl.BlockSpec(shape, lambda i: (0,) * len(shape), pipeline_mode=pl.Buffered(1))
    w_spec = pl.BlockSpec((None, da + db, d), lambda i: (layer, 0, 0), pipeline_mode=pl.Buffered(1))
    in_specs = [row_spec(d), row_spec(da), row_spec(db), w_spec, const((1, d))]
    out_specs = [row_spec(d), row_spec(d)]
    out_shape = [jax.ShapeDtypeStruct((t, d), F32), jax.ShapeDtypeStruct((t, d), F32 if moe else BF16)]
    scratch = [pltpu.VMEM((da + db, d), BF16)]
    args = [x, y_a, y_b, w_o, norm_w.reshape(1, d)]
    if moe:
        in_specs.append(const((d, LANES)))
        args.append(jnp.zeros((d, LANES), F32).at[:, :N_EXPERTS].set(router_w))
        out_specs += [row_spec(LANES), pl.BlockSpec((1, LANES), lambda i: (0, 0))]
        out_shape += [jax.ShapeDtypeStruct((t, LANES), F32), jax.ShapeDtypeStruct((1, LANES), F32)]
        scratch += [pltpu.VMEM((d, 2 * LANES), BF16), pltpu.VMEM((1, LANES), F32)]
    return pl.pallas_call(
        functools.partial(_outproj_kernel, da=da, moe=moe),
        grid=(t // bm,),
        in_specs=in_specs,
        out_specs=out_specs,
        out_shape=out_shape,
        scratch_shapes=scratch,
        compiler_params=_cparams(("arbitrary",)),
        name="out_proj_route" if moe else "out_proj",
    )(*args)


def _ffn_kernel(x_hbm, h_ref, wg_ref, wu_ref, wd_ref, *rest, n_pass):
    n_cast = (len(rest) - 2) // 2
    o_ref, sem = rest[n_cast], rest[-1]
    i, j, k = pl.program_id(0), pl.program_id(1), pl.program_id(2)
    bm = o_ref.shape[0]
    sub = bm // n_pass

    def x_copy(p):
        src = x_hbm.at[pl.ds(pl.multiple_of(i * bm + p * sub, sub), sub)]
        return pltpu.make_async_copy(src, o_ref.at[pl.ds(p * sub, sub)], sem.at[p])

    @pl.when((j == 0) & (k == 0))
    def _():
        for p in range(n_pass):
            x_copy(p).start()

    for p in range(n_pass):
        @pl.when((j == 0) & (k == p))
        def _():
            x_copy(p).wait()

    for src_ref, dst_ref in zip(rest[:n_cast], rest[n_cast + 1:-1]):
        dst_ref[...] = src_ref[...].astype(BF16)

    rows = pl.ds(pl.multiple_of(k * sub, sub), sub)
    hb = h_ref[rows, :]
    g = jnp.dot(hb, wg_ref[...], preferred_element_type=F32)
    u = jnp.dot(hb, wu_ref[...], preferred_element_type=F32)
    act = (_silu(g) * u).astype(BF16)
    o_ref[rows, :] += jnp.dot(act, wd_ref[...], preferred_element_type=F32)


def dense_ffn(x, h, wg, wu, wd, cast=(), bm=1024, bf=512, n_pass=2):
    t, d = x.shape
    f = wg.shape[1]
    bm = min(bm, t)
    ni, nj = t // bm, f // bf
    cast_specs = []
    for w in cast:
        r, c = w.shape
        if c % nj == 0 and (c // nj) % LANES == 0 and r % (ni * n_pass) == 0:
            cast_specs.append(pl.BlockSpec((r // (ni * n_pass), c // nj), lambda i, j, k: (i * n_pass + k, j)))
        else:
            assert r % (ni * nj * n_pass) == 0, (w.shape, ni, nj, n_pass)
            cast_specs.append(pl.BlockSpec((r // (ni * nj * n_pass), c),
                                           lambda i, j, k: ((i * nj + j) * n_pass + k, 0)))
    outs = pl.pallas_call(
        functools.partial(_ffn_kernel, n_pass=n_pass),
        grid=(ni, nj, n_pass),
        in_specs=[
            pl.BlockSpec(memory_space=pl.ANY),
            pl.BlockSpec((bm, d), lambda i, j, k: (i, 0)),
            pl.BlockSpec((d, bf), lambda i, j, k: (0, j)),
            pl.BlockSpec((d, bf), lambda i, j, k: (0, j)),
            pl.BlockSpec((bf, d), lambda i, j, k: (j, 0)),
        ] + cast_specs,
        out_specs=[pl.BlockSpec((bm, d), lambda i, j, k: (i, 0))] + cast_specs,
        out_shape=[jax.ShapeDtypeStruct((t, d), F32)] + [jax.ShapeDtypeStruct(w.shape, BF16) for w in cast],
        scratch_shapes=[pltpu.SemaphoreType.DMA((n_pass,))],
        compiler_params=_cparams(("parallel", "arbitrary", "arbitrary")),
        name="dense_ffn",
    )(x, h, wg, wu, wd, *cast)
    return outs[0], tuple(outs[1:])


def _row_copy(src_ref, dst_ref, sem, src_row, dst_row):
    return pltpu.make_async_copy(src_ref.at[pl.ds(src_row, 1)], dst_ref.at[pl.ds(dst_row, 1)], sem)


ROW_DMA_UNROLL = 8
GMM_TILE = 1024
GMM_SUBTILES = 2


def _dispatch_kernel(zlo_ref, zhi_ref, pos_ref, h_ref, xs_ref, zero_scr, sem, zsem):
    bt = h_ref.shape[0]

    @pl.when(pl.program_id(0) == 0)
    def _():
        zero_scr[...] = jnp.zeros_like(zero_scr)
        zr = zero_scr.shape[0]

        def zero_rows(e, begin):
            lo, hi = zlo_ref[e], zhi_ref[e]
            mid = jnp.minimum(hi, (lo + zr - 1) // zr * zr)

            def one(r, carry):
                cp = _row_copy(zero_scr, xs_ref, zsem, 0, r)
                cp.start() if begin else cp.wait()
                return carry

            def slab(b, carry):
                cp = pltpu.make_async_copy(zero_scr, xs_ref.at[pl.ds(pl.multiple_of(b * zr, zr), zr)], zsem)
                cp.start() if begin else cp.wait()
                return carry

            lax.fori_loop(lo, mid, one, 0)
            lax.fori_loop(mid // zr, hi // zr, slab, 0)

        for e in range(N_EXPERTS):
            zero_rows(e, True)
        for e in range(N_EXPERTS):
            zero_rows(e, False)

    def start(r, carry):
        for kk in range(TOP_K):
            _row_copy(h_ref, xs_ref, sem, r, pos_ref[0, TOP_K * r + kk]).start(priority=kk % 2)
        return carry

    lax.fori_loop(0, bt, start, 0, unroll=ROW_DMA_UNROLL)
    for kk in range(TOP_K):
        pltpu.make_async_copy(h_ref, xs_ref.at[pl.ds(0, bt)], sem).wait()


def moe_dispatch(h, pos, zero_lo, zero_hi, n_rows, bt=256):
    t, d = h.shape
    bt = min(bt, t)
    pos2 = pos.reshape(t // bt, 1, TOP_K * bt)
    return pl.pallas_call(
        _dispatch_kernel,
        grid_spec=pltpu.PrefetchScalarGridSpec(
            num_scalar_prefetch=2,
            grid=(t // bt,),
            in_specs=[
                pl.BlockSpec((None, 1, TOP_K * bt), lambda i, zl, zh: (i, 0, 0), memory_space=pltpu.SMEM),
                pl.BlockSpec((bt, d), lambda i, zl, zh: (i, 0)),
            ],
            out_specs=pl.BlockSpec(memory_space=pl.ANY),
            scratch_shapes=[pltpu.VMEM((8, d), F32), pltpu.SemaphoreType.DMA(()), pltpu.SemaphoreType.DMA(())],
        ),
        out_shape=jax.ShapeDtypeStruct((n_rows, d), F32),
        compiler_params=_cparams(("arbitrary",)),
        name="moe_dispatch",
    )(zero_lo, zero_hi, pos2, h)


def _gmm_kernel(te_ref, tr_ref, nv_ref, xs_ref, wg_ref, wu_ref, wd_ref, ys_ref, xb_scr):
    i, j = pl.program_id(0), pl.program_id(1)
    bm = xs_ref.shape[0]
    rows = tr_ref[i]
    sub = bm // GMM_SUBTILES

    @pl.when(j == 0)
    def _():
        xb_scr[...] = xs_ref[...].astype(BF16)
        ys_ref[...] = jnp.zeros_like(ys_ref)

    def ffn(r0, n):
        xb = xb_scr[r0:r0 + n, :]
        g = jnp.dot(xb, wg_ref[...], preferred_element_type=F32)
        u = jnp.dot(xb, wu_ref[...], preferred_element_type=F32)
        act = (_silu(g) * u).astype(BF16)
        ys_ref[r0:r0 + n, :] += jnp.dot(act, wd_ref[...], preferred_element_type=F32)

    full = rows == bm

    @pl.when(full)
    def _():
        for s in range(GMM_SUBTILES):
            ffn(s * sub, sub)

    for s in range(GMM_SUBTILES):
        left = rows - s * sub

        @pl.when(jnp.logical_not(full) & (left > sub // 2))
        def _():
            ffn(s * sub, sub)

        @pl.when((left > 0) & (left <= sub // 2))
        def _():
            ffn(s * sub, sub // 2)


def moe_grouped_ffn(xs, tile_expert, tile_rows, n_valid, wg, wu, wd, bm, bf=512):
    p, d = xs.shape
    f = wg.shape[2]
    nf = f // bf

    def row_map(i, j, te, tr, nv):
        return (jnp.minimum(i, nv[0] - 1), 0)

    def fcol(i, j, nv):
        return jnp.where(i < nv[0], j, nf - 1)

    return pl.pallas_call(
        _gmm_kernel,
        grid_spec=pltpu.PrefetchScalarGridSpec(
            num_scalar_prefetch=3,
            grid=(p // bm, nf),
            in_specs=[
                pl.BlockSpec((bm, d), row_map),
                pl.BlockSpec((None, d, bf), lambda i, j, te, tr, nv: (te[i], 0, fcol(i, j, nv))),
                pl.BlockSpec((None, d, bf), lambda i, j, te, tr, nv: (te[i], 0, fcol(i, j, nv))),
                pl.BlockSpec((None, bf, d), lambda i, j, te, tr, nv: (te[i], fcol(i, j, nv), 0)),
            ],
            out_specs=pl.BlockSpec((bm, d), lambda i, j, te, tr, nv: (i, 0)),
            scratch_shapes=[pltpu.VMEM((bm, d), BF16)],
        ),
        out_shape=jax.ShapeDtypeStruct((p, d), F32),
        compiler_params=_cparams(("arbitrary", "arbitrary")),
        name="moe_grouped_ffn",
    )(tile_expert, tile_rows, n_valid, xs, wg, wu, wd)


def _combine_kernel(pos_ref, pos_next_ref, x_ref, route_ref, nw_ref, ys_ref, o_ref, y_scr, sem, *, final_norm):
    bt = x_ref.shape[0]
    i = pl.program_id(0)
    slot = i % 2

    def gather(p_ref, s, unroll):
        def start(r, carry):
            for kk in range(TOP_K):
                _row_copy(ys_ref, y_scr.at[s, kk], sem.at[s], p_ref[0, TOP_K * r + kk], r).start(priority=kk % 2)
            return carry
        lax.fori_loop(0, bt, start, 0, unroll=unroll)

    def wait(s):
        for kk in range(TOP_K):
            pltpu.make_async_copy(ys_ref.at[pl.ds(0, bt)], y_scr.at[s, kk], sem.at[s]).wait()

    @pl.when(i == 0)
    def _():
        gather(pos_ref, 0, ROW_DMA_UNROLL)

    wait(slot)
    gather(pos_next_ref, 1 - slot, True)
    w0 = route_ref[:, ROUTE_W0:ROUTE_W0 + 1]
    w1 = route_ref[:, ROUTE_W1:ROUTE_W1 + 1]
    out = x_ref[...] + (w0 * y_scr[slot, 0] + w1 * y_scr[slot, 1])
    if final_norm:
        out = _rms(out, nw_ref[...])
    o_ref[...] = out

    @pl.when(i + 1 == pl.num_programs(0))
    def _():
        wait(1 - slot)


def moe_combine(x, route, pos, ys, norm_w, final_norm, bt=256):
    t, d = x.shape
    bt = min(bt, t)
    n = t // bt
    pos2 = pos.reshape(n, 1, TOP_K * bt)
    pos_spec = lambda nxt: pl.BlockSpec((None, 1, TOP_K * bt), lambda i: (jnp.minimum(i + nxt, n - 1), 0, 0),
                                        memory_space=pltpu.SMEM)
    return pl.pallas_call(
        functools.partial(_combine_kernel, final_norm=final_norm),
        grid=(n,),
        in_specs=[
            pos_spec(0),
            pos_spec(1),
            pl.BlockSpec((bt, d), lambda i: (i, 0)),
            pl.BlockSpec((bt, LANES), lambda i: (i, 0)),
            pl.BlockSpec((1, d), lambda i: (0, 0)),
            pl.BlockSpec(memory_space=pl.ANY),
        ],
        out_specs=pl.BlockSpec((bt, d), lambda i: (i, 0)),
        out_shape=jax.ShapeDtypeStruct((t, d), F32),
        scratch_shapes=[pltpu.VMEM((2, TOP_K, bt, d), F32), pltpu.SemaphoreType.DMA((2,))],
        compiler_params=_cparams(("arbitrary",)),
        name="moe_combine",
    )(pos2, pos2, x, route, norm_w.reshape(1, d), ys)


def moe_ffn(x, h, route, counts, wg, wu, wd, norm_w, final_norm, bm=GMM_TILE):
    t, d = x.shape

    cnt = counts[0, :N_EXPERTS].astype(jnp.int32)
    tiles = (cnt + bm - 1) // bm
    tile_end = jnp.cumsum(tiles)
    offset = (tile_end - tiles) * bm
    n_tiles = (TOP_K * t) // bm + N_EXPERTS
    n_valid = tile_end[-1:].astype(jnp.int32)
    tile_id = jnp.minimum(jnp.arange(n_tiles, dtype=jnp.int32), n_valid[0] - 1)
    tile_expert = jnp.sum((tile_end[None, :] <= tile_id[:, None]).astype(jnp.int32), axis=1)
    tile_in_group = tile_id - (tile_end - tiles)[tile_expert]
    tile_rows = jnp.clip(cnt[tile_expert] - tile_in_group * bm, 0, bm)
    tile_rows = jnp.where(jnp.arange(n_tiles) < n_valid[0], tile_rows, 0).astype(jnp.int32)
    experts = route[:, ROUTE_E0:ROUTE_E1 + 1].astype(jnp.int32)
    ranks = route[:, ROUTE_R0:ROUTE_R1 + 1].astype(jnp.int32)
    pos = offset[experts] + ranks
    zero_lo = offset + cnt
    zero_hi = jnp.concatenate([offset[1:], jnp.full((1,), n_tiles * bm, jnp.int32)])

    xs = moe_dispatch(h, pos, zero_lo, zero_hi, n_tiles * bm)
    ys = moe_grouped_ffn(xs, tile_expert, tile_rows, n_valid, wg, wu, wd, bm)
    return moe_combine(x, route, pos, ys, norm_w, final_norm)


def _norm_kernel(x_ref, nw_ref, o_ref):
    o_ref[...] = _rms(x_ref[...], nw_ref[...])


def final_norm(x, norm_w, bm=512):
    t, d = x.shape
    bm = min(bm, t)
    return pl.pallas_call(
        _norm_kernel,
        grid=(t // bm,),
        in_specs=[pl.BlockSpec((bm, d), lambda i: (i, 0)), pl.BlockSpec((1, d), lambda i: (0, 0))],
        out_specs=pl.BlockSpec((bm, d), lambda i: (i, 0)),
        out_shape=jax.ShapeDtypeStruct((t, d), F32),
        compiler_params=_cparams(("parallel",)),
        name="final_norm",
    )(x, norm_w.reshape(1, d))


def kernel(x, mix_norm_w, w_in, gmlp_w_s, gmlp_b_s, gmlp_ln_w, gmlp_ln_b, gmlp_out_w, hgrn_lb_logits, hgrn_out_w, w_o, ffn_norm_w, dense_w_gate, dense_w_up, dense_w_down, router_w, expert_w_gate, expert_w_up, expert_w_down, final_norm_w):
    batch, seq, d = x.shape
    depth = w_in.shape[0]
    d_gmlp = gmlp_ln_w.shape[1]
    d_hgrn = hgrn_out_w.shape[1]
    assert d_gmlp == d_hgrn and seq % HGRN_C == 0 and seq % GMLP_CHUNK == 0

    lbs = jax.nn.softmax(hgrn_lb_logits.astype(F32), axis=0)
    lbs = jnp.cumsum(lbs, axis=0) - lbs[0:1]

    xt = x.reshape(batch * seq, d)
    for l in range(depth):
        z = in_proj(xt, mix_norm_w[l], w_in[l].astype(BF16))
        y_a = gmlp_mixer(z, gmlp_w_s[l], gmlp_b_s[l], gmlp_ln_w[l], gmlp_ln_b[l], gmlp_out_w[l])
        y_b = hgrn2_mixer(z, lbs[l], hgrn_out_w[l], batch, col0=2 * d_gmlp // d_hgrn)
        last = l == depth - 1
        j = l // 2
        if l % 2 == 0:
            xt, h = out_proj(xt, y_a, y_b, w_o, l, ffn_norm_w[l])
            experts = () if last else tuple(w[j].reshape(-1, w.shape[-1])
                                            for w in (expert_w_gate, expert_w_up, expert_w_down))
            xt, experts_bf16 = dense_ffn(xt, h, dense_w_gate[j].astype(BF16), dense_w_up[j].astype(BF16),
                                         dense_w_down[j].astype(BF16), cast=experts)
            if last:
                xt = final_norm(xt, final_norm_w)
        else:
            xt, h, route, counts = out_proj(xt, y_a, y_b, w_o, l, ffn_norm_w[l], router_w=router_w[j])
            wg, wu, wd = (w.reshape(s.shape[1:]) for w, s in
                          zip(experts_bf16, (expert_w_gate, expert_w_up, expert_w_down)))
            xt = moe_ffn(xt, h, route, counts, wg, wu, wd, final_norm_w, final_norm=last)
    return xt.reshape(batch, seq, d)
```

```python
import functools

import jax
import jax.numpy as jnp
from jax import lax
from jax.experimental import pallas as pl
from jax.experimental.pallas import tpu as pltpu

F32 = jnp.float32
BF16 = jnp.bfloat16

GMLP_HEAD = 128
GMLP_CHUNK = 128
HGRN_HEAD = 128
N_EXPERTS = 8
TOP_K = 2
EPS = 1e-6
F_MIN = 1e-6

LANES = 128
GMLP_GROUP = 8
HGRN_C = 128
HGRN_GROUP = 8
HGRN_DIAG = 8
HGRN_DIAG_REF = 3
VMEM_LIMIT = 56 * 1024 * 1024
VMEM_LIMIT_IN_PROJ = 60 * 1024 * 1024


def _cparams(sem, vmem=VMEM_LIMIT):
    return pltpu.CompilerParams(dimension_semantics=sem, vmem_limit_bytes=vmem)


def _silu(x):
    hx = 0.5 * x
    return hx + hx * jnp.tanh(hx)


GELU_C1 = 0.7978845608028654
GELU_C2 = GELU_C1 * 0.044715


def _gelu_tanh(x):
    hx = 0.5 * x
    return hx + hx * jnp.tanh(x * (GELU_C1 + GELU_C2 * (x * x)))


def _rms(x, w):
    return x * lax.rsqrt(jnp.mean(x * x, axis=-1, keepdims=True) + EPS) * w


BF16_SUBLANES = 16


def _cast_block_specs(arrays, ni, nj, nk=1):
    specs = []
    for w in arrays:
        r, c = w.shape
        if c % nj == 0 and (c // nj) % LANES == 0 and r % (ni * nk * BF16_SUBLANES) == 0:
            specs.append(pl.BlockSpec((r // (ni * nk), c // nj), lambda i, j, k=0: (i * nk + k, j)))
        else:
            assert r % (ni * nj * nk * BF16_SUBLANES) == 0, (w.shape, ni, nj, nk)
            specs.append(pl.BlockSpec((r // (ni * nj * nk), c), lambda i, j, k=0: ((i * nj + j) * nk + k, 0)))
    return specs


def _narrow_slabs(src_refs, dst_refs):
    for src_ref, dst_ref in zip(src_refs, dst_refs):
        dst_ref[...] = src_ref[...].astype(BF16)


def _inproj_kernel(x_ref, nw_ref, w_ref, *rest):
    n_cast = (len(rest) - 2) // 2
    z_ref, h_scr = rest[n_cast], rest[-1]

    @pl.when(pl.program_id(1) == 0)
    def _():
        h_scr[...] = _rms(x_ref[...], nw_ref[...]).astype(BF16)

    _narrow_slabs(rest[:n_cast], rest[n_cast + 1:-1])
    z_ref[...] = jnp.dot(h_scr[...], w_ref[...], preferred_element_type=F32)


def in_proj(x, norm_w, w, cast=(), bm=1024, bn=1536):
    t, d = x.shape
    n = w.shape[1]
    bm = min(bm, t)
    ni, nj = t // bm, n // bn
    cast_specs = _cast_block_specs(cast, ni, nj)
    outs = pl.pallas_call(
        _inproj_kernel,
        grid=(ni, nj),
        in_specs=[
            pl.BlockSpec((bm, d), lambda i, j: (i, 0)),
            pl.BlockSpec((1, d), lambda i, j: (0, 0)),
            pl.BlockSpec((d, bn), lambda i, j: (0, j)),
        ] + cast_specs,
        out_specs=[pl.BlockSpec((bm, bn), lambda i, j: (i, j))] + cast_specs,
        out_shape=[jax.ShapeDtypeStruct((t, n), F32)] + [jax.ShapeDtypeStruct(a.shape, BF16) for a in cast],
        scratch_shapes=[pltpu.VMEM((bm, d), BF16)],
        compiler_params=_cparams(("parallel", "arbitrary"), VMEM_LIMIT_IN_PROJ),
        name="in_proj",
    )(x, norm_w.reshape(1, d), w, *cast)
    return outs[0], tuple(outs[1:])


def _gmlp_kernel(zu_ref, zv_ref, ws_ref, bs_ref, lnw_ref, lnb_ref, ow_ref, y_ref, wc_scr, *, n_heads, n_chunks):
    c = GMLP_CHUNK
    row = lax.broadcasted_iota(jnp.int32, (c, c), 0)
    col = lax.broadcasted_iota(jnp.int32, (c, c), 1)
    causal = row >= col
    ones = jnp.ones((GMLP_HEAD, GMLP_HEAD), BF16)
    lane_mean = lambda x: jnp.dot(x.astype(BF16), ones, preferred_element_type=F32) * (1.0 / GMLP_HEAD)

    for h in range(n_heads):
        wc_scr[h] = jnp.where(causal, ws_ref[h], 0.0).astype(BF16)

    def chunk(ci, carry):
        r0 = pl.multiple_of(ci * c, c)
        for g0 in range(0, n_heads, GMLP_GROUP):
            heads = range(g0, g0 + GMLP_GROUP)
            sls = [slice(h * GMLP_HEAD, (h + 1) * GMLP_HEAD) for h in heads]
            v = [_gelu_tanh(zv_ref[pl.ds(r0, c), sl]) for sl in sls]
            mu = [lane_mean(x) for x in v]
            vc = [x - m for x, m in zip(v, mu)]
            var = [lane_mean(x * x) for x in vc]
            vn = [(x * lax.rsqrt(s + EPS) * lnw_ref[:, sl] + lnb_ref[:, sl]).astype(BF16)
                  for x, s, sl in zip(vc, var, sls)]
            sv = [jnp.dot(wc_scr[h], x, preferred_element_type=F32) + bs_ref[:, h:h + 1] for h, x in zip(heads, vn)]
            y = [_gelu_tanh(zu_ref[pl.ds(r0, c), sl]) * x for sl, x in zip(sls, sv)]
            ms = [lane_mean(x * x) for x in y]
            for x, s, sl in zip(y, ms, sls):
                y_ref[pl.ds(r0, c), sl] = (x * lax.rsqrt(s + EPS) * ow_ref[:, sl]).astype(BF16)
        return carry

    lax.fori_loop(0, n_chunks, chunk, 0)


def gmlp_mixer(z, w_s, b_s, ln_w, ln_b, out_w, rows=512):
    t = z.shape[0]
    n_heads = w_s.shape[0]
    dg = n_heads * GMLP_HEAD
    rows = min(rows, t)
    kern = functools.partial(_gmlp_kernel, n_heads=n_heads, n_chunks=rows // GMLP_CHUNK)
    vec = lambda a: a.reshape(1, dg)
    return pl.pallas_call(
        kern,
        grid=(t // rows,),
        in_specs=[
            pl.BlockSpec((rows, dg), lambda i: (i, 0)),
            pl.BlockSpec((rows, dg), lambda i: (i, 1)),
            pl.BlockSpec((n_heads, GMLP_CHUNK, GMLP_CHUNK), lambda i: (0, 0, 0)),
            pl.BlockSpec((GMLP_CHUNK, n_heads), lambda i: (0, 0)),
            pl.BlockSpec((1, dg), lambda i: (0, 0)),
            pl.BlockSpec((1, dg), lambda i: (0, 0)),
            pl.BlockSpec((1, dg), lambda i: (0, 0)),
        ],
        out_specs=pl.BlockSpec((rows, dg), lambda i: (i, 0)),
        out_shape=jax.ShapeDtypeStruct((t, dg), BF16),
        scratch_shapes=[pltpu.VMEM((n_heads, GMLP_CHUNK, GMLP_CHUNK), BF16)],
        compiler_params=_cparams(("parallel",)),
        name="gmlp_mixer",
    )(z, z, w_s, b_s.T, vec(ln_w), vec(ln_b), vec(out_w))


def _hgrn_level_terms(a, q, k, blk):
    c, d = a.shape
    half = blk // 2
    nb = c // blk
    lower = lambda v: v.reshape(nb, blk, d)[:, :half, :]
    upper = lambda v: v.reshape(nb, blk, d)[:, half:, :]
    ref = a.reshape(nb, blk, d)[:, half - 1:half, :]
    qh = upper(q) * jnp.exp2(upper(a) - ref)
    kh = lower(k) * jnp.exp2(ref - lower(a))
    zeros = jnp.zeros_like(qh)
    qh = jnp.concatenate([zeros, qh], axis=1).reshape(c, d)
    kh = jnp.concatenate([kh, zeros], axis=1).reshape(c, d)
    return qh.astype(BF16), kh.astype(BF16)


def _hgrn_kernel(zq_ref, zf_ref, zi_ref, zg_ref, lb_ref, ow_ref, y_ref, st_ref, *, n_heads, n_chunks):
    c, d = HGRN_C, HGRN_HEAD
    nt = (((1,), (1,)), ((), ()))
    tn = (((0,), (0,)), ((), ()))

    @pl.when(pl.program_id(1) == 0)
    def _():
        st_ref[...] = jnp.zeros_like(st_ref)

    row = lax.broadcasted_iota(jnp.int32, (c, c), 0)
    col = lax.broadcasted_iota(jnp.int32, (c, c), 1)
    tril = (row >= col).astype(BF16)
    diag_mask = (row >= col) & ((row // HGRN_DIAG) == (col // HGRN_DIAG))
    level_masks = []
    blk = 2 * HGRN_DIAG
    while blk <= c:
        level_masks.append((blk, None if blk == c else (row // blk) == (col // blk)))
        blk *= 2

    def nt_dot(x, y):
        return lax.dot_general(x, y, nt, preferred_element_type=F32)

    def chunk(ci, carry):
        r0 = pl.multiple_of(ci * c, c)
        for g0 in range(0, n_heads, HGRN_GROUP):
            heads = list(range(g0, g0 + HGRN_GROUP))
            sls = [slice(h * d, (h + 1) * d) for h in heads]
            f = [0.5 * (1.0 + lb_ref[:, sl]) + (0.5 * (1.0 - lb_ref[:, sl])) * jnp.tanh(0.5 * zf_ref[pl.ds(r0, c), sl])
                 for sl in sls]
            logf = [jnp.log2(jnp.clip(x, F_MIN, 1.0)) for x in f]
            pieces = [jnp.concatenate(_split_bf16(x), axis=1) for x in logf]
            cs = [jnp.dot(tril, x, preferred_element_type=F32) for x in pieces]
            a = [x[:, :d] + x[:, d:] for x in cs]
            k = [1.0 - x for x in f]
            q = [_silu(zq_ref[pl.ds(r0, c), sl]) for sl in sls]
            iv = [zi_ref[pl.ds(r0, c), sl].astype(BF16) for sl in sls]

            dd = []
            for x in a:
                a3 = x.reshape(c // HGRN_DIAG, HGRN_DIAG, d)
                dd.append((a3 - a3[:, HGRN_DIAG_REF:HGRN_DIAG_REF + 1, :]).reshape(c, d))
            sd = [nt_dot((qx * jnp.exp2(x)).astype(BF16), (kx * jnp.exp2(-x)).astype(BF16))
                  for qx, kx, x in zip(q, k, dd)]

            scores = None
            for blk, same_block in reversed(level_masks):
                terms = [_hgrn_level_terms(ax, qx, kx, blk) for ax, qx, kx in zip(a, q, k)]
                prods = [nt_dot(qh, kh) for qh, kh in terms]
                scores = prods if scores is None else [jnp.where(same_block, p, s) for s, p in zip(scores, prods)]
            scores = [jnp.where(diag_mask, x, s) for x, s in zip(sd, scores)]

            st = [st_ref[h] for h in heads]
            qe = [(qx * jnp.exp2(ax)).astype(BF16) for qx, ax in zip(q, a)]
            o = [jnp.dot(s.astype(BF16), ivx, preferred_element_type=F32) + nt_dot(qx, sx.astype(BF16))
                 for s, ivx, qx, sx in zip(scores, iv, qe, st)]
            kd = [(kx * jnp.exp2(ax[c - 1:c, :] - ax)).astype(BF16) for kx, ax in zip(k, a)]
            for h, ax, sx, ivx, kx in zip(heads, a, st, iv, kd):
                st_ref[h] = jnp.exp2(ax[c - 1:c, :]) * sx + lax.dot_general(ivx, kx, tn, preferred_element_type=F32)
            for x, sl in zip(o, sls):
                y_ref[pl.ds(r0, c), sl] = (_rms(x, ow_ref[:, sl]) * _silu(zg_ref[pl.ds(r0, c), sl])).astype(BF16)
        return carry

    lax.fori_loop(0, n_chunks, chunk, 0)


def hgrn2_mixer(z, lb, out_w, batch, col0, rows=512):
    t = z.shape[0]
    s = t // batch
    dh = lb.shape[0]
    n_heads = dh // HGRN_HEAD
    rows = min(rows, s)
    spb = s // rows
    kern = functools.partial(_hgrn_kernel, n_heads=n_heads, n_chunks=rows // HGRN_C)
    zspec = lambda j: pl.BlockSpec((rows, dh), lambda b, i, j=j: (b * spb + i, col0 + j))
    return pl.pallas_call(
        kern,
        grid=(batch, spb),
        in_specs=[zspec(0), zspec(1), zspec(2), zspec(3),
                  pl.BlockSpec((1, dh), lambda b, i: (0, 0)),
                  pl.BlockSpec((1, dh), lambda b, i: (0, 0))],
        out_specs=pl.BlockSpec((rows, dh), lambda b, i: (b * spb + i, 0)),
        out_shape=jax.ShapeDtypeStruct((t, dh), BF16),
        scratch_shapes=[pltpu.VMEM((n_heads, HGRN_HEAD, HGRN_HEAD), F32)],
        compiler_params=_cparams(("parallel", "arbitrary")),
        name="hgrn2_mixer",
    )(z, z, z, z, lb.reshape(1, dh), out_w.reshape(1, dh))


ROUTE_E0, ROUTE_E1, ROUTE_W0, ROUTE_W1, ROUTE_R0, ROUTE_R1 = range(6)


def _split_bf16(v):
    hi = v.astype(BF16)
    return hi, (v - hi.astype(F32)).astype(BF16)


def _route_rows(h, rw_pieces, carry_ref):
    bm = h.shape[0]
    h_hi, h_lo = _split_bf16(h)
    hh_hl = jnp.dot(h_hi, rw_pieces, preferred_element_type=F32)
    logits = hh_hl[:, :LANES] + hh_hl[:, LANES:] + jnp.dot(h_lo, rw_pieces[:, :LANES], preferred_element_type=F32)
    lane = lax.broadcasted_iota(jnp.int32, (bm, LANES), 1)
    neg = jnp.float32(-jnp.inf)
    logits = jnp.where(lane < N_EXPERTS, logits, neg)
    v0 = jnp.max(logits, axis=-1, keepdims=True)
    e0 = jnp.min(jnp.where(logits == v0, lane, LANES), axis=-1, keepdims=True)
    rest = jnp.where(lane == e0, neg, logits)
    v1 = jnp.max(rest, axis=-1, keepdims=True)
    e1 = jnp.min(jnp.where(rest == v1, lane, LANES), axis=-1, keepdims=True)
    ex = jnp.exp(v1 - v0)
    w0 = 1.0 / (1.0 + ex)
    w1 = ex / (1.0 + ex)

    onehot = ((lane == e0) | (lane == e1)).astype(F32)
    row = lax.broadcasted_iota(jnp.int32, (bm, bm), 0)
    col = lax.broadcasted_iota(jnp.int32, (bm, bm), 1)
    before = (row > col).astype(BF16)
    excl = jnp.dot(before, onehot.astype(BF16), preferred_element_type=F32) + carry_ref[...]
    r0 = jnp.sum(jnp.where(lane == e0, excl, 0.0), axis=-1, keepdims=True)
    r1 = jnp.sum(jnp.where(lane == e1, excl, 0.0), axis=-1, keepdims=True)
    carry_ref[...] += jnp.sum(onehot, axis=0, keepdims=True)

    out = jnp.zeros((bm, LANES), F32)
    for idx, val in ((ROUTE_E0, e0.astype(F32)), (ROUTE_E1, e1.astype(F32)), (ROUTE_W0, w0), (ROUTE_W1, w1),
                     (ROUTE_R0, r0), (ROUTE_R1, r1)):
        out = jnp.where(lane == idx, val, out)
    return out


def _outproj_kernel(*refs, da, moe):
    if moe:
        (x_ref, ya_ref, yb_ref, wo_ref, nw_ref, rw_ref, xo_ref, h_ref, route_ref, cnt_ref,
         wo_scr, rw_scr, carry_scr) = refs
    else:
        x_ref, ya_ref, yb_ref, wo_ref, nw_ref, xo_ref, h_ref, wo_scr = refs

    @pl.when(pl.program_id(0) == 0)
    def _():
        wo_scr[...] = wo_ref[...].astype(BF16)
        if moe:
            rw_hi, rw_lo = _split_bf16(rw_ref[...])
            rw_scr[:, :LANES] = rw_hi
            rw_scr[:, LANES:] = rw_lo
            carry_scr[...] = jnp.zeros_like(carry_scr)

    acc = x_ref[...] + jnp.dot(ya_ref[...], wo_scr[:da, :], preferred_element_type=F32) \
        + jnp.dot(yb_ref[...], wo_scr[da:, :], preferred_element_type=F32)
    xo_ref[...] = acc
    h = _rms(acc, nw_ref[...])
    h_ref[...] = h.astype(h_ref.dtype)
    if moe:
        route_ref[...] = _route_rows(h, rw_scr[...], carry_scr)
        cnt_ref[...] = carry_scr[...]


def out_proj(x, y_a, y_b, w_o, layer, norm_w, router_w=None, bm=None):
    t, d = x.shape
    da, db = y_a.shape[1], y_b.shape[1]
    moe = router_w is not None
    bm = min(bm or (256 if moe else 512), t)
    row_spec = lambda w: pl.BlockSpec((bm, w), lambda i: (i, 0))
    const = lambda shape: pl.BlockSpec(shape, lambda i: (0,) * len(shape), pipeline_mode=pl.Buffered(1))
    w_spec = pl.BlockSpec((None, da + db, d), lambda i: (layer, 0, 0), pipeline_mode=pl.Buffered(1))
    in_specs = [row_spec(d), row_spec(da), row_spec(db), w_spec, const((1, d))]
    out_specs = [row_spec(d), row_spec(d)]
    out_shape = [jax.ShapeDtypeStruct((t, d), F32), jax.ShapeDtypeStruct((t, d), F32 if moe else BF16)]
    scratch = [pltpu.VMEM((da + db, d), BF16)]
    args = [x, y_a, y_b, w_o, norm_w.reshape(1, d)]
    if moe:
        in_specs.append(const((d, LANES)))
        args.append(jnp.zeros((d, LANES), F32).at[:, :N_EXPERTS].set(router_w))
        out_specs += [row_spec(LANES), pl.BlockSpec((1, LANES), lambda i: (0, 0))]
        out_shape += [jax.ShapeDtypeStruct((t, LANES), F32), jax.ShapeDtypeStruct((1, LANES), F32)]
        scratch += [pltpu.VMEM((d, 2 * LANES), BF16), pltpu.VMEM((1, LANES), F32)]
    return pl.pallas_call(
        functools.partial(_outproj_kernel, da=da, moe=moe),
        grid=(t // bm,),
        in_specs=in_specs,
        out_specs=out_specs,
        out_shape=out_shape,
        scratch_shapes=scratch,
        compiler_params=_cparams(("arbitrary",)),
        name="out_proj_route" if moe else "out_proj",
    )(*args)


def _ffn_kernel(x_hbm, h_ref, wg_ref, wu_ref, wd_ref, *rest, n_pass):
    n_cast = (len(rest) - 2) // 2
    o_ref, sem = rest[n_cast], rest[-1]
    i, j, k = pl.program_id(0), pl.program_id(1), pl.program_id(2)
    bm = o_ref.shape[0]
    sub = bm // n_pass

    def x_copy(p):
        src = x_hbm.at[pl.ds(pl.multiple_of(i * bm + p * sub, sub), sub)]
        return pltpu.make_async_copy(src, o_ref.at[pl.ds(p * sub, sub)], sem.at[p])

    @pl.when((j == 0) & (k == 0))
    def _():
        for p in range(n_pass):
            x_copy(p).start()

    for p in range(n_pass):
        @pl.when((j == 0) & (k == p))
        def _():
            x_copy(p).wait()

    _narrow_slabs(rest[:n_cast], rest[n_cast + 1:-1])

    rows = pl.ds(pl.multiple_of(k * sub, sub), sub)
    hb = h_ref[rows, :]
    g = jnp.dot(hb, wg_ref[...], preferred_element_type=F32)
    u = jnp.dot(hb, wu_ref[...], preferred_element_type=F32)
    act = (_silu(g) * u).astype(BF16)
    o_ref[rows, :] += jnp.dot(act, wd_ref[...], preferred_element_type=F32)


def dense_ffn(x, h, wg, wu, wd, cast=(), bm=1024, bf=512, n_pass=2):
    t, d = x.shape
    f = wg.shape[1]
    bm = min(bm, t)
    ni, nj = t // bm, f // bf
    cast_specs = _cast_block_specs(cast, ni, nj, n_pass)
    outs = pl.pallas_call(
        functools.partial(_ffn_kernel, n_pass=n_pass),
        grid=(ni, nj, n_pass),
        in_specs=[
            pl.BlockSpec(memory_space=pl.ANY),
            pl.BlockSpec((bm, d), lambda i, j, k: (i, 0)),
            pl.BlockSpec((d, bf), lambda i, j, k: (0, j)),
            pl.BlockSpec((d, bf), lambda i, j, k: (0, j)),
            pl.BlockSpec((bf, d), lambda i, j, k: (j, 0)),
        ] + cast_specs,
        out_specs=[pl.BlockSpec((bm, d), lambda i, j, k: (i, 0))] + cast_specs,
        out_shape=[jax.ShapeDtypeStruct((t, d), F32)] + [jax.ShapeDtypeStruct(w.shape, BF16) for w in cast],
        scratch_shapes=[pltpu.SemaphoreType.DMA((n_pass,))],
        compiler_params=_cparams(("parallel", "arbitrary", "arbitrary")),
        name="dense_ffn",
    )(x, h, wg, wu, wd, *cast)
    return outs[0], tuple(outs[1:])


def _row_copy(src_ref, dst_ref, sem, src_row, dst_row):
    return pltpu.make_async_copy(src_ref.at[pl.ds(src_row, 1)], dst_ref.at[pl.ds(dst_row, 1)], sem)


ROW_DMA_UNROLL = 8
GMM_TILE = 1024
GMM_SUBTILES = 2


def _dispatch_kernel(zlo_ref, zhi_ref, pos_ref, h_ref, xs_ref, zero_scr, sem, zsem):
    bt = h_ref.shape[0]

    @pl.when(pl.program_id(0) == 0)
    def _():
        zero_scr[...] = jnp.zeros_like(zero_scr)
        zr = zero_scr.shape[0]

        def zero_rows(e, begin):
            lo, hi = zlo_ref[e], zhi_ref[e]
            mid = jnp.minimum(hi, (lo + zr - 1) // zr * zr)

            def one(r, carry):
                cp = _row_copy(zero_scr, xs_ref, zsem, 0, r)
                cp.start() if begin else cp.wait()
                return carry

            def slab(b, carry):
                cp = pltpu.make_async_copy(zero_scr, xs_ref.at[pl.ds(pl.multiple_of(b * zr, zr), zr)], zsem)
                cp.start() if begin else cp.wait()
                return carry

            lax.fori_loop(lo, mid, one, 0)
            lax.fori_loop(mid // zr, hi // zr, slab, 0)

        for e in range(N_EXPERTS):
            zero_rows(e, True)
        for e in range(N_EXPERTS):
            zero_rows(e, False)

    def start(r, carry):
        for kk in range(TOP_K):
            _row_copy(h_ref, xs_ref, sem, r, pos_ref[0, TOP_K * r + kk]).start(priority=kk % 2)
        return carry

    lax.fori_loop(0, bt, start, 0, unroll=ROW_DMA_UNROLL)
    for kk in range(TOP_K):
        pltpu.make_async_copy(h_ref, xs_ref.at[pl.ds(0, bt)], sem).wait()


def moe_dispatch(h, pos, zero_lo, zero_hi, n_rows, bt=256):
    t, d = h.shape
    bt = min(bt, t)
    pos2 = pos.reshape(t // bt, 1, TOP_K * bt)
    return pl.pallas_call(
        _dispatch_kernel,
        grid_spec=pltpu.PrefetchScalarGridSpec(
            num_scalar_prefetch=2,
            grid=(t // bt,),
            in_specs=[
                pl.BlockSpec((None, 1, TOP_K * bt), lambda i, zl, zh: (i, 0, 0), memory_space=pltpu.SMEM),
                pl.BlockSpec((bt, d), lambda i, zl, zh: (i, 0)),
            ],
            out_specs=pl.BlockSpec(memory_space=pl.ANY),
            scratch_shapes=[pltpu.VMEM((8, d), F32), pltpu.SemaphoreType.DMA(()), pltpu.SemaphoreType.DMA(())],
        ),
        out_shape=jax.ShapeDtypeStruct((n_rows, d), F32),
        compiler_params=_cparams(("arbitrary",)),
        name="moe_dispatch",
    )(zero_lo, zero_hi, pos2, h)


def _gmm_kernel(te_ref, tr_ref, nv_ref, xs_ref, wg_ref, wu_ref, wd_ref, ys_ref, xb_scr):
    i, j = pl.program_id(0), pl.program_id(1)
    bm = xs_ref.shape[0]
    rows = tr_ref[i]
    sub = bm // GMM_SUBTILES

    @pl.when(j == 0)
    def _():
        xb_scr[...] = xs_ref[...].astype(BF16)
        ys_ref[...] = jnp.zeros_like(ys_ref)

    def ffn(r0, n):
        xb = xb_scr[r0:r0 + n, :]
        g = jnp.dot(xb, wg_ref[...], preferred_element_type=F32)
        u = jnp.dot(xb, wu_ref[...], preferred_element_type=F32)
        act = (_silu(g) * u).astype(BF16)
        ys_ref[r0:r0 + n, :] += jnp.dot(act, wd_ref[...], preferred_element_type=F32)

    full = rows == bm

    @pl.when(full)
    def _():
        for s in range(GMM_SUBTILES):
            ffn(s * sub, sub)

    for s in range(GMM_SUBTILES):
        left = rows - s * sub

        @pl.when(jnp.logical_not(full) & (left > sub // 2))
        def _():
            ffn(s * sub, sub)

        @pl.when((left > 0) & (left <= sub // 2))
        def _():
            ffn(s * sub, sub // 2)


def moe_grouped_ffn(xs, tile_expert, tile_rows, n_valid, wg, wu, wd, bm, bf=512):
    p, d = xs.shape
    f = wg.shape[2]
    nf = f // bf

    def row_map(i, j, te, tr, nv):
        return (jnp.minimum(i, nv[0] - 1), 0)

    def fcol(i, j, nv):
        return jnp.where(i < nv[0], j, nf - 1)

    return pl.pallas_call(
        _gmm_kernel,
        grid_spec=pltpu.PrefetchScalarGridSpec(
            num_scalar_prefetch=3,
            grid=(p // bm, nf),
            in_specs=[
                pl.BlockSpec((bm, d), row_map),
                pl.BlockSpec((None, d, bf), lambda i, j, te, tr, nv: (te[i], 0, fcol(i, j, nv))),
                pl.BlockSpec((None, d, bf), lambda i, j, te, tr, nv: (te[i], 0, fcol(i, j, nv))),
                pl.BlockSpec((None, bf, d), lambda i, j, te, tr, nv: (te[i], fcol(i, j, nv), 0)),
            ],
            out_specs=pl.BlockSpec((bm, d), lambda i, j, te, tr, nv: (i, 0)),
            scratch_shapes=[pltpu.VMEM((bm, d), BF16)],
        ),
        out_shape=jax.ShapeDtypeStruct((p, d), F32),
        compiler_params=_cparams(("arbitrary", "arbitrary")),
        name="moe_grouped_ffn",
    )(tile_expert, tile_rows, n_valid, xs, wg, wu, wd)


def _combine_kernel(pos_ref, pos_next_ref, x_ref, route_ref, nw_ref, ys_ref, o_ref, y_scr, sem, *, final_norm):
    bt = x_ref.shape[0]
    i = pl.program_id(0)
    slot = i % 2

    def gather(p_ref, s, unroll):
        def start(r, carry):
            for kk in range(TOP_K):
                _row_copy(ys_ref, y_scr.at[s, kk], sem.at[s], p_ref[0, TOP_K * r + kk], r).start(priority=kk % 2)
            return carry
        lax.fori_loop(0, bt, start, 0, unroll=unroll)

    def wait(s):
        for kk in range(TOP_K):
            pltpu.make_async_copy(ys_ref.at[pl.ds(0, bt)], y_scr.at[s, kk], sem.at[s]).wait()

    @pl.when(i == 0)
    def _():
        gather(pos_ref, 0, ROW_DMA_UNROLL)

    wait(slot)
    gather(pos_next_ref, 1 - slot, True)
    w0 = route_ref[:, ROUTE_W0:ROUTE_W0 + 1]
    w1 = route_ref[:, ROUTE_W1:ROUTE_W1 + 1]
    out = x_ref[...] + (w0 * y_scr[slot, 0] + w1 * y_scr[slot, 1])
    if final_norm:
        out = _rms(out, nw_ref[...])
    o_ref[...] = out

    @pl.when(i + 1 == pl.num_programs(0))
    def _():
        wait(1 - slot)


def moe_combine(x, route, pos, ys, norm_w, final_norm, bt=256):
    t, d = x.shape
    bt = min(bt, t)
    n = t // bt
    pos2 = pos.reshape(n, 1, TOP_K * bt)
    pos_spec = lambda nxt: pl.BlockSpec((None, 1, TOP_K * bt), lambda i: (jnp.minimum(i + nxt, n - 1), 0, 0),
                                        memory_space=pltpu.SMEM)
    return pl.pallas_call(
        functools.partial(_combine_kernel, final_norm=final_norm),
        grid=(n,),
        in_specs=[
            pos_spec(0),
            pos_spec(1),
            pl.BlockSpec((bt, d), lambda i: (i, 0)),
            pl.BlockSpec((bt, LANES), lambda i: (i, 0)),
            pl.BlockSpec((1, d), lambda i: (0, 0)),
            pl.BlockSpec(memory_space=pl.ANY),
        ],
        out_specs=pl.BlockSpec((bt, d), lambda i: (i, 0)),
        out_shape=jax.ShapeDtypeStruct((t, d), F32),
        scratch_shapes=[pltpu.VMEM((2, TOP_K, bt, d), F32), pltpu.SemaphoreType.DMA((2,))],
        compiler_params=_cparams(("arbitrary",)),
        name="moe_combine",
    )(pos2, pos2, x, route, norm_w.reshape(1, d), ys)


def moe_ffn(x, h, route, counts, wg, wu, wd, norm_w, final_norm, bm=GMM_TILE):
    t, d = x.shape

    cnt = counts[0, :N_EXPERTS].astype(jnp.int32)
    tiles = (cnt + bm - 1) // bm
    tile_end = jnp.cumsum(tiles)
    offset = (tile_end - tiles) * bm
    n_tiles = (TOP_K * t) // bm + N_EXPERTS
    n_valid = tile_end[-1:].astype(jnp.int32)
    tile_id = jnp.minimum(jnp.arange(n_tiles, dtype=jnp.int32), n_valid[0] - 1)
    tile_expert = jnp.sum((tile_end[None, :] <= tile_id[:, None]).astype(jnp.int32), axis=1)
    tile_in_group = tile_id - (tile_end - tiles)[tile_expert]
    tile_rows = jnp.clip(cnt[tile_expert] - tile_in_group * bm, 0, bm)
    tile_rows = jnp.where(jnp.arange(n_tiles) < n_valid[0], tile_rows, 0).astype(jnp.int32)
    experts = route[:, ROUTE_E0:ROUTE_E1 + 1].astype(jnp.int32)
    ranks = route[:, ROUTE_R0:ROUTE_R1 + 1].astype(jnp.int32)
    pos = offset[experts] + ranks
    zero_lo = offset + cnt
    zero_hi = jnp.concatenate([offset[1:], jnp.full((1,), n_tiles * bm, jnp.int32)])

    xs = moe_dispatch(h, pos, zero_lo, zero_hi, n_tiles * bm)
    ys = moe_grouped_ffn(xs, tile_expert, tile_rows, n_valid, wg, wu, wd, bm)
    return moe_combine(x, route, pos, ys, norm_w, final_norm)


def _norm_kernel(x_ref, nw_ref, o_ref):
    o_ref[...] = _rms(x_ref[...], nw_ref[...])


def final_norm(x, norm_w, bm=512):
    t, d = x.shape
    bm = min(bm, t)
    return pl.pallas_call(
        _norm_kernel,
        grid=(t // bm,),
        in_specs=[pl.BlockSpec((bm, d), lambda i: (i, 0)), pl.BlockSpec((1, d), lambda i: (0, 0))],
        out_specs=pl.BlockSpec((bm, d), lambda i: (i, 0)),
        out_shape=jax.ShapeDtypeStruct((t, d), F32),
        compiler_params=_cparams(("parallel",)),
        name="final_norm",
    )(x, norm_w.reshape(1, d))


def kernel(x, mix_norm_w, w_in, gmlp_w_s, gmlp_b_s, gmlp_ln_w, gmlp_ln_b, gmlp_out_w, hgrn_lb_logits, hgrn_out_w, w_o, ffn_norm_w, dense_w_gate, dense_w_up, dense_w_down, router_w, expert_w_gate, expert_w_up, expert_w_down, final_norm_w):
    batch, seq, d = x.shape
    depth = w_in.shape[0]
    d_gmlp = gmlp_ln_w.shape[1]
    d_hgrn = hgrn_out_w.shape[1]
    assert d_gmlp == d_hgrn and seq % HGRN_C == 0 and seq % GMLP_CHUNK == 0

    lbs = jax.nn.softmax(hgrn_lb_logits.astype(F32), axis=0)
    lbs = jnp.cumsum(lbs, axis=0) - lbs[0:1]

    xt = x.reshape(batch * seq, d)
    w_in_bf16 = w_in[0].astype(BF16)
    for l in range(depth):
        last = l == depth - 1
        j = l // 2
        soon = [] if last else [w_in[l + 1]]
        if l % 2 == 0:
            soon += [dense_w_gate[j], dense_w_up[j], dense_w_down[j]]
        z, narrowed = in_proj(xt, mix_norm_w[l], w_in_bf16, cast=tuple(soon))
        narrowed = list(narrowed)
        if not last:
            w_in_bf16 = narrowed.pop(0)
        y_a = gmlp_mixer(z, gmlp_w_s[l], gmlp_b_s[l], gmlp_ln_w[l], gmlp_ln_b[l], gmlp_out_w[l])
        y_b = hgrn2_mixer(z, lbs[l], hgrn_out_w[l], batch, col0=2 * d_gmlp // d_hgrn)
        if l % 2 == 0:
            xt, h = out_proj(xt, y_a, y_b, w_o, l, ffn_norm_w[l])
            experts = () if last else tuple(w[j].reshape(-1, w.shape[-1])
                                            for w in (expert_w_gate, expert_w_up, expert_w_down))
            xt, experts_bf16 = dense_ffn(xt, h, *narrowed, cast=experts)
            if last:
                xt = final_norm(xt, final_norm_w)
        else:
            xt, h, route, counts = out_proj(xt, y_a, y_b, w_o, l, ffn_norm_w[l], router_w=router_w[j])
            wg, wu, wd = (w.reshape(s.shape[1:]) for w, s in
                          zip(experts_bf16, (expert_w_gate, expert_w_up, expert_w_down)))
            xt = moe_ffn(xt, h, route, counts, wg, wu, wd, final_norm_w, final_norm=last)
    return xt.reshape(batch, seq, d)
```

```python
import functools

import jax
import jax.numpy as jnp
from jax import lax
from jax.experimental import pallas as pl
from jax.experimental.pallas import tpu as pltpu

F32 = jnp.float32
BF16 = jnp.bfloat16

GMLP_HEAD = 128
GMLP_CHUNK = 128
HGRN_HEAD = 128
N_EXPERTS = 8
TOP_K = 2
EPS = 1e-6
F_MIN = 1e-6

LANES = 128
GMLP_GROUP = 8
HGRN_C = 128
HGRN_GROUP = 8
HGRN_DIAG = 8
HGRN_DIAG_REF = 3
VMEM_LIMIT = 56 * 1024 * 1024
VMEM_LIMIT_IN_PROJ = 60 * 1024 * 1024


def _cparams(sem, vmem=VMEM_LIMIT):
    return pltpu.CompilerParams(dimension_semantics=sem, vmem_limit_bytes=vmem)


def _silu(x):
    hx = 0.5 * x
    return hx + hx * jnp.tanh(hx)


GELU_C1 = 0.7978845608028654
GELU_C2 = GELU_C1 * 0.044715


def _gelu_tanh(x):
    hx = 0.5 * x
    return hx + hx * jnp.tanh(x * (GELU_C1 + GELU_C2 * (x * x)))


def _rms(x, w):
    return x * lax.rsqrt(jnp.mean(x * x, axis=-1, keepdims=True) + EPS) * w


BF16_SUBLANES = 16


def _cast_block_specs(items, ni, nj, nk=1):
    arrays, in_specs, out_specs, out_shapes = [], [], [], []
    for item in items:
        w, row0, r = item if isinstance(item, tuple) else (item, 0, item.shape[0])
        c = w.shape[1]
        if c % nj == 0 and (c // nj) % LANES == 0 and r % (ni * nk * BF16_SUBLANES) == 0:
            block = (r // (ni * nk), c // nj)
            index = lambda i, j, k=0, off=0: (i * nk + k + off, j)
        else:
            assert r % (ni * nj * nk * BF16_SUBLANES) == 0, (w.shape, r, ni, nj, nk)
            block = (r // (ni * nj * nk), c)
            index = lambda i, j, k=0, off=0: ((i * nj + j) * nk + k + off, 0)
        assert row0 % block[0] == 0
        arrays.append(w)
        in_specs.append(pl.BlockSpec(block, functools.partial(index, off=row0 // block[0])))
        out_specs.append(pl.BlockSpec(block, index))
        out_shapes.append(jax.ShapeDtypeStruct((r, c), BF16))
    return arrays, in_specs, out_specs, out_shapes


def _narrow_slabs(src_refs, dst_refs):
    for src_ref, dst_ref in zip(src_refs, dst_refs):
        dst_ref[...] = src_ref[...].astype(BF16)


def _inproj_kernel(x_ref, nw_ref, w_ref, *rest):
    n_cast = (len(rest) - 2) // 2
    z_ref, h_scr = rest[n_cast], rest[-1]

    @pl.when(pl.program_id(1) == 0)
    def _():
        h_scr[...] = _rms(x_ref[...], nw_ref[...]).astype(BF16)

    _narrow_slabs(rest[:n_cast], rest[n_cast + 1:-1])
    z_ref[...] = jnp.dot(h_scr[...], w_ref[...], preferred_element_type=F32)


def in_proj(x, norm_w, w, cast=(), bm=1024, bn=1536):
    t, d = x.shape
    n = w.shape[1]
    bm = min(bm, t)
    ni, nj = t // bm, n // bn
    cast_arrays, cast_in, cast_out, cast_shapes = _cast_block_specs(cast, ni, nj)
    outs = pl.pallas_call(
        _inproj_kernel,
        grid=(ni, nj),
        in_specs=[
            pl.BlockSpec((bm, d), lambda i, j: (i, 0)),
            pl.BlockSpec((1, d), lambda i, j: (0, 0)),
            pl.BlockSpec((d, bn), lambda i, j: (0, j)),
        ] + cast_in,
        out_specs=[pl.BlockSpec((bm, bn), lambda i, j: (i, j))] + cast_out,
        out_shape=[jax.ShapeDtypeStruct((t, n), F32)] + cast_shapes,
        scratch_shapes=[pltpu.VMEM((bm, d), BF16)],
        compiler_params=_cparams(("parallel", "arbitrary"), VMEM_LIMIT_IN_PROJ),
        name="in_proj",
    )(x, norm_w.reshape(1, d), w, *cast_arrays)
    return outs[0], tuple(outs[1:])


def _gmlp_kernel(zu_ref, zv_ref, ws_ref, bs_ref, lnw_ref, lnb_ref, ow_ref, y_ref, wc_scr, *, n_heads, n_chunks):
    c = GMLP_CHUNK
    row = lax.broadcasted_iota(jnp.int32, (c, c), 0)
    col = lax.broadcasted_iota(jnp.int32, (c, c), 1)
    causal = row >= col
    ones = jnp.ones((GMLP_HEAD, GMLP_HEAD), BF16)
    lane_mean = lambda x: jnp.dot(x.astype(BF16), ones, preferred_element_type=F32) * (1.0 / GMLP_HEAD)

    for h in range(n_heads):
        wc_scr[h] = jnp.where(causal, ws_ref[h], 0.0).astype(BF16)

    def chunk(ci, carry):
        r0 = pl.multiple_of(ci * c, c)
        for g0 in range(0, n_heads, GMLP_GROUP):
            heads = range(g0, g0 + GMLP_GROUP)
            sls = [slice(h * GMLP_HEAD, (h + 1) * GMLP_HEAD) for h in heads]
            v = [_gelu_tanh(zv_ref[pl.ds(r0, c), sl]) for sl in sls]
            mu = [lane_mean(x) for x in v]
            vc = [x - m for x, m in zip(v, mu)]
            var = [lane_mean(x * x) for x in vc]
            vn = [(x * lax.rsqrt(s + EPS) * lnw_ref[:, sl] + lnb_ref[:, sl]).astype(BF16)
                  for x, s, sl in zip(vc, var, sls)]
            sv = [jnp.dot(wc_scr[h], x, preferred_element_type=F32) + bs_ref[:, h:h + 1] for h, x in zip(heads, vn)]
            y = [_gelu_tanh(zu_ref[pl.ds(r0, c), sl]) * x for sl, x in zip(sls, sv)]
            ms = [lane_mean(x * x) for x in y]
            for x, s, sl in zip(y, ms, sls):
                y_ref[pl.ds(r0, c), sl] = (x * lax.rsqrt(s + EPS) * ow_ref[:, sl]).astype(BF16)
        return carry

    lax.fori_loop(0, n_chunks, chunk, 0)


def gmlp_mixer(z, w_s, b_s, ln_w, ln_b, out_w, rows=512):
    t = z.shape[0]
    n_heads = w_s.shape[0]
    dg = n_heads * GMLP_HEAD
    rows = min(rows, t)
    kern = functools.partial(_gmlp_kernel, n_heads=n_heads, n_chunks=rows // GMLP_CHUNK)
    vec = lambda a: a.reshape(1, dg)
    return pl.pallas_call(
        kern,
        grid=(t // rows,),
        in_specs=[
            pl.BlockSpec((rows, dg), lambda i: (i, 0)),
            pl.BlockSpec((rows, dg), lambda i: (i, 1)),
            pl.BlockSpec((n_heads, GMLP_CHUNK, GMLP_CHUNK), lambda i: (0, 0, 0)),
            pl.BlockSpec((GMLP_CHUNK, n_heads), lambda i: (0, 0)),
            pl.BlockSpec((1, dg), lambda i: (0, 0)),
            pl.BlockSpec((1, dg), lambda i: (0, 0)),
            pl.BlockSpec((1, dg), lambda i: (0, 0)),
        ],
        out_specs=pl.BlockSpec((rows, dg), lambda i: (i, 0)),
        out_shape=jax.ShapeDtypeStruct((t, dg), BF16),
        scratch_shapes=[pltpu.VMEM((n_heads, GMLP_CHUNK, GMLP_CHUNK), BF16)],
        compiler_params=_cparams(("parallel",)),
        name="gmlp_mixer",
    )(z, z, w_s, b_s.T, vec(ln_w), vec(ln_b), vec(out_w))


def _hgrn_level_terms(a, q, k, blk):
    c, d = a.shape
    half = blk // 2
    nb = c // blk
    lower = lambda v: v.reshape(nb, blk, d)[:, :half, :]
    upper = lambda v: v.reshape(nb, blk, d)[:, half:, :]
    ref = a.reshape(nb, blk, d)[:, half - 1:half, :]
    qh = upper(q) * jnp.exp2(upper(a) - ref)
    kh = lower(k) * jnp.exp2(ref - lower(a))
    zeros = jnp.zeros_like(qh)
    qh = jnp.concatenate([zeros, qh], axis=1).reshape(c, d)
    kh = jnp.concatenate([kh, zeros], axis=1).reshape(c, d)
    return qh.astype(BF16), kh.astype(BF16)


def _hgrn_kernel(zq_ref, zf_ref, zi_ref, zg_ref, lb_ref, ow_ref, y_ref, st_ref, *, n_heads, n_chunks):
    c, d = HGRN_C, HGRN_HEAD
    nt = (((1,), (1,)), ((), ()))
    tn = (((0,), (0,)), ((), ()))

    @pl.when(pl.program_id(1) == 0)
    def _():
        st_ref[...] = jnp.zeros_like(st_ref)

    row = lax.broadcasted_iota(jnp.int32, (c, c), 0)
    col = lax.broadcasted_iota(jnp.int32, (c, c), 1)
    tril = (row >= col).astype(BF16)
    diag_mask = (row >= col) & ((row // HGRN_DIAG) == (col // HGRN_DIAG))
    level_masks = []
    blk = 2 * HGRN_DIAG
    while blk <= c:
        level_masks.append((blk, None if blk == c else (row // blk) == (col // blk)))
        blk *= 2

    def nt_dot(x, y):
        return lax.dot_general(x, y, nt, preferred_element_type=F32)

    def chunk(ci, carry):
        r0 = pl.multiple_of(ci * c, c)
        for g0 in range(0, n_heads, HGRN_GROUP):
            heads = list(range(g0, g0 + HGRN_GROUP))
            sls = [slice(h * d, (h + 1) * d) for h in heads]
            f = [0.5 * (1.0 + lb_ref[:, sl]) + (0.5 * (1.0 - lb_ref[:, sl])) * jnp.tanh(0.5 * zf_ref[pl.ds(r0, c), sl])
                 for sl in sls]
            logf = [jnp.log2(jnp.clip(x, F_MIN, 1.0)) for x in f]
            pieces = [jnp.concatenate(_split_bf16(x), axis=1) for x in logf]
            cs = [jnp.dot(tril, x, preferred_element_type=F32) for x in pieces]
            a = [x[:, :d] + x[:, d:] for x in cs]
            k = [1.0 - x for x in f]
            q = [_silu(zq_ref[pl.ds(r0, c), sl]) for sl in sls]
            iv = [zi_ref[pl.ds(r0, c), sl].astype(BF16) for sl in sls]

            dd = []
            for x in a:
                a3 = x.reshape(c // HGRN_DIAG, HGRN_DIAG, d)
                dd.append((a3 - a3[:, HGRN_DIAG_REF:HGRN_DIAG_REF + 1, :]).reshape(c, d))
            sd = [nt_dot((qx * jnp.exp2(x)).astype(BF16), (kx * jnp.exp2(-x)).astype(BF16))
                  for qx, kx, x in zip(q, k, dd)]

            scores = None
            for blk, same_block in reversed(level_masks):
                terms = [_hgrn_level_terms(ax, qx, kx, blk) for ax, qx, kx in zip(a, q, k)]
                prods = [nt_dot(qh, kh) for qh, kh in terms]
                scores = prods if scores is None else [jnp.where(same_block, p, s) for s, p in zip(scores, prods)]
            scores = [jnp.where(diag_mask, x, s) for x, s in zip(sd, scores)]

            st = [st_ref[h] for h in heads]
            qe = [(qx * jnp.exp2(ax)).astype(BF16) for qx, ax in zip(q, a)]
            o = [jnp.dot(s.astype(BF16), ivx, preferred_element_type=F32) + nt_dot(qx, sx.astype(BF16))
                 for s, ivx, qx, sx in zip(scores, iv, qe, st)]
            kd = [(kx * jnp.exp2(ax[c - 1:c, :] - ax)).astype(BF16) for kx, ax in zip(k, a)]
            for h, ax, sx, ivx, kx in zip(heads, a, st, iv, kd):
                st_ref[h] = jnp.exp2(ax[c - 1:c, :]) * sx + lax.dot_general(ivx, kx, tn, preferred_element_type=F32)
            for x, sl in zip(o, sls):
                y_ref[pl.ds(r0, c), sl] = (_rms(x, ow_ref[:, sl]) * _silu(zg_ref[pl.ds(r0, c), sl])).astype(BF16)
        return carry

    lax.fori_loop(0, n_chunks, chunk, 0)


def hgrn2_mixer(z, lb, out_w, batch, col0, rows=512):
    t = z.shape[0]
    s = t // batch
    dh = lb.shape[0]
    n_heads = dh // HGRN_HEAD
    rows = min(rows, s)
    spb = s // rows
    kern = functools.partial(_hgrn_kernel, n_heads=n_heads, n_chunks=rows // HGRN_C)
    zspec = lambda j: pl.BlockSpec((rows, dh), lambda b, i, j=j: (b * spb + i, col0 + j))
    return pl.pallas_call(
        kern,
        grid=(batch, spb),
        in_specs=[zspec(0), zspec(1), zspec(2), zspec(3),
                  pl.BlockSpec((1, dh), lambda b, i: (0, 0)),
                  pl.BlockSpec((1, dh), lambda b, i: (0, 0))],
        out_specs=pl.BlockSpec((rows, dh), lambda b, i: (b * spb + i, 0)),
        out_shape=jax.ShapeDtypeStruct((t, dh), BF16),
        scratch_shapes=[pltpu.VMEM((n_heads, HGRN_HEAD, HGRN_HEAD), F32)],
        compiler_params=_cparams(("parallel", "arbitrary")),
        name="hgrn2_mixer",
    )(z, z, z, z, lb.reshape(1, dh), out_w.reshape(1, dh))


ROUTE_E0, ROUTE_E1, ROUTE_W0, ROUTE_W1, ROUTE_R0, ROUTE_R1 = range(6)


def _split_bf16(v):
    hi = v.astype(BF16)
    return hi, (v - hi.astype(F32)).astype(BF16)


def _route_rows(h, rw_pieces, carry_ref):
    bm = h.shape[0]
    h_hi, h_lo = _split_bf16(h)
    hh_hl = jnp.dot(h_hi, rw_pieces, preferred_element_type=F32)
    logits = hh_hl[:, :LANES] + hh_hl[:, LANES:] + jnp.dot(h_lo, rw_pieces[:, :LANES], preferred_element_type=F32)
    lane = lax.broadcasted_iota(jnp.int32, (bm, LANES), 1)
    neg = jnp.float32(-jnp.inf)
    logits = jnp.where(lane < N_EXPERTS, logits, neg)
    v0 = jnp.max(logits, axis=-1, keepdims=True)
    e0 = jnp.min(jnp.where(logits == v0, lane, LANES), axis=-1, keepdims=True)
    rest = jnp.where(lane == e0, neg, logits)
    v1 = jnp.max(rest, axis=-1, keepdims=True)
    e1 = jnp.min(jnp.where(rest == v1, lane, LANES), axis=-1, keepdims=True)
    ex = jnp.exp(v1 - v0)
    w0 = 1.0 / (1.0 + ex)
    w1 = ex / (1.0 + ex)

    onehot = ((lane == e0) | (lane == e1)).astype(F32)
    row = lax.broadcasted_iota(jnp.int32, (bm, bm), 0)
    col = lax.broadcasted_iota(jnp.int32, (bm, bm), 1)
    before = (row > col).astype(BF16)
    excl = jnp.dot(before, onehot.astype(BF16), preferred_element_type=F32) + carry_ref[...]
    r0 = jnp.sum(jnp.where(lane == e0, excl, 0.0), axis=-1, keepdims=True)
    r1 = jnp.sum(jnp.where(lane == e1, excl, 0.0), axis=-1, keepdims=True)
    carry_ref[...] += jnp.sum(onehot, axis=0, keepdims=True)

    out = jnp.zeros((bm, LANES), F32)
    for idx, val in ((ROUTE_E0, e0.astype(F32)), (ROUTE_E1, e1.astype(F32)), (ROUTE_W0, w0), (ROUTE_W1, w1),
                     (ROUTE_R0, r0), (ROUTE_R1, r1)):
        out = jnp.where(lane == idx, val, out)
    return out


def _outproj_kernel(*refs, da, moe):
    if moe:
        (x_ref, ya_ref, yb_ref, wo_ref, nw_ref, rw_ref, xo_ref, h_ref, route_ref, cnt_ref,
         wo_scr, rw_scr, carry_scr) = refs
    else:
        x_ref, ya_ref, yb_ref, wo_ref, nw_ref, xo_ref, h_ref, wo_scr = refs

    @pl.when(pl.program_id(0) == 0)
    def _():
        wo_scr[...] = wo_ref[...].astype(BF16)
        if moe:
            rw_hi, rw_lo = _split_bf16(rw_ref[...])
            rw_scr[:, :LANES] = rw_hi
            rw_scr[:, LANES:] = rw_lo
            carry_scr[...] = jnp.zeros_like(carry_scr)

    acc = x_ref[...] + jnp.dot(ya_ref[...], wo_scr[:da, :], preferred_element_type=F32) \
        + jnp.dot(yb_ref[...], wo_scr[da:, :], preferred_element_type=F32)
    xo_ref[...] = acc
    h = _rms(acc, nw_ref[...])
    h_ref[...] = h.astype(h_ref.dtype)
    if moe:
        route_ref[...] = _route_rows(h, rw_scr[...], carry_scr)
        cnt_ref[...] = carry_scr[...]


def out_proj(x, y_a, y_b, w_o, layer, norm_w, router_w=None, bm=None):
    t, d = x.shape
    da, db = y_a.shape[1], y_b.shape[1]
    moe = router_w is not None
    bm = min(bm or (256 if moe else 512), t)
    row_spec = lambda w: pl.BlockSpec((bm, w), lambda i: (i, 0))
    const = lambda shape: pl.BlockSpec(shape, lambda i: (0,) * len(shape), pipeline_mode=pl.Buffered(1))
    w_spec = pl.BlockSpec((None, da + db, d), lambda i: (layer, 0, 0), pipeline_mode=pl.Buffered(1))
    in_specs = [row_spec(d), row_spec(da), row_spec(db), w_spec, const((1, d))]
    out_specs = [row_spec(d), row_spec(d)]
    out_shape = [jax.ShapeDtypeStruct((t, d), F32), jax.ShapeDtypeStruct((t, d), F32 if moe else BF16)]
    scratch = [pltpu.VMEM((da + db, d), BF16)]
    args = [x, y_a, y_b, w_o, norm_w.reshape(1, d)]
    if moe:
        in_specs.append(const((d, LANES)))
        args.append(jnp.zeros((d, LANES), F32).at[:, :N_EXPERTS].set(router_w))
        out_specs += [row_spec(LANES), pl.BlockSpec((1, LANES), lambda i: (0, 0))]
        out_shape += [jax.ShapeDtypeStruct((t, LANES), F32), jax.ShapeDtypeStruct((1, LANES), F32)]
        scratch += [pltpu.VMEM((d, 2 * LANES), BF16), pltpu.VMEM((1, LANES), F32)]
    return pl.pallas_call(
        functools.partial(_outproj_kernel, da=da, moe=moe),
        grid=(t // bm,),
        in_specs=in_specs,
        out_specs=out_specs,
        out_shape=out_shape,
        scratch_shapes=scratch,
        compiler_params=_cparams(("arbitrary",)),
        name="out_proj_route" if moe else "out_proj",
    )(*args)


def _ffn_kernel(x_hbm, h_ref, wg_ref, wu_ref, wd_ref, *rest, n_pass):
    n_cast = (len(rest) - 2) // 2
    o_ref, sem = rest[n_cast], rest[-1]
    i, j, k = pl.program_id(0), pl.program_id(1), pl.program_id(2)
    bm = o_ref.shape[0]
    sub = bm // n_pass

    def x_copy(p):
        src = x_hbm.at[pl.ds(pl.multiple_of(i * bm + p * sub, sub), sub)]
        return pltpu.make_async_copy(src, o_ref.at[pl.ds(p * sub, sub)], sem.at[p])

    @pl.when((j == 0) & (k == 0))
    def _():
        for p in range(n_pass):
            x_copy(p).start()

    for p in range(n_pass):
        @pl.when((j == 0) & (k == p))
        def _():
            x_copy(p).wait()

    _narrow_slabs(rest[:n_cast], rest[n_cast + 1:-1])

    rows = pl.ds(pl.multiple_of(k * sub, sub), sub)
    hb = h_ref[rows, :]
    g = jnp.dot(hb, wg_ref[...], preferred_element_type=F32)
    u = jnp.dot(hb, wu_ref[...], preferred_element_type=F32)
    act = (_silu(g) * u).astype(BF16)
    o_ref[rows, :] += jnp.dot(act, wd_ref[...], preferred_element_type=F32)


def dense_ffn(x, h, wg, wu, wd, cast=(), bm=1024, bf=512, n_pass=2):
    t, d = x.shape
    f = wg.shape[1]
    bm = min(bm, t)
    ni, nj = t // bm, f // bf
    cast_arrays, cast_in, cast_out, cast_shapes = _cast_block_specs(cast, ni, nj, n_pass)
    outs = pl.pallas_call(
        functools.partial(_ffn_kernel, n_pass=n_pass),
        grid=(ni, nj, n_pass),
        in_specs=[
            pl.BlockSpec(memory_space=pl.ANY),
            pl.BlockSpec((bm, d), lambda i, j, k: (i, 0)),
            pl.BlockSpec((d, bf), lambda i, j, k: (0, j)),
            pl.BlockSpec((d, bf), lambda i, j, k: (0, j)),
            pl.BlockSpec((bf, d), lambda i, j, k: (j, 0)),
        ] + cast_in,
        out_specs=[pl.BlockSpec((bm, d), lambda i, j, k: (i, 0))] + cast_out,
        out_shape=[jax.ShapeDtypeStruct((t, d), F32)] + cast_shapes,
        scratch_shapes=[pltpu.SemaphoreType.DMA((n_pass,))],
        compiler_params=_cparams(("parallel", "arbitrary", "arbitrary")),
        name="dense_ffn",
    )(x, h, wg, wu, wd, *cast_arrays)
    return outs[0], tuple(outs[1:])


def _row_copy(src_ref, dst_ref, sem, src_row, dst_row):
    return pltpu.make_async_copy(src_ref.at[pl.ds(src_row, 1)], dst_ref.at[pl.ds(dst_row, 1)], sem)


ROW_DMA_UNROLL = 8
GMM_TILE = 1024
GMM_SUBTILES = 2


def _dispatch_kernel(zlo_ref, zhi_ref, pos_ref, h_ref, xs_ref, zero_scr, sem, zsem):
    bt = h_ref.shape[0]

    @pl.when(pl.program_id(0) == 0)
    def _():
        zero_scr[...] = jnp.zeros_like(zero_scr)
        zr = zero_scr.shape[0]

        def zero_rows(e, begin):
            lo, hi = zlo_ref[e], zhi_ref[e]
            mid = jnp.minimum(hi, (lo + zr - 1) // zr * zr)

            def one(r, carry):
                cp = _row_copy(zero_scr, xs_ref, zsem, 0, r)
                cp.start() if begin else cp.wait()
                return carry

            def slab(b, carry):
                cp = pltpu.make_async_copy(zero_scr, xs_ref.at[pl.ds(pl.multiple_of(b * zr, zr), zr)], zsem)
                cp.start() if begin else cp.wait()
                return carry

            lax.fori_loop(lo, mid, one, 0)
            lax.fori_loop(mid // zr, hi // zr, slab, 0)

        for e in range(N_EXPERTS):
            zero_rows(e, True)
        for e in range(N_EXPERTS):
            zero_rows(e, False)

    def start(r, carry):
        for kk in range(TOP_K):
            _row_copy(h_ref, xs_ref, sem, r, pos_ref[0, TOP_K * r + kk]).start(priority=kk % 2)
        return carry

    lax.fori_loop(0, bt, start, 0, unroll=ROW_DMA_UNROLL)
    for kk in range(TOP_K):
        pltpu.make_async_copy(h_ref, xs_ref.at[pl.ds(0, bt)], sem).wait()


def moe_dispatch(h, pos, zero_lo, zero_hi, n_rows, bt=256):
    t, d = h.shape
    bt = min(bt, t)
    pos2 = pos.reshape(t // bt, 1, TOP_K * bt)
    return pl.pallas_call(
        _dispatch_kernel,
        grid_spec=pltpu.PrefetchScalarGridSpec(
            num_scalar_prefetch=2,
            grid=(t // bt,),
            in_specs=[
                pl.BlockSpec((None, 1, TOP_K * bt), lambda i, zl, zh: (i, 0, 0), memory_space=pltpu.SMEM),
                pl.BlockSpec((bt, d), lambda i, zl, zh: (i, 0)),
            ],
            out_specs=pl.BlockSpec(memory_space=pl.ANY),
            scratch_shapes=[pltpu.VMEM((8, d), F32), pltpu.SemaphoreType.DMA(()), pltpu.SemaphoreType.DMA(())],
        ),
        out_shape=jax.ShapeDtypeStruct((n_rows, d), F32),
        compiler_params=_cparams(("arbitrary",)),
        name="moe_dispatch",
    )(zero_lo, zero_hi, pos2, h)


def _gmm_kernel(te_ref, tr_ref, nv_ref, xs_ref, wg_ref, wu_ref, wd_ref, ys_ref, xb_scr):
    i, j = pl.program_id(0), pl.program_id(1)
    bm = xs_ref.shape[0]
    rows = tr_ref[i]
    sub = bm // GMM_SUBTILES

    @pl.when(j == 0)
    def _():
        xb_scr[...] = xs_ref[...].astype(BF16)
        ys_ref[...] = jnp.zeros_like(ys_ref)

    def ffn(r0, n):
        xb = xb_scr[r0:r0 + n, :]
        g = jnp.dot(xb, wg_ref[...], preferred_element_type=F32)
        u = jnp.dot(xb, wu_ref[...], preferred_element_type=F32)
        act = (_silu(g) * u).astype(BF16)
        ys_ref[r0:r0 + n, :] += jnp.dot(act, wd_ref[...], preferred_element_type=F32)

    full = rows == bm

    @pl.when(full)
    def _():
        for s in range(GMM_SUBTILES):
            ffn(s * sub, sub)

    for s in range(GMM_SUBTILES):
        left = rows - s * sub

        @pl.when(jnp.logical_not(full) & (left > sub // 2))
        def _():
            ffn(s * sub, sub)

        @pl.when((left > 0) & (left <= sub // 2))
        def _():
            ffn(s * sub, sub // 2)


def moe_grouped_ffn(xs, tile_expert, tile_rows, n_valid, wg, wu, wd, bm, bf=512):
    p, d = xs.shape
    f = wg.shape[2]
    nf = f // bf

    def row_map(i, j, te, tr, nv):
        return (jnp.minimum(i, nv[0] - 1), 0)

    def fcol(i, j, nv):
        return jnp.where(i < nv[0], j, nf - 1)

    return pl.pallas_call(
        _gmm_kernel,
        grid_spec=pltpu.PrefetchScalarGridSpec(
            num_scalar_prefetch=3,
            grid=(p // bm, nf),
            in_specs=[
                pl.BlockSpec((bm, d), row_map),
                pl.BlockSpec((None, d, bf), lambda i, j, te, tr, nv: (te[i], 0, fcol(i, j, nv))),
                pl.BlockSpec((None, d, bf), lambda i, j, te, tr, nv: (te[i], 0, fcol(i, j, nv))),
                pl.BlockSpec((None, bf, d), lambda i, j, te, tr, nv: (te[i], fcol(i, j, nv), 0)),
            ],
            out_specs=pl.BlockSpec((bm, d), lambda i, j, te, tr, nv: (i, 0)),
            scratch_shapes=[pltpu.VMEM((bm, d), BF16)],
        ),
        out_shape=jax.ShapeDtypeStruct((p, d), F32),
        compiler_params=_cparams(("arbitrary", "arbitrary")),
        name="moe_grouped_ffn",
    )(tile_expert, tile_rows, n_valid, xs, wg, wu, wd)


def _combine_kernel(pos_ref, pos_next_ref, x_ref, route_ref, nw_ref, ys_ref, o_ref, y_scr, sem, *, final_norm):
    bt = x_ref.shape[0]
    i = pl.program_id(0)
    slot = i % 2

    def gather(p_ref, s, unroll):
        def start(r, carry):
            for kk in range(TOP_K):
                _row_copy(ys_ref, y_scr.at[s, kk], sem.at[s], p_ref[0, TOP_K * r + kk], r).start(priority=kk % 2)
            return carry
        lax.fori_loop(0, bt, start, 0, unroll=unroll)

    def wait(s):
        for kk in range(TOP_K):
            pltpu.make_async_copy(ys_ref.at[pl.ds(0, bt)], y_scr.at[s, kk], sem.at[s]).wait()

    @pl.when(i == 0)
    def _():
        gather(pos_ref, 0, ROW_DMA_UNROLL)

    wait(slot)
    gather(pos_next_ref, 1 - slot, True)
    w0 = route_ref[:, ROUTE_W0:ROUTE_W0 + 1]
    w1 = route_ref[:, ROUTE_W1:ROUTE_W1 + 1]
    out = x_ref[...] + (w0 * y_scr[slot, 0] + w1 * y_scr[slot, 1])
    if final_norm:
        out = _rms(out, nw_ref[...])
    o_ref[...] = out

    @pl.when(i + 1 == pl.num_programs(0))
    def _():
        wait(1 - slot)


def moe_combine(x, route, pos, ys, norm_w, final_norm, bt=256):
    t, d = x.shape
    bt = min(bt, t)
    n = t // bt
    pos2 = pos.reshape(n, 1, TOP_K * bt)
    pos_spec = lambda nxt: pl.BlockSpec((None, 1, TOP_K * bt), lambda i: (jnp.minimum(i + nxt, n - 1), 0, 0),
                                        memory_space=pltpu.SMEM)
    return pl.pallas_call(
        functools.partial(_combine_kernel, final_norm=final_norm),
        grid=(n,),
        in_specs=[
            pos_spec(0),
            pos_spec(1),
            pl.BlockSpec((bt, d), lambda i: (i, 0)),
            pl.BlockSpec((bt, LANES), lambda i: (i, 0)),
            pl.BlockSpec((1, d), lambda i: (0, 0)),
            pl.BlockSpec(memory_space=pl.ANY),
        ],
        out_specs=pl.BlockSpec((bt, d), lambda i: (i, 0)),
        out_shape=jax.ShapeDtypeStruct((t, d), F32),
        scratch_shapes=[pltpu.VMEM((2, TOP_K, bt, d), F32), pltpu.SemaphoreType.DMA((2,))],
        compiler_params=_cparams(("arbitrary",)),
        name="moe_combine",
    )(pos2, pos2, x, route, norm_w.reshape(1, d), ys)


def moe_ffn(x, h, route, counts, wg, wu, wd, norm_w, final_norm, bm=GMM_TILE):
    t, d = x.shape

    cnt = counts[0, :N_EXPERTS].astype(jnp.int32)
    tiles = (cnt + bm - 1) // bm
    tile_end = jnp.cumsum(tiles)
    offset = (tile_end - tiles) * bm
    n_tiles = (TOP_K * t) // bm + N_EXPERTS
    n_valid = tile_end[-1:].astype(jnp.int32)
    tile_id = jnp.minimum(jnp.arange(n_tiles, dtype=jnp.int32), n_valid[0] - 1)
    tile_expert = jnp.sum((tile_end[None, :] <= tile_id[:, None]).astype(jnp.int32), axis=1)
    tile_in_group = tile_id - (tile_end - tiles)[tile_expert]
    tile_rows = jnp.clip(cnt[tile_expert] - tile_in_group * bm, 0, bm)
    tile_rows = jnp.where(jnp.arange(n_tiles) < n_valid[0], tile_rows, 0).astype(jnp.int32)
    experts = route[:, ROUTE_E0:ROUTE_E1 + 1].astype(jnp.int32)
    ranks = route[:, ROUTE_R0:ROUTE_R1 + 1].astype(jnp.int32)
    pos = offset[experts] + ranks
    zero_lo = offset + cnt
    zero_hi = jnp.concatenate([offset[1:], jnp.full((1,), n_tiles * bm, jnp.int32)])

    xs = moe_dispatch(h, pos, zero_lo, zero_hi, n_tiles * bm)
    ys = moe_grouped_ffn(xs, tile_expert, tile_rows, n_valid, wg, wu, wd, bm)
    return moe_combine(x, route, pos, ys, norm_w, final_norm)


def _norm_kernel(x_ref, nw_ref, o_ref):
    o_ref[...] = _rms(x_ref[...], nw_ref[...])


def final_norm(x, norm_w, bm=512):
    t, d = x.shape
    bm = min(bm, t)
    return pl.pallas_call(
        _norm_kernel,
        grid=(t // bm,),
        in_specs=[pl.BlockSpec((bm, d), lambda i: (i, 0)), pl.BlockSpec((1, d), lambda i: (0, 0))],
        out_specs=pl.BlockSpec((bm, d), lambda i: (i, 0)),
        out_shape=jax.ShapeDtypeStruct((t, d), F32),
        compiler_params=_cparams(("parallel",)),
        name="final_norm",
    )(x, norm_w.reshape(1, d))


def kernel(x, mix_norm_w, w_in, gmlp_w_s, gmlp_b_s, gmlp_ln_w, gmlp_ln_b, gmlp_out_w, hgrn_lb_logits, hgrn_out_w, w_o, ffn_norm_w, dense_w_gate, dense_w_up, dense_w_down, router_w, expert_w_gate, expert_w_up, expert_w_down, final_norm_w):
    batch, seq, d = x.shape
    depth = w_in.shape[0]
    d_gmlp = gmlp_ln_w.shape[1]
    d_hgrn = hgrn_out_w.shape[1]
    assert d_gmlp == d_hgrn and seq % HGRN_C == 0 and seq % GMLP_CHUNK == 0

    lbs = jax.nn.softmax(hgrn_lb_logits.astype(F32), axis=0)
    lbs = jnp.cumsum(lbs, axis=0) - lbs[0:1]

    xt = x.reshape(batch * seq, d)
    w_in_bf16 = w_in[0].astype(BF16)
    for l in range(depth):
        last = l == depth - 1
        j = l // 2
        soon = [] if last else [(w_in.reshape(depth * d, -1), (l + 1) * d, d)]
        if l % 2 == 0:
            soon += [dense_w_gate[j], dense_w_up[j], dense_w_down[j]]
        z, narrowed = in_proj(xt, mix_norm_w[l], w_in_bf16, cast=tuple(soon))
        narrowed = list(narrowed)
        if not last:
            w_in_bf16 = narrowed.pop(0)
        y_a = gmlp_mixer(z, gmlp_w_s[l], gmlp_b_s[l], gmlp_ln_w[l], gmlp_ln_b[l], gmlp_out_w[l])
        y_b = hgrn2_mixer(z, lbs[l], hgrn_out_w[l], batch, col0=2 * d_gmlp // d_hgrn)
        if l % 2 == 0:
            xt, h = out_proj(xt, y_a, y_b, w_o, l, ffn_norm_w[l])
            experts = () if last else tuple(w[j].reshape(-1, w.shape[-1])
                                            for w in (expert_w_gate, expert_w_up, expert_w_down))
            xt, experts_bf16 = dense_ffn(xt, h, *narrowed, cast=experts)
            if last:
                xt = final_norm(xt, final_norm_w)
        else:
            xt, h, route, counts = out_proj(xt, y_a, y_b, w_o, l, ffn_norm_w[l], router_w=router_w[j])
            wg, wu, wd = (w.reshape(s.shape[1:]) for w, s in
                          zip(experts_bf16, (expert_w_gate, expert_w_up, expert_w_down)))
            xt = moe_ffn(xt, h, route, counts, wg, wu, wd, final_norm_w, final_norm=last)
    return xt.reshape(batch, seq, d)
```

```python
import functools

import jax
import jax.numpy as jnp
from jax import lax
from jax.experimental import pallas as pl
from jax.experimental.pallas import tpu as pltpu

F32 = jnp.float32
BF16 = jnp.bfloat16

GMLP_HEAD = 128
GMLP_CHUNK = 128
HGRN_HEAD = 128
N_EXPERTS = 8
TOP_K = 2
EPS = 1e-6
F_MIN = 1e-6

LANES = 128
GMLP_GROUP = 8
HGRN_C = 128
HGRN_GROUP = 8
HGRN_DIAG = 8
HGRN_DIAG_REF = 3
VMEM_LIMIT = 56 * 1024 * 1024
VMEM_LIMIT_IN_PROJ = 60 * 1024 * 1024


def _cparams(sem, vmem=VMEM_LIMIT):
    return pltpu.CompilerParams(dimension_semantics=sem, vmem_limit_bytes=vmem)


def _silu(x):
    hx = 0.5 * x
    return hx + hx * jnp.tanh(hx)


GELU_C1 = 0.7978845608028654
GELU_C2 = GELU_C1 * 0.044715


def _gelu_tanh(x):
    hx = 0.5 * x
    return hx + hx * jnp.tanh(x * (GELU_C1 + GELU_C2 * (x * x)))


def _rms(x, w):
    return x * lax.rsqrt(jnp.mean(x * x, axis=-1, keepdims=True) + EPS) * w


BF16_SUBLANES = 16


def _cast_block_specs(items, ni, nj, nk=1):
    arrays, in_specs, out_specs, out_shapes = [], [], [], []
    for item in items:
        w, row0, r = item if isinstance(item, tuple) else (item, 0, item.shape[0])
        c = w.shape[1]
        if c % nj == 0 and (c // nj) % LANES == 0 and r % (ni * nk * BF16_SUBLANES) == 0:
            block = (r // (ni * nk), c // nj)
            index = lambda i, j, k=0, off=0: (i * nk + k + off, j)
        else:
            assert r % (ni * nj * nk * BF16_SUBLANES) == 0, (w.shape, r, ni, nj, nk)
            block = (r // (ni * nj * nk), c)
            index = lambda i, j, k=0, off=0: ((i * nj + j) * nk + k + off, 0)
        assert row0 % block[0] == 0
        arrays.append(w)
        in_specs.append(pl.BlockSpec(block, functools.partial(index, off=row0 // block[0])))
        out_specs.append(pl.BlockSpec(block, index))
        out_shapes.append(jax.ShapeDtypeStruct((r, c), BF16))
    return arrays, in_specs, out_specs, out_shapes


def _narrow_slabs(src_refs, dst_refs):
    for src_ref, dst_ref in zip(src_refs, dst_refs):
        dst_ref[...] = src_ref[...].astype(BF16)


def _inproj_kernel(x_ref, nw_ref, w_ref, *rest):
    n_cast = (len(rest) - 2) // 2
    z_ref, h_scr = rest[n_cast], rest[-1]

    @pl.when(pl.program_id(1) == 0)
    def _():
        h_scr[...] = _rms(x_ref[...], nw_ref[...]).astype(BF16)

    _narrow_slabs(rest[:n_cast], rest[n_cast + 1:-1])
    z_ref[...] = jnp.dot(h_scr[...], w_ref[...], preferred_element_type=F32)


def in_proj(x, norm_w, w, cast=(), bm=1024, bn=1536):
    t, d = x.shape
    n = w.shape[1]
    bm = min(bm, t)
    ni, nj = t // bm, n // bn
    cast_arrays, cast_in, cast_out, cast_shapes = _cast_block_specs(cast, ni, nj)
    outs = pl.pallas_call(
        _inproj_kernel,
        grid=(ni, nj),
        in_specs=[
            pl.BlockSpec((bm, d), lambda i, j: (i, 0)),
            pl.BlockSpec((1, d), lambda i, j: (0, 0)),
            pl.BlockSpec((d, bn), lambda i, j: (0, j)),
        ] + cast_in,
        out_specs=[pl.BlockSpec((bm, bn), lambda i, j: (i, j))] + cast_out,
        out_shape=[jax.ShapeDtypeStruct((t, n), F32)] + cast_shapes,
        scratch_shapes=[pltpu.VMEM((bm, d), BF16)],
        compiler_params=_cparams(("parallel", "arbitrary"), VMEM_LIMIT_IN_PROJ),
        name="in_proj",
    )(x, norm_w.reshape(1, d), w, *cast_arrays)
    return outs[0], tuple(outs[1:])


def _gmlp_kernel(zu_ref, zv_ref, ws_ref, bs_ref, lnw_ref, lnb_ref, ow_ref, y_ref, wc_scr, *, n_heads, n_chunks):
    c = GMLP_CHUNK
    row = lax.broadcasted_iota(jnp.int32, (c, c), 0)
    col = lax.broadcasted_iota(jnp.int32, (c, c), 1)
    causal = row >= col
    ones = jnp.ones((GMLP_HEAD, GMLP_HEAD), BF16)
    lane_mean = lambda x: jnp.dot(x.astype(BF16), ones, preferred_element_type=F32) * (1.0 / GMLP_HEAD)

    for h in range(n_heads):
        wc_scr[h] = jnp.where(causal, ws_ref[h], 0.0).astype(BF16)

    def chunk(ci, carry):
        r0 = pl.multiple_of(ci * c, c)
        for g0 in range(0, n_heads, GMLP_GROUP):
            heads = range(g0, g0 + GMLP_GROUP)
            sls = [slice(h * GMLP_HEAD, (h + 1) * GMLP_HEAD) for h in heads]
            v = [_gelu_tanh(zv_ref[pl.ds(r0, c), sl]) for sl in sls]
            mu = [lane_mean(x) for x in v]
            vc = [x - m for x, m in zip(v, mu)]
            var = [lane_mean(x * x) for x in vc]
            vn = [(x * lax.rsqrt(s + EPS) * lnw_ref[:, sl] + lnb_ref[:, sl]).astype(BF16)
                  for x, s, sl in zip(vc, var, sls)]
            sv = [jnp.dot(wc_scr[h], x, preferred_element_type=F32) + bs_ref[:, h:h + 1] for h, x in zip(heads, vn)]
            y = [_gelu_tanh(zu_ref[pl.ds(r0, c), sl]) * x for sl, x in zip(sls, sv)]
            ms = [lane_mean(x * x) for x in y]
            for x, s, sl in zip(y, ms, sls):
                y_ref[pl.ds(r0, c), sl] = (x * lax.rsqrt(s + EPS) * ow_ref[:, sl]).astype(BF16)
        return carry

    lax.fori_loop(0, n_chunks, chunk, 0)


def gmlp_mixer(z, w_s, b_s, ln_w, ln_b, out_w, rows=1024):
    t = z.shape[0]
    n_heads = w_s.shape[0]
    dg = n_heads * GMLP_HEAD
    rows = min(rows, t)
    kern = functools.partial(_gmlp_kernel, n_heads=n_heads, n_chunks=rows // GMLP_CHUNK)
    vec = lambda a: a.reshape(1, dg)
    return pl.pallas_call(
        kern,
        grid=(t // rows,),
        in_specs=[
            pl.BlockSpec((rows, dg), lambda i: (i, 0)),
            pl.BlockSpec((rows, dg), lambda i: (i, 1)),
            pl.BlockSpec((n_heads, GMLP_CHUNK, GMLP_CHUNK), lambda i: (0, 0, 0)),
            pl.BlockSpec((GMLP_CHUNK, n_heads), lambda i: (0, 0)),
            pl.BlockSpec((1, dg), lambda i: (0, 0)),
            pl.BlockSpec((1, dg), lambda i: (0, 0)),
            pl.BlockSpec((1, dg), lambda i: (0, 0)),
        ],
        out_specs=pl.BlockSpec((rows, dg), lambda i: (i, 0)),
        out_shape=jax.ShapeDtypeStruct((t, dg), BF16),
        scratch_shapes=[pltpu.VMEM((n_heads, GMLP_CHUNK, GMLP_CHUNK), BF16)],
        compiler_params=_cparams(("parallel",)),
        name="gmlp_mixer",
    )(z, z, w_s, b_s.T, vec(ln_w), vec(ln_b), vec(out_w))


def _hgrn_level_terms(a, q, k, blk):
    c, d = a.shape
    half = blk // 2
    nb = c // blk
    lower = lambda v: v.reshape(nb, blk, d)[:, :half, :]
    upper = lambda v: v.reshape(nb, blk, d)[:, half:, :]
    ref = a.reshape(nb, blk, d)[:, half - 1:half, :]
    qh = upper(q) * jnp.exp2(upper(a) - ref)
    kh = lower(k) * jnp.exp2(ref - lower(a))
    zeros = jnp.zeros_like(qh)
    qh = jnp.concatenate([zeros, qh], axis=1).reshape(c, d)
    kh = jnp.concatenate([kh, zeros], axis=1).reshape(c, d)
    return qh.astype(BF16), kh.astype(BF16)


def _hgrn_kernel(zq_ref, zf_ref, zi_ref, zg_ref, lb_ref, ow_ref, y_ref, st_ref, *, n_heads, n_chunks):
    c, d = HGRN_C, HGRN_HEAD
    nt = (((1,), (1,)), ((), ()))
    tn = (((0,), (0,)), ((), ()))

    @pl.when(pl.program_id(1) == 0)
    def _():
        st_ref[...] = jnp.zeros_like(st_ref)

    row = lax.broadcasted_iota(jnp.int32, (c, c), 0)
    col = lax.broadcasted_iota(jnp.int32, (c, c), 1)
    tril = (row >= col).astype(BF16)
    diag_mask = (row >= col) & ((row // HGRN_DIAG) == (col // HGRN_DIAG))
    level_masks = []
    blk = 2 * HGRN_DIAG
    while blk <= c:
        level_masks.append((blk, None if blk == c else (row // blk) == (col // blk)))
        blk *= 2

    def nt_dot(x, y):
        return lax.dot_general(x, y, nt, preferred_element_type=F32)

    def chunk(ci, carry):
        r0 = pl.multiple_of(ci * c, c)
        for g0 in range(0, n_heads, HGRN_GROUP):
            heads = list(range(g0, g0 + HGRN_GROUP))
            sls = [slice(h * d, (h + 1) * d) for h in heads]
            f = [0.5 * (1.0 + lb_ref[:, sl]) + (0.5 * (1.0 - lb_ref[:, sl])) * jnp.tanh(0.5 * zf_ref[pl.ds(r0, c), sl])
                 for sl in sls]
            logf = [jnp.log2(jnp.clip(x, F_MIN, 1.0)) for x in f]
            pieces = [jnp.concatenate(_split_bf16(x), axis=1) for x in logf]
            cs = [jnp.dot(tril, x, preferred_element_type=F32) for x in pieces]
            a = [x[:, :d] + x[:, d:] for x in cs]
            k = [1.0 - x for x in f]
            q = [_silu(zq_ref[pl.ds(r0, c), sl]) for sl in sls]
            iv = [zi_ref[pl.ds(r0, c), sl].astype(BF16) for sl in sls]

            dd = []
            for x in a:
                a3 = x.reshape(c // HGRN_DIAG, HGRN_DIAG, d)
                dd.append((a3 - a3[:, HGRN_DIAG_REF:HGRN_DIAG_REF + 1, :]).reshape(c, d))
            sd = [nt_dot((qx * jnp.exp2(x)).astype(BF16), (kx * jnp.exp2(-x)).astype(BF16))
                  for qx, kx, x in zip(q, k, dd)]

            scores = None
            for blk, same_block in reversed(level_masks):
                terms = [_hgrn_level_terms(ax, qx, kx, blk) for ax, qx, kx in zip(a, q, k)]
                prods = [nt_dot(qh, kh) for qh, kh in terms]
                scores = prods if scores is None else [jnp.where(same_block, p, s) for s, p in zip(scores, prods)]
            scores = [jnp.where(diag_mask, x, s) for x, s in zip(sd, scores)]

            st = [st_ref[h] for h in heads]
            qe = [(qx * jnp.exp2(ax)).astype(BF16) for qx, ax in zip(q, a)]
            o = [jnp.dot(s.astype(BF16), ivx, preferred_element_type=F32) + nt_dot(qx, sx.astype(BF16))
                 for s, ivx, qx, sx in zip(scores, iv, qe, st)]
            kd = [(kx * jnp.exp2(ax[c - 1:c, :] - ax)).astype(BF16) for kx, ax in zip(k, a)]
            for h, ax, sx, ivx, kx in zip(heads, a, st, iv, kd):
                st_ref[h] = jnp.exp2(ax[c - 1:c, :]) * sx + lax.dot_general(ivx, kx, tn, preferred_element_type=F32)
            for x, sl in zip(o, sls):
                y_ref[pl.ds(r0, c), sl] = (_rms(x, ow_ref[:, sl]) * _silu(zg_ref[pl.ds(r0, c), sl])).astype(BF16)
        return carry

    lax.fori_loop(0, n_chunks, chunk, 0)


def hgrn2_mixer(z, lb, out_w, batch, col0, rows=1024):
    t = z.shape[0]
    s = t // batch
    dh = lb.shape[0]
    n_heads = dh // HGRN_HEAD
    rows = min(rows, s)
    spb = s // rows
    kern = functools.partial(_hgrn_kernel, n_heads=n_heads, n_chunks=rows // HGRN_C)
    zspec = lambda j: pl.BlockSpec((rows, dh), lambda b, i, j=j: (b * spb + i, col0 + j))
    return pl.pallas_call(
        kern,
        grid=(batch, spb),
        in_specs=[zspec(0), zspec(1), zspec(2), zspec(3),
                  pl.BlockSpec((1, dh), lambda b, i: (0, 0)),
                  pl.BlockSpec((1, dh), lambda b, i: (0, 0))],
        out_specs=pl.BlockSpec((rows, dh), lambda b, i: (b * spb + i, 0)),
        out_shape=jax.ShapeDtypeStruct((t, dh), BF16),
        scratch_shapes=[pltpu.VMEM((n_heads, HGRN_HEAD, HGRN_HEAD), F32)],
        compiler_params=_cparams(("parallel", "arbitrary")),
        name="hgrn2_mixer",
    )(z, z, z, z, lb.reshape(1, dh), out_w.reshape(1, dh))


ROUTE_E0, ROUTE_E1, ROUTE_W0, ROUTE_W1, ROUTE_R0, ROUTE_R1 = range(6)


def _split_bf16(v):
    hi = v.astype(BF16)
    return hi, (v - hi.astype(F32)).astype(BF16)


def _route_rows(h, rw_pieces, carry_ref):
    bm = h.shape[0]
    h_hi, h_lo = _split_bf16(h)
    hh_hl = jnp.dot(h_hi, rw_pieces, preferred_element_type=F32)
    logits = hh_hl[:, :LANES] + hh_hl[:, LANES:] + jnp.dot(h_lo, rw_pieces[:, :LANES], preferred_element_type=F32)
    lane = lax.broadcasted_iota(jnp.int32, (bm, LANES), 1)
    neg = jnp.float32(-jnp.inf)
    logits = jnp.where(lane < N_EXPERTS, logits, neg)
    v0 = jnp.max(logits, axis=-1, keepdims=True)
    e0 = jnp.min(jnp.where(logits == v0, lane, LANES), axis=-1, keepdims=True)
    rest = jnp.where(lane == e0, neg, logits)
    v1 = jnp.max(rest, axis=-1, keepdims=True)
    e1 = jnp.min(jnp.where(rest == v1, lane, LANES), axis=-1, keepdims=True)
    ex = jnp.exp(v1 - v0)
    w0 = 1.0 / (1.0 + ex)
    w1 = ex / (1.0 + ex)

    onehot = ((lane == e0) | (lane == e1)).astype(F32)
    row = lax.broadcasted_iota(jnp.int32, (bm, bm), 0)
    col = lax.broadcasted_iota(jnp.int32, (bm, bm), 1)
    before = (row > col).astype(BF16)
    excl = jnp.dot(before, onehot.astype(BF16), preferred_element_type=F32) + carry_ref[...]
    r0 = jnp.sum(jnp.where(lane == e0, excl, 0.0), axis=-1, keepdims=True)
    r1 = jnp.sum(jnp.where(lane == e1, excl, 0.0), axis=-1, keepdims=True)
    carry_ref[...] += jnp.sum(onehot, axis=0, keepdims=True)

    out = jnp.zeros((bm, LANES), F32)
    for idx, val in ((ROUTE_E0, e0.astype(F32)), (ROUTE_E1, e1.astype(F32)), (ROUTE_W0, w0), (ROUTE_W1, w1),
                     (ROUTE_R0, r0), (ROUTE_R1, r1)):
        out = jnp.where(lane == idx, val, out)
    return out


def _outproj_kernel(*refs, da, moe):
    if moe:
        (x_ref, ya_ref, yb_ref, wo_ref, nw_ref, rw_ref, xo_ref, h_ref, route_ref, cnt_ref,
         wo_scr, rw_scr, carry_scr) = refs
    else:
        x_ref, ya_ref, yb_ref, wo_ref, nw_ref, xo_ref, h_ref, wo_scr = refs

    @pl.when(pl.program_id(0) == 0)
    def _():
        wo_scr[...] = wo_ref[...].astype(BF16)
        if moe:
            rw_hi, rw_lo = _split_bf16(rw_ref[...])
            rw_scr[:, :LANES] = rw_hi
            rw_scr[:, LANES:] = rw_lo
            carry_scr[...] = jnp.zeros_like(carry_scr)

    acc = x_ref[...] + jnp.dot(ya_ref[...], wo_scr[:da, :], preferred_element_type=F32) \
        + jnp.dot(yb_ref[...], wo_scr[da:, :], preferred_element_type=F32)
    xo_ref[...] = acc
    h = _rms(acc, nw_ref[...])
    h_ref[...] = h.astype(h_ref.dtype)
    if moe:
        route_ref[...] = _route_rows(h, rw_scr[...], carry_scr)
        cnt_ref[...] = carry_scr[...]


def out_proj(x, y_a, y_b, w_o, layer, norm_w, router_w=None, bm=None):
    t, d = x.shape
    da, db = y_a.shape[1], y_b.shape[1]
    moe = router_w is not None
    bm = min(bm or (256 if moe else 512), t)
    row_spec = lambda w: pl.BlockSpec((bm, w), lambda i: (i, 0))
    const = lambda shape: pl.BlockSpec(shape, lambda i: (0,) * len(shape), pipeline_mode=pl.Buffered(1))
    w_spec = pl.BlockSpec((None, da + db, d), lambda i: (layer, 0, 0), pipeline_mode=pl.Buffered(1))
    in_specs = [row_spec(d), row_spec(da), row_spec(db), w_spec, const((1, d))]
    out_specs = [row_spec(d), row_spec(d)]
    out_shape = [jax.ShapeDtypeStruct((t, d), F32), jax.ShapeDtypeStruct((t, d), F32 if moe else BF16)]
    scratch = [pltpu.VMEM((da + db, d), BF16)]
    args = [x, y_a, y_b, w_o, norm_w.reshape(1, d)]
    if moe:
        in_specs.append(const((d, LANES)))
        args.append(jnp.zeros((d, LANES), F32).at[:, :N_EXPERTS].set(router_w))
        out_specs += [row_spec(LANES), pl.BlockSpec((1, LANES), lambda i: (0, 0))]
        out_shape += [jax.ShapeDtypeStruct((t, LANES), F32), jax.ShapeDtypeStruct((1, LANES), F32)]
        scratch += [pltpu.VMEM((d, 2 * LANES), BF16), pltpu.VMEM((1, LANES), F32)]
    return pl.pallas_call(
        functools.partial(_outproj_kernel, da=da, moe=moe),
        grid=(t // bm,),
        in_specs=in_specs,
        out_specs=out_specs,
        out_shape=out_shape,
        scratch_shapes=scratch,
        compiler_params=_cparams(("arbitrary",)),
        name="out_proj_route" if moe else "out_proj",
    )(*args)


def _ffn_kernel(x_hbm, h_ref, wg_ref, wu_ref, wd_ref, *rest, n_pass):
    n_cast = (len(rest) - 2) // 2
    o_ref, sem = rest[n_cast], rest[-1]
    i, j, k = pl.program_id(0), pl.program_id(1), pl.program_id(2)
    bm = o_ref.shape[0]
    sub = bm // n_pass

    def x_copy(p):
        src = x_hbm.at[pl.ds(pl.multiple_of(i * bm + p * sub, sub), sub)]
        return pltpu.make_async_copy(src, o_ref.at[pl.ds(p * sub, sub)], sem.at[p])

    @pl.when((j == 0) & (k == 0))
    def _():
        for p in range(n_pass):
            x_copy(p).start()

    for p in range(n_pass):
        @pl.when((j == 0) & (k == p))
        def _():
            x_copy(p).wait()

    _narrow_slabs(rest[:n_cast], rest[n_cast + 1:-1])

    rows = pl.ds(pl.multiple_of(k * sub, sub), sub)
    hb = h_ref[rows, :]
    g = jnp.dot(hb, wg_ref[...], preferred_element_type=F32)
    u = jnp.dot(hb, wu_ref[...], preferred_element_type=F32)
    act = (_silu(g) * u).astype(BF16)
    o_ref[rows, :] += jnp.dot(act, wd_ref[...], preferred_element_type=F32)


def dense_ffn(x, h, wg, wu, wd, cast=(), bm=1024, bf=512, n_pass=2):
    t, d = x.shape
    f = wg.shape[1]
    bm = min(bm, t)
    ni, nj = t // bm, f // bf
    cast_arrays, cast_in, cast_out, cast_shapes = _cast_block_specs(cast, ni, nj, n_pass)
    outs = pl.pallas_call(
        functools.partial(_ffn_kernel, n_pass=n_pass),
        grid=(ni, nj, n_pass),
        in_specs=[
            pl.BlockSpec(memory_space=pl.ANY),
            pl.BlockSpec((bm, d), lambda i, j, k: (i, 0)),
            pl.BlockSpec((d, bf), lambda i, j, k: (0, j)),
            pl.BlockSpec((d, bf), lambda i, j, k: (0, j)),
            pl.BlockSpec((bf, d), lambda i, j, k: (j, 0)),
        ] + cast_in,
        out_specs=[pl.BlockSpec((bm, d), lambda i, j, k: (i, 0))] + cast_out,
        out_shape=[jax.ShapeDtypeStruct((t, d), F32)] + cast_shapes,
        scratch_shapes=[pltpu.SemaphoreType.DMA((n_pass,))],
        compiler_params=_cparams(("parallel", "arbitrary", "arbitrary")),
        name="dense_ffn",
    )(x, h, wg, wu, wd, *cast_arrays)
    return outs[0], tuple(outs[1:])


def _row_copy(src_ref, dst_ref, sem, src_row, dst_row):
    return pltpu.make_async_copy(src_ref.at[pl.ds(src_row, 1)], dst_ref.at[pl.ds(dst_row, 1)], sem)


ROW_DMA_UNROLL = 8
GMM_TILE = 1024
GMM_SUBTILES = 2


def _dispatch_kernel(zlo_ref, zhi_ref, pos_ref, h_ref, xs_ref, zero_scr, sem, zsem):
    bt = h_ref.shape[0]

    @pl.when(pl.program_id(0) == 0)
    def _():
        zero_scr[...] = jnp.zeros_like(zero_scr)
        zr = zero_scr.shape[0]

        def zero_rows(e, begin):
            lo, hi = zlo_ref[e], zhi_ref[e]
            mid = jnp.minimum(hi, (lo + zr - 1) // zr * zr)

            def one(r, carry):
                cp = _row_copy(zero_scr, xs_ref, zsem, 0, r)
                cp.start() if begin else cp.wait()
                return carry

            def slab(b, carry):
                cp = pltpu.make_async_copy(zero_scr, xs_ref.at[pl.ds(pl.multiple_of(b * zr, zr), zr)], zsem)
                cp.start() if begin else cp.wait()
                return carry

            lax.fori_loop(lo, mid, one, 0)
            lax.fori_loop(mid // zr, hi // zr, slab, 0)

        for e in range(N_EXPERTS):
            zero_rows(e, True)
        for e in range(N_EXPERTS):
            zero_rows(e, False)

    def start(r, carry):
        for kk in range(TOP_K):
            _row_copy(h_ref, xs_ref, sem, r, pos_ref[0, TOP_K * r + kk]).start(priority=kk % 2)
        return carry

    lax.fori_loop(0, bt, start, 0, unroll=ROW_DMA_UNROLL)
    for kk in range(TOP_K):
        pltpu.make_async_copy(h_ref, xs_ref.at[pl.ds(0, bt)], sem).wait()


def moe_dispatch(h, pos, zero_lo, zero_hi, n_rows, bt=512):
    t, d = h.shape
    bt = min(bt, t)
    pos2 = pos.reshape(t // bt, 1, TOP_K * bt)
    return pl.pallas_call(
        _dispatch_kernel,
        grid_spec=pltpu.PrefetchScalarGridSpec(
            num_scalar_prefetch=2,
            grid=(t // bt,),
            in_specs=[
                pl.BlockSpec((None, 1, TOP_K * bt), lambda i, zl, zh: (i, 0, 0), memory_space=pltpu.SMEM),
                pl.BlockSpec((bt, d), lambda i, zl, zh: (i, 0)),
            ],
            out_specs=pl.BlockSpec(memory_space=pl.ANY),
            scratch_shapes=[pltpu.VMEM((8, d), F32), pltpu.SemaphoreType.DMA(()), pltpu.SemaphoreType.DMA(())],
        ),
        out_shape=jax.ShapeDtypeStruct((n_rows, d), F32),
        compiler_params=_cparams(("arbitrary",)),
        name="moe_dispatch",
    )(zero_lo, zero_hi, pos2, h)


def _gmm_kernel(te_ref, tr_ref, nv_ref, xs_ref, wg_ref, wu_ref, wd_ref, ys_ref, xb_scr):
    i, j = pl.program_id(0), pl.program_id(1)
    bm = xs_ref.shape[0]
    rows = tr_ref[i]
    sub = bm // GMM_SUBTILES

    @pl.when(j == 0)
    def _():
        xb_scr[...] = xs_ref[...].astype(BF16)
        ys_ref[...] = jnp.zeros_like(ys_ref)

    def ffn(r0, n):
        xb = xb_scr[r0:r0 + n, :]
        g = jnp.dot(xb, wg_ref[...], preferred_element_type=F32)
        u = jnp.dot(xb, wu_ref[...], preferred_element_type=F32)
        act = (_silu(g) * u).astype(BF16)
        ys_ref[r0:r0 + n, :] += jnp.dot(act, wd_ref[...], preferred_element_type=F32)

    full = rows == bm

    @pl.when(full)
    def _():
        for s in range(GMM_SUBTILES):
            ffn(s * sub, sub)

    for s in range(GMM_SUBTILES):
        left = rows - s * sub

        @pl.when(jnp.logical_not(full) & (left > sub // 2))
        def _():
            ffn(s * sub, sub)

        @pl.when((left > 0) & (left <= sub // 2))
        def _():
            ffn(s * sub, sub // 2)


def moe_grouped_ffn(xs, tile_expert, tile_rows, n_valid, wg, wu, wd, bm, bf=512):
    p, d = xs.shape
    f = wg.shape[2]
    nf = f // bf

    def row_map(i, j, te, tr, nv):
        return (jnp.minimum(i, nv[0] - 1), 0)

    def fcol(i, j, nv):
        return jnp.where(i < nv[0], j, nf - 1)

    return pl.pallas_call(
        _gmm_kernel,
        grid_spec=pltpu.PrefetchScalarGridSpec(
            num_scalar_prefetch=3,
            grid=(p // bm, nf),
            in_specs=[
                pl.BlockSpec((bm, d), row_map),
                pl.BlockSpec((None, d, bf), lambda i, j, te, tr, nv: (te[i], 0, fcol(i, j, nv))),
                pl.BlockSpec((None, d, bf), lambda i, j, te, tr, nv: (te[i], 0, fcol(i, j, nv))),
                pl.BlockSpec((None, bf, d), lambda i, j, te, tr, nv: (te[i], fcol(i, j, nv), 0)),
            ],
            out_specs=pl.BlockSpec((bm, d), lambda i, j, te, tr, nv: (i, 0)),
            scratch_shapes=[pltpu.VMEM((bm, d), BF16)],
        ),
        out_shape=jax.ShapeDtypeStruct((p, d), F32),
        compiler_params=_cparams(("arbitrary", "arbitrary")),
        name="moe_grouped_ffn",
    )(tile_expert, tile_rows, n_valid, xs, wg, wu, wd)


def _combine_kernel(pos_ref, pos_next_ref, x_ref, route_ref, nw_ref, ys_ref, o_ref, y_scr, sem, *, final_norm):
    bt = x_ref.shape[0]
    i = pl.program_id(0)
    slot = i % 2

    def gather(p_ref, s, unroll):
        def start(r, carry):
            for kk in range(TOP_K):
                _row_copy(ys_ref, y_scr.at[s, kk], sem.at[s], p_ref[0, TOP_K * r + kk], r).start(priority=kk % 2)
            return carry
        lax.fori_loop(0, bt, start, 0, unroll=unroll)

    def wait(s):
        for kk in range(TOP_K):
            pltpu.make_async_copy(ys_ref.at[pl.ds(0, bt)], y_scr.at[s, kk], sem.at[s]).wait()

    @pl.when(i == 0)
    def _():
        gather(pos_ref, 0, ROW_DMA_UNROLL)

    wait(slot)
    gather(pos_next_ref, 1 - slot, True)
    w0 = route_ref[:, ROUTE_W0:ROUTE_W0 + 1]
    w1 = route_ref[:, ROUTE_W1:ROUTE_W1 + 1]
    out = x_ref[...] + (w0 * y_scr[slot, 0] + w1 * y_scr[slot, 1])
    if final_norm:
        out = _rms(out, nw_ref[...])
    o_ref[...] = out

    @pl.when(i + 1 == pl.num_programs(0))
    def _():
        wait(1 - slot)


def moe_combine(x, route, pos, ys, norm_w, final_norm, bt=256):
    t, d = x.shape
    bt = min(bt, t)
    n = t // bt
    pos2 = pos.reshape(n, 1, TOP_K * bt)
    pos_spec = lambda nxt: pl.BlockSpec((None, 1, TOP_K * bt), lambda i: (jnp.minimum(i + nxt, n - 1), 0, 0),
                                        memory_space=pltpu.SMEM)
    return pl.pallas_call(
        functools.partial(_combine_kernel, final_norm=final_norm),
        grid=(n,),
        in_specs=[
            pos_spec(0),
            pos_spec(1),
            pl.BlockSpec((bt, d), lambda i: (i, 0)),
            pl.BlockSpec((bt, LANES), lambda i: (i, 0)),
            pl.BlockSpec((1, d), lambda i: (0, 0)),
            pl.BlockSpec(memory_space=pl.ANY),
        ],
        out_specs=pl.BlockSpec((bt, d), lambda i: (i, 0)),
        out_shape=jax.ShapeDtypeStruct((t, d), F32),
        scratch_shapes=[pltpu.VMEM((2, TOP_K, bt, d), F32), pltpu.SemaphoreType.DMA((2,))],
        compiler_params=_cparams(("arbitrary",)),
        name="moe_combine",
    )(pos2, pos2, x, route, norm_w.reshape(1, d), ys)


def moe_ffn(x, h, route, counts, wg, wu, wd, norm_w, final_norm, bm=GMM_TILE):
    t, d = x.shape

    cnt = counts[0, :N_EXPERTS].astype(jnp.int32)
    tiles = (cnt + bm - 1) // bm
    tile_end = jnp.cumsum(tiles)
    offset = (tile_end - tiles) * bm
    n_tiles = (TOP_K * t) // bm + N_EXPERTS
    n_valid = tile_end[-1:].astype(jnp.int32)
    tile_id = jnp.minimum(jnp.arange(n_tiles, dtype=jnp.int32), n_valid[0] - 1)
    tile_expert = jnp.sum((tile_end[None, :] <= tile_id[:, None]).astype(jnp.int32), axis=1)
    tile_in_group = tile_id - (tile_end - tiles)[tile_expert]
    tile_rows = jnp.clip(cnt[tile_expert] - tile_in_group * bm, 0, bm)
    tile_rows = jnp.where(jnp.arange(n_tiles) < n_valid[0], tile_rows, 0).astype(jnp.int32)
    experts = route[:, ROUTE_E0:ROUTE_E1 + 1].astype(jnp.int32)
    ranks = route[:, ROUTE_R0:ROUTE_R1 + 1].astype(jnp.int32)
    pos = offset[experts] + ranks
    zero_lo = offset + cnt
    zero_hi = jnp.concatenate([offset[1:], jnp.full((1,), n_tiles * bm, jnp.int32)])

    xs = moe_dispatch(h, pos, zero_lo, zero_hi, n_tiles * bm)
    ys = moe_grouped_ffn(xs, tile_expert, tile_rows, n_valid, wg, wu, wd, bm)
    return moe_combine(x, route, pos, ys, norm_w, final_norm)


def _norm_kernel(x_ref, nw_ref, o_ref):
    o_ref[...] = _rms(x_ref[...], nw_ref[...])


def final_norm(x, norm_w, bm=512):
    t, d = x.shape
    bm = min(bm, t)
    return pl.pallas_call(
        _norm_kernel,
        grid=(t // bm,),
        in_specs=[pl.BlockSpec((bm, d), lambda i: (i, 0)), pl.BlockSpec((1, d), lambda i: (0, 0))],
        out_specs=pl.BlockSpec((bm, d), lambda i: (i, 0)),
        out_shape=jax.ShapeDtypeStruct((t, d), F32),
        compiler_params=_cparams(("parallel",)),
        name="final_norm",
    )(x, norm_w.reshape(1, d))


def kernel(x, mix_norm_w, w_in, gmlp_w_s, gmlp_b_s, gmlp_ln_w, gmlp_ln_b, gmlp_out_w, hgrn_lb_logits, hgrn_out_w, w_o, ffn_norm_w, dense_w_gate, dense_w_up, dense_w_down, router_w, expert_w_gate, expert_w_up, expert_w_down, final_norm_w):
    batch, seq, d = x.shape
    depth = w_in.shape[0]
    d_gmlp = gmlp_ln_w.shape[1]
    d_hgrn = hgrn_out_w.shape[1]
    assert d_gmlp == d_hgrn and seq % HGRN_C == 0 and seq % GMLP_CHUNK == 0

    lbs = jax.nn.softmax(hgrn_lb_logits.astype(F32), axis=0)
    lbs = jnp.cumsum(lbs, axis=0) - lbs[0:1]

    xt = x.reshape(batch * seq, d)
    w_in_bf16 = w_in[0].astype(BF16)
    for l in range(depth):
        last = l == depth - 1
        j = l // 2
        soon = [] if last else [(w_in.reshape(depth * d, -1), (l + 1) * d, d)]
        if l % 2 == 0:
            soon += [dense_w_gate[j], dense_w_up[j], dense_w_down[j]]
        z, narrowed = in_proj(xt, mix_norm_w[l], w_in_bf16, cast=tuple(soon))
        narrowed = list(narrowed)
        if not last:
            w_in_bf16 = narrowed.pop(0)
        y_a = gmlp_mixer(z, gmlp_w_s[l], gmlp_b_s[l], gmlp_ln_w[l], gmlp_ln_b[l], gmlp_out_w[l])
        y_b = hgrn2_mixer(z, lbs[l], hgrn_out_w[l], batch, col0=2 * d_gmlp // d_hgrn)
        if l % 2 == 0:
            xt, h = out_proj(xt, y_a, y_b, w_o, l, ffn_norm_w[l])
            experts = () if last else tuple(w[j].reshape(-1, w.shape[-1])
                                            for w in (expert_w_gate, expert_w_up, expert_w_down))
            xt, experts_bf16 = dense_ffn(xt, h, *narrowed, cast=experts)
            if last:
                xt = final_norm(xt, final_norm_w)
        else:
            xt, h, route, counts = out_proj(xt, y_a, y_b, w_o, l, ffn_norm_w[l], router_w=router_w[j])
            wg, wu, wd = (w.reshape(s.shape[1:]) for w, s in
                          zip(experts_bf16, (expert_w_gate, expert_w_up, expert_w_down)))
            xt = moe_ffn(xt, h, route, counts, wg, wu, wd, final_norm_w, final_norm=last)
    return xt.reshape(batch, seq, d)
```

```python
import functools

import jax
import jax.numpy as jnp
from jax import lax
from jax.experimental import pallas as pl
from jax.experimental.pallas import tpu as pltpu

F32 = jnp.float32
BF16 = jnp.bfloat16

GMLP_HEAD = 128
GMLP_CHUNK = 128
HGRN_HEAD = 128
N_EXPERTS = 8
TOP_K = 2
EPS = 1e-6
F_MIN = 1e-6

LANES = 128
GMLP_GROUP = 8
HGRN_C = 128
HGRN_GROUP = 8
HGRN_DIAG = 8
HGRN_DIAG_REF = 3
VMEM_LIMIT = 56 * 1024 * 1024
VMEM_LIMIT_IN_PROJ = 60 * 1024 * 1024


def _cparams(sem, vmem=VMEM_LIMIT):
    return pltpu.CompilerParams(dimension_semantics=sem, vmem_limit_bytes=vmem)


def _silu(x):
    hx = 0.5 * x
    return hx + hx * jnp.tanh(hx)


GELU_C1 = 0.7978845608028654
GELU_C2 = GELU_C1 * 0.044715


def _gelu_tanh(x):
    hx = 0.5 * x
    return hx + hx * jnp.tanh(x * (GELU_C1 + GELU_C2 * (x * x)))


def _rms(x, w):
    return x * lax.rsqrt(jnp.mean(x * x, axis=-1, keepdims=True) + EPS) * w


BF16_SUBLANES = 16


def _cast_block_specs(items, ni, nj, nk=1):
    arrays, in_specs, out_specs, out_shapes = [], [], [], []
    for item in items:
        w, row0, r = item if isinstance(item, tuple) else (item, 0, item.shape[0])
        c = w.shape[1]
        if c % nj == 0 and (c // nj) % LANES == 0 and r % (ni * nk * BF16_SUBLANES) == 0:
            block = (r // (ni * nk), c // nj)
            index = lambda i, j, k=0, off=0: (i * nk + k + off, j)
        else:
            assert r % (ni * nj * nk * BF16_SUBLANES) == 0, (w.shape, r, ni, nj, nk)
            block = (r // (ni * nj * nk), c)
            index = lambda i, j, k=0, off=0: ((i * nj + j) * nk + k + off, 0)
        assert row0 % block[0] == 0
        arrays.append(w)
        in_specs.append(pl.BlockSpec(block, functools.partial(index, off=row0 // block[0])))
        out_specs.append(pl.BlockSpec(block, index))
        out_shapes.append(jax.ShapeDtypeStruct((r, c), BF16))
    return arrays, in_specs, out_specs, out_shapes


def _narrow_slabs(src_refs, dst_refs):
    for src_ref, dst_ref in zip(src_refs, dst_refs):
        dst_ref[...] = src_ref[...].astype(BF16)


def _inproj_kernel(x_ref, nw_ref, w_ref, *rest):
    n_cast = (len(rest) - 2) // 2
    z_ref, h_scr = rest[n_cast], rest[-1]

    @pl.when(pl.program_id(1) == 0)
    def _():
        h_scr[...] = _rms(x_ref[...], nw_ref[...]).astype(BF16)

    _narrow_slabs(rest[:n_cast], rest[n_cast + 1:-1])
    z_ref[...] = jnp.dot(h_scr[...], w_ref[...], preferred_element_type=F32)


def in_proj(x, norm_w, w, cast=(), bm=1024, bn=1536):
    t, d = x.shape
    n = w.shape[1]
    bm = min(bm, t)
    ni, nj = t // bm, n // bn
    cast_arrays, cast_in, cast_out, cast_shapes = _cast_block_specs(cast, ni, nj)
    outs = pl.pallas_call(
        _inproj_kernel,
        grid=(ni, nj),
        in_specs=[
            pl.BlockSpec((bm, d), lambda i, j: (i, 0)),
            pl.BlockSpec((1, d), lambda i, j: (0, 0)),
            pl.BlockSpec((d, bn), lambda i, j: (0, j)),
        ] + cast_in,
        out_specs=[pl.BlockSpec((bm, bn), lambda i, j: (i, j))] + cast_out,
        out_shape=[jax.ShapeDtypeStruct((t, n), F32)] + cast_shapes,
        scratch_shapes=[pltpu.VMEM((bm, d), BF16)],
        compiler_params=_cparams(("parallel", "arbitrary"), VMEM_LIMIT_IN_PROJ),
        name="in_proj",
    )(x, norm_w.reshape(1, d), w, *cast_arrays)
    return outs[0], tuple(outs[1:])


def _gmlp_kernel(zu_ref, zv_ref, ws_ref, bs_ref, lnw_ref, lnb_ref, ow_ref, y_ref, wc_scr, *, n_heads, n_chunks):
    c = GMLP_CHUNK
    row = lax.broadcasted_iota(jnp.int32, (c, c), 0)
    col = lax.broadcasted_iota(jnp.int32, (c, c), 1)
    causal = row >= col
    ones = jnp.ones((GMLP_HEAD, GMLP_HEAD), BF16)
    lane_mean = lambda x: jnp.dot(x.astype(BF16), ones, preferred_element_type=F32) * (1.0 / GMLP_HEAD)

    for h in range(n_heads):
        wc_scr[h] = jnp.where(causal, ws_ref[h], 0.0).astype(BF16)

    def chunk(ci, carry):
        r0 = pl.multiple_of(ci * c, c)
        for g0 in range(0, n_heads, GMLP_GROUP):
            heads = range(g0, g0 + GMLP_GROUP)
            sls = [slice(h * GMLP_HEAD, (h + 1) * GMLP_HEAD) for h in heads]
            v = [_gelu_tanh(zv_ref[pl.ds(r0, c), sl]) for sl in sls]
            mu = [lane_mean(x) for x in v]
            vc = [x - m for x, m in zip(v, mu)]
            var = [lane_mean(x * x) for x in vc]
            vn = [(x * lax.rsqrt(s + EPS) * lnw_ref[:, sl] + lnb_ref[:, sl]).astype(BF16)
                  for x, s, sl in zip(vc, var, sls)]
            sv = [jnp.dot(wc_scr[h], x, preferred_element_type=F32) + bs_ref[:, h:h + 1] for h, x in zip(heads, vn)]
            y = [_gelu_tanh(zu_ref[pl.ds(r0, c), sl]) * x for sl, x in zip(sls, sv)]
            ms = [lane_mean(x * x) for x in y]
            for x, s, sl in zip(y, ms, sls):
                y_ref[pl.ds(r0, c), sl] = (x * lax.rsqrt(s + EPS) * ow_ref[:, sl]).astype(BF16)
        return carry

    lax.fori_loop(0, n_chunks, chunk, 0)


def gmlp_mixer(z, w_s, b_s, ln_w, ln_b, out_w, rows=1024):
    t = z.shape[0]
    n_heads = w_s.shape[0]
    dg = n_heads * GMLP_HEAD
    rows = min(rows, t)
    kern = functools.partial(_gmlp_kernel, n_heads=n_heads, n_chunks=rows // GMLP_CHUNK)
    vec = lambda a: a.reshape(1, dg)
    return pl.pallas_call(
        kern,
        grid=(t // rows,),
        in_specs=[
            pl.BlockSpec((rows, dg), lambda i: (i, 0)),
            pl.BlockSpec((rows, dg), lambda i: (i, 1)),
            pl.BlockSpec((n_heads, GMLP_CHUNK, GMLP_CHUNK), lambda i: (0, 0, 0)),
            pl.BlockSpec((GMLP_CHUNK, n_heads), lambda i: (0, 0)),
            pl.BlockSpec((1, dg), lambda i: (0, 0)),
            pl.BlockSpec((1, dg), lambda i: (0, 0)),
            pl.BlockSpec((1, dg), lambda i: (0, 0)),
        ],
        out_specs=pl.BlockSpec((rows, dg), lambda i: (i, 0)),
        out_shape=jax.ShapeDtypeStruct((t, dg), BF16),
        scratch_shapes=[pltpu.VMEM((n_heads, GMLP_CHUNK, GMLP_CHUNK), BF16)],
        compiler_params=_cparams(("parallel",)),
        name="gmlp_mixer",
    )(z, z, w_s, b_s.T, vec(ln_w), vec(ln_b), vec(out_w))


def _hgrn_level_terms(a, q, k, blk):
    c, d = a.shape
    half = blk // 2
    nb = c // blk
    lower = lambda v: v.reshape(nb, blk, d)[:, :half, :]
    upper = lambda v: v.reshape(nb, blk, d)[:, half:, :]
    ref = a.reshape(nb, blk, d)[:, half - 1:half, :]
    qh = upper(q) * jnp.exp2(upper(a) - ref)
    kh = lower(k) * jnp.exp2(ref - lower(a))
    zeros = jnp.zeros_like(qh)
    qh = jnp.concatenate([zeros, qh], axis=1).reshape(c, d)
    kh = jnp.concatenate([kh, zeros], axis=1).reshape(c, d)
    return qh.astype(BF16), kh.astype(BF16)


def _hgrn_kernel(zq_ref, zf_ref, zi_ref, zg_ref, lb_ref, ow_ref, y_ref, st_ref, *, n_heads, n_chunks):
    c, d = HGRN_C, HGRN_HEAD
    nt = (((1,), (1,)), ((), ()))
    tn = (((0,), (0,)), ((), ()))

    @pl.when(pl.program_id(1) == 0)
    def _():
        st_ref[...] = jnp.zeros_like(st_ref)

    row = lax.broadcasted_iota(jnp.int32, (c, c), 0)
    col = lax.broadcasted_iota(jnp.int32, (c, c), 1)
    tril = (row >= col).astype(BF16)
    diag_mask = (row >= col) & ((row // HGRN_DIAG) == (col // HGRN_DIAG))
    level_masks = []
    blk = 2 * HGRN_DIAG
    while blk <= c:
        level_masks.append((blk, None if blk == c else (row // blk) == (col // blk)))
        blk *= 2

    def nt_dot(x, y):
        return lax.dot_general(x, y, nt, preferred_element_type=F32)

    def chunk(ci, carry):
        r0 = pl.multiple_of(ci * c, c)
        for g0 in range(0, n_heads, HGRN_GROUP):
            heads = list(range(g0, g0 + HGRN_GROUP))
            sls = [slice(h * d, (h + 1) * d) for h in heads]
            f = [0.5 * (1.0 + lb_ref[:, sl]) + (0.5 * (1.0 - lb_ref[:, sl])) * jnp.tanh(0.5 * zf_ref[pl.ds(r0, c), sl])
                 for sl in sls]
            logf = [jnp.log2(jnp.clip(x, F_MIN, 1.0)) for x in f]
            pieces = [jnp.concatenate(_split_bf16(x), axis=1) for x in logf]
            cs = [jnp.dot(tril, x, preferred_element_type=F32) for x in pieces]
            a = [x[:, :d] + x[:, d:] for x in cs]
            k = [1.0 - x for x in f]
            q = [_silu(zq_ref[pl.ds(r0, c), sl]) for sl in sls]
            iv = [zi_ref[pl.ds(r0, c), sl].astype(BF16) for sl in sls]

            dd = []
            for x in a:
                a3 = x.reshape(c // HGRN_DIAG, HGRN_DIAG, d)
                dd.append((a3 - a3[:, HGRN_DIAG_REF:HGRN_DIAG_REF + 1, :]).reshape(c, d))
            sd = [nt_dot((qx * jnp.exp2(x)).astype(BF16), (kx * jnp.exp2(-x)).astype(BF16))
                  for qx, kx, x in zip(q, k, dd)]

            scores = None
            for blk, same_block in reversed(level_masks):
                terms = [_hgrn_level_terms(ax, qx, kx, blk) for ax, qx, kx in zip(a, q, k)]
                prods = [nt_dot(qh, kh) for qh, kh in terms]
                scores = prods if scores is None else [jnp.where(same_block, p, s) for s, p in zip(scores, prods)]
            scores = [jnp.where(diag_mask, x, s) for x, s in zip(sd, scores)]

            st = [st_ref[h] for h in heads]
            qe = [(qx * jnp.exp2(ax)).astype(BF16) for qx, ax in zip(q, a)]
            o = [jnp.dot(s.astype(BF16), ivx, preferred_element_type=F32) + nt_dot(qx, sx.astype(BF16))
                 for s, ivx, qx, sx in zip(scores, iv, qe, st)]
            kd = [(kx * jnp.exp2(ax[c - 1:c, :] - ax)).astype(BF16) for kx, ax in zip(k, a)]
            for h, ax, sx, ivx, kx in zip(heads, a, st, iv, kd):
                st_ref[h] = jnp.exp2(ax[c - 1:c, :]) * sx + lax.dot_general(ivx, kx, tn, preferred_element_type=F32)
            for x, sl in zip(o, sls):
                y_ref[pl.ds(r0, c), sl] = (_rms(x, ow_ref[:, sl]) * _silu(zg_ref[pl.ds(r0, c), sl])).astype(BF16)
        return carry

    lax.fori_loop(0, n_chunks, chunk, 0)


def hgrn2_mixer(z, lb, out_w, batch, col0, rows=1024):
    t = z.shape[0]
    s = t // batch
    dh = lb.shape[0]
    n_heads = dh // HGRN_HEAD
    rows = min(rows, s)
    spb = s // rows
    kern = functools.partial(_hgrn_kernel, n_heads=n_heads, n_chunks=rows // HGRN_C)
    zspec = lambda j: pl.BlockSpec((rows, dh), lambda b, i, j=j: (b * spb + i, col0 + j))
    return pl.pallas_call(
        kern,
        grid=(batch, spb),
        in_specs=[zspec(0), zspec(1), zspec(2), zspec(3),
                  pl.BlockSpec((1, dh), lambda b, i: (0, 0)),
                  pl.BlockSpec((1, dh), lambda b, i: (0, 0))],
        out_specs=pl.BlockSpec((rows, dh), lambda b, i: (b * spb + i, 0)),
        out_shape=jax.ShapeDtypeStruct((t, dh), BF16),
        scratch_shapes=[pltpu.VMEM((n_heads, HGRN_HEAD, HGRN_HEAD), F32)],
        compiler_params=_cparams(("parallel", "arbitrary")),
        name="hgrn2_mixer",
    )(z, z, z, z, lb.reshape(1, dh), out_w.reshape(1, dh))


ROUTE_E0, ROUTE_E1, ROUTE_W0, ROUTE_W1, ROUTE_R0, ROUTE_R1 = range(6)
ROUTE_FIELDS = 8


def _split_bf16(v):
    hi = v.astype(BF16)
    return hi, (v - hi.astype(F32)).astype(BF16)


def _route_rows(h, rw_pieces, carry_ref):
    bm = h.shape[0]
    h_hi, h_lo = _split_bf16(h)
    hh_hl = jnp.dot(h_hi, rw_pieces, preferred_element_type=F32)
    logits = hh_hl[:, :LANES] + hh_hl[:, LANES:] + jnp.dot(h_lo, rw_pieces[:, :LANES], preferred_element_type=F32)
    lane = lax.broadcasted_iota(jnp.int32, (bm, LANES), 1)
    neg = jnp.float32(-jnp.inf)
    logits = jnp.where(lane < N_EXPERTS, logits, neg)
    v0 = jnp.max(logits, axis=-1, keepdims=True)
    e0 = jnp.min(jnp.where(logits == v0, lane, LANES), axis=-1, keepdims=True)
    rest = jnp.where(lane == e0, neg, logits)
    v1 = jnp.max(rest, axis=-1, keepdims=True)
    e1 = jnp.min(jnp.where(rest == v1, lane, LANES), axis=-1, keepdims=True)
    ex = jnp.exp(v1 - v0)
    w0 = 1.0 / (1.0 + ex)
    w1 = ex / (1.0 + ex)

    onehot = ((lane == e0) | (lane == e1)).astype(F32)
    row = lax.broadcasted_iota(jnp.int32, (bm, bm), 0)
    col = lax.broadcasted_iota(jnp.int32, (bm, bm), 1)
    before = (row > col).astype(BF16)
    excl = jnp.dot(before, onehot.astype(BF16), preferred_element_type=F32) + carry_ref[...]
    r0 = jnp.sum(jnp.where(lane == e0, excl, 0.0), axis=-1, keepdims=True)
    r1 = jnp.sum(jnp.where(lane == e1, excl, 0.0), axis=-1, keepdims=True)
    carry_ref[...] += jnp.sum(onehot, axis=0, keepdims=True)

    out = jnp.zeros((bm, LANES), F32)
    for idx, val in ((ROUTE_E0, e0.astype(F32)), (ROUTE_E1, e1.astype(F32)), (ROUTE_W0, w0), (ROUTE_W1, w1),
                     (ROUTE_R0, r0), (ROUTE_R1, r1)):
        out = jnp.where(lane == idx, val, out)
    return out


def _outproj_kernel(*refs, da, moe):
    if moe:
        (x_ref, ya_ref, yb_ref, wo_ref, nw_ref, rw_ref, xo_ref, h_ref, route_ref, route_t_ref, cnt_ref,
         wo_scr, rw_scr, carry_scr) = refs
    else:
        x_ref, ya_ref, yb_ref, wo_ref, nw_ref, xo_ref, h_ref, wo_scr = refs

    @pl.when(pl.program_id(0) == 0)
    def _():
        wo_scr[...] = wo_ref[...].astype(BF16)
        if moe:
            rw_hi, rw_lo = _split_bf16(rw_ref[...])
            rw_scr[:, :LANES] = rw_hi
            rw_scr[:, LANES:] = rw_lo
            carry_scr[...] = jnp.zeros_like(carry_scr)

    acc = x_ref[...] + jnp.dot(ya_ref[...], wo_scr[:da, :], preferred_element_type=F32) \
        + jnp.dot(yb_ref[...], wo_scr[da:, :], preferred_element_type=F32)
    xo_ref[...] = acc
    h = _rms(acc, nw_ref[...])
    h_ref[...] = h.astype(h_ref.dtype)
    if moe:
        route = _route_rows(h, rw_scr[...], carry_scr)
        route_ref[...] = route
        route_t_ref[...] = route.T[:route_t_ref.shape[0], :]
        cnt_ref[...] = carry_scr[...]


def out_proj(x, y_a, y_b, w_o, layer, norm_w, router_w=None, bm=None):
    t, d = x.shape
    da, db = y_a.shape[1], y_b.shape[1]
    moe = router_w is not None
    bm = min(bm or (256 if moe else 512), t)
    row_spec = lambda w: pl.BlockSpec((bm, w), lambda i: (i, 0))
    const = lambda shape: pl.BlockSpec(shape, lambda i: (0,) * len(shape), pipeline_mode=pl.Buffered(1))
    w_spec = pl.BlockSpec((None, da + db, d), lambda i: (layer, 0, 0), pipeline_mode=pl.Buffered(1))
    in_specs = [row_spec(d), row_spec(da), row_spec(db), w_spec, const((1, d))]
    out_specs = [row_spec(d), row_spec(d)]
    out_shape = [jax.ShapeDtypeStruct((t, d), F32), jax.ShapeDtypeStruct((t, d), F32 if moe else BF16)]
    scratch = [pltpu.VMEM((da + db, d), BF16)]
    args = [x, y_a, y_b, w_o, norm_w.reshape(1, d)]
    if moe:
        in_specs.append(const((d, LANES)))
        args.append(jnp.zeros((d, LANES), F32).at[:, :N_EXPERTS].set(router_w))
        out_specs += [row_spec(LANES), pl.BlockSpec((ROUTE_FIELDS, bm), lambda i: (0, i)),
                      pl.BlockSpec((1, LANES), lambda i: (0, 0))]
        out_shape += [jax.ShapeDtypeStruct((t, LANES), F32), jax.ShapeDtypeStruct((ROUTE_FIELDS, t), F32),
                      jax.ShapeDtypeStruct((1, LANES), F32)]
        scratch += [pltpu.VMEM((d, 2 * LANES), BF16), pltpu.VMEM((1, LANES), F32)]
    return pl.pallas_call(
        functools.partial(_outproj_kernel, da=da, moe=moe),
        grid=(t // bm,),
        in_specs=in_specs,
        out_specs=out_specs,
        out_shape=out_shape,
        scratch_shapes=scratch,
        compiler_params=_cparams(("arbitrary",)),
        name="out_proj_route" if moe else "out_proj",
    )(*args)


def _ffn_kernel(x_hbm, h_ref, wg_ref, wu_ref, wd_ref, *rest, n_pass):
    n_cast = (len(rest) - 2) // 2
    o_ref, sem = rest[n_cast], rest[-1]
    i, j, k = pl.program_id(0), pl.program_id(1), pl.program_id(2)
    bm = o_ref.shape[0]
    sub = bm // n_pass

    def x_copy(p):
        src = x_hbm.at[pl.ds(pl.multiple_of(i * bm + p * sub, sub), sub)]
        return pltpu.make_async_copy(src, o_ref.at[pl.ds(p * sub, sub)], sem.at[p])

    @pl.when((j == 0) & (k == 0))
    def _():
        for p in range(n_pass):
            x_copy(p).start()

    for p in range(n_pass):
        @pl.when((j == 0) & (k == p))
        def _():
            x_copy(p).wait()

    _narrow_slabs(rest[:n_cast], rest[n_cast + 1:-1])

    rows = pl.ds(pl.multiple_of(k * sub, sub), sub)
    hb = h_ref[rows, :]
    g = jnp.dot(hb, wg_ref[...], preferred_element_type=F32)
    u = jnp.dot(hb, wu_ref[...], preferred_element_type=F32)
    act = (_silu(g) * u).astype(BF16)
    o_ref[rows, :] += jnp.dot(act, wd_ref[...], preferred_element_type=F32)


def dense_ffn(x, h, wg, wu, wd, cast=(), bm=1024, bf=512, n_pass=2):
    t, d = x.shape
    f = wg.shape[1]
    bm = min(bm, t)
    ni, nj = t // bm, f // bf
    cast_arrays, cast_in, cast_out, cast_shapes = _cast_block_specs(cast, ni, nj, n_pass)
    outs = pl.pallas_call(
        functools.partial(_ffn_kernel, n_pass=n_pass),
        grid=(ni, nj, n_pass),
        in_specs=[
            pl.BlockSpec(memory_space=pl.ANY),
            pl.BlockSpec((bm, d), lambda i, j, k: (i, 0)),
            pl.BlockSpec((d, bf), lambda i, j, k: (0, j)),
            pl.BlockSpec((d, bf), lambda i, j, k: (0, j)),
            pl.BlockSpec((bf, d), lambda i, j, k: (j, 0)),
        ] + cast_in,
        out_specs=[pl.BlockSpec((bm, d), lambda i, j, k: (i, 0))] + cast_out,
        out_shape=[jax.ShapeDtypeStruct((t, d), F32)] + cast_shapes,
        scratch_shapes=[pltpu.SemaphoreType.DMA((n_pass,))],
        compiler_params=_cparams(("parallel", "arbitrary", "arbitrary")),
        name="dense_ffn",
    )(x, h, wg, wu, wd, *cast_arrays)
    return outs[0], tuple(outs[1:])


def _row_copy(src_ref, dst_ref, sem, src_row, dst_row):
    return pltpu.make_async_copy(src_ref.at[pl.ds(src_row, 1)], dst_ref.at[pl.ds(dst_row, 1)], sem)


ROW_DMA_UNROLL = 8
GMM_TILE = 1024
GMM_SUBTILES = 2


def _dispatch_kernel(zlo_ref, zhi_ref, pos_ref, h_ref, xs_ref, zero_scr, sem, zsem):
    bt = h_ref.shape[0]

    @pl.when(pl.program_id(0) == 0)
    def _():
        zero_scr[...] = jnp.zeros_like(zero_scr)
        zr = zero_scr.shape[0]

        def zero_rows(e, begin):
            lo, hi = zlo_ref[e], zhi_ref[e]
            mid = jnp.minimum(hi, (lo + zr - 1) // zr * zr)

            def one(r, carry):
                cp = _row_copy(zero_scr, xs_ref, zsem, 0, r)
                cp.start() if begin else cp.wait()
                return carry

            def slab(b, carry):
                cp = pltpu.make_async_copy(zero_scr, xs_ref.at[pl.ds(pl.multiple_of(b * zr, zr), zr)], zsem)
                cp.start() if begin else cp.wait()
                return carry

            lax.fori_loop(lo, mid, one, 0)
            lax.fori_loop(mid // zr, hi // zr, slab, 0)

        for e in range(N_EXPERTS):
            zero_rows(e, True)
        for e in range(N_EXPERTS):
            zero_rows(e, False)

    def start(r, carry):
        for kk in range(TOP_K):
            _row_copy(h_ref, xs_ref, sem, r, pos_ref[0, TOP_K * r + kk]).start(priority=kk % 2)
        return carry

    lax.fori_loop(0, bt, start, 0, unroll=ROW_DMA_UNROLL)
    for kk in range(TOP_K):
        pltpu.make_async_copy(h_ref, xs_ref.at[pl.ds(0, bt)], sem).wait()


def moe_dispatch(h, pos, zero_lo, zero_hi, n_rows, bt=512):
    t, d = h.shape
    bt = min(bt, t)
    pos2 = pos.reshape(t // bt, 1, TOP_K * bt)
    return pl.pallas_call(
        _dispatch_kernel,
        grid_spec=pltpu.PrefetchScalarGridSpec(
            num_scalar_prefetch=2,
            grid=(t // bt,),
            in_specs=[
                pl.BlockSpec((None, 1, TOP_K * bt), lambda i, zl, zh: (i, 0, 0), memory_space=pltpu.SMEM),
                pl.BlockSpec((bt, d), lambda i, zl, zh: (i, 0)),
            ],
            out_specs=pl.BlockSpec(memory_space=pl.ANY),
            scratch_shapes=[pltpu.VMEM((8, d), F32), pltpu.SemaphoreType.DMA(()), pltpu.SemaphoreType.DMA(())],
        ),
        out_shape=jax.ShapeDtypeStruct((n_rows, d), F32),
        compiler_params=_cparams(("arbitrary",)),
        name="moe_dispatch",
    )(zero_lo, zero_hi, pos2, h)


def _gmm_kernel(te_ref, tr_ref, nv_ref, xs_ref, wg_ref, wu_ref, wd_ref, ys_ref, xb_scr):
    i, j = pl.program_id(0), pl.program_id(1)
    bm = xs_ref.shape[0]
    rows = tr_ref[i]
    sub = bm // GMM_SUBTILES

    @pl.when(j == 0)
    def _():
        xb_scr[...] = xs_ref[...].astype(BF16)
        ys_ref[...] = jnp.zeros_like(ys_ref)

    def ffn(r0, n):
        xb = xb_scr[r0:r0 + n, :]
        g = jnp.dot(xb, wg_ref[...], preferred_element_type=F32)
        u = jnp.dot(xb, wu_ref[...], preferred_element_type=F32)
        act = (_silu(g) * u).astype(BF16)
        ys_ref[r0:r0 + n, :] += jnp.dot(act, wd_ref[...], preferred_element_type=F32)

    full = rows == bm

    @pl.when(full)
    def _():
        for s in range(GMM_SUBTILES):
            ffn(s * sub, sub)

    for s in range(GMM_SUBTILES):
        left = rows - s * sub

        @pl.when(jnp.logical_not(full) & (left > sub // 2))
        def _():
            ffn(s * sub, sub)

        @pl.when((left > 0) & (left <= sub // 2))
        def _():
            ffn(s * sub, sub // 2)


def moe_grouped_ffn(xs, tile_expert, tile_rows, n_valid, wg, wu, wd, bm, bf=512):
    p, d = xs.shape
    f = wg.shape[2]
    nf = f // bf

    def row_map(i, j, te, tr, nv):
        return (jnp.minimum(i, nv[0] - 1), 0)

    def fcol(i, j, nv):
        return jnp.where(i < nv[0], j, nf - 1)

    return pl.pallas_call(
        _gmm_kernel,
        grid_spec=pltpu.PrefetchScalarGridSpec(
            num_scalar_prefetch=3,
            grid=(p // bm, nf),
            in_specs=[
                pl.BlockSpec((bm, d), row_map),
                pl.BlockSpec((None, d, bf), lambda i, j, te, tr, nv: (te[i], 0, fcol(i, j, nv))),
                pl.BlockSpec((None, d, bf), lambda i, j, te, tr, nv: (te[i], 0, fcol(i, j, nv))),
                pl.BlockSpec((None, bf, d), lambda i, j, te, tr, nv: (te[i], fcol(i, j, nv), 0)),
            ],
            out_specs=pl.BlockSpec((bm, d), lambda i, j, te, tr, nv: (i, 0)),
            scratch_shapes=[pltpu.VMEM((bm, d), BF16)],
        ),
        out_shape=jax.ShapeDtypeStruct((p, d), F32),
        compiler_params=_cparams(("arbitrary", "arbitrary")),
        name="moe_grouped_ffn",
    )(tile_expert, tile_rows, n_valid, xs, wg, wu, wd)


def _combine_kernel(pos_ref, pos_next_ref, x_ref, route_ref, nw_ref, ys_ref, o_ref, y_scr, sem, *, final_norm):
    bt = x_ref.shape[0]
    i = pl.program_id(0)
    slot = i % 2

    def gather(p_ref, s, unroll):
        def start(r, carry):
            for kk in range(TOP_K):
                _row_copy(ys_ref, y_scr.at[s, kk], sem.at[s], p_ref[0, TOP_K * r + kk], r).start(priority=kk % 2)
            return carry
        lax.fori_loop(0, bt, start, 0, unroll=unroll)

    def wait(s):
        for kk in range(TOP_K):
            pltpu.make_async_copy(ys_ref.at[pl.ds(0, bt)], y_scr.at[s, kk], sem.at[s]).wait()

    @pl.when(i == 0)
    def _():
        gather(pos_ref, 0, ROW_DMA_UNROLL)

    wait(slot)
    gather(pos_next_ref, 1 - slot, True)
    w0 = route_ref[:, ROUTE_W0:ROUTE_W0 + 1]
    w1 = route_ref[:, ROUTE_W1:ROUTE_W1 + 1]
    out = x_ref[...] + (w0 * y_scr[slot, 0] + w1 * y_scr[slot, 1])
    if final_norm:
        out = _rms(out, nw_ref[...])
    o_ref[...] = out

    @pl.when(i + 1 == pl.num_programs(0))
    def _():
        wait(1 - slot)


def moe_combine(x, route, pos, ys, norm_w, final_norm, bt=256):
    t, d = x.shape
    bt = min(bt, t)
    n = t // bt
    pos2 = pos.reshape(n, 1, TOP_K * bt)
    pos_spec = lambda nxt: pl.BlockSpec((None, 1, TOP_K * bt), lambda i: (jnp.minimum(i + nxt, n - 1), 0, 0),
                                        memory_space=pltpu.SMEM)
    return pl.pallas_call(
        functools.partial(_combine_kernel, final_norm=final_norm),
        grid=(n,),
        in_specs=[
            pos_spec(0),
            pos_spec(1),
            pl.BlockSpec((bt, d), lambda i: (i, 0)),
            pl.BlockSpec((bt, LANES), lambda i: (i, 0)),
            pl.BlockSpec((1, d), lambda i: (0, 0)),
            pl.BlockSpec(memory_space=pl.ANY),
        ],
        out_specs=pl.BlockSpec((bt, d), lambda i: (i, 0)),
        out_shape=jax.ShapeDtypeStruct((t, d), F32),
        scratch_shapes=[pltpu.VMEM((2, TOP_K, bt, d), F32), pltpu.SemaphoreType.DMA((2,))],
        compiler_params=_cparams(("arbitrary",)),
        name="moe_combine",
    )(pos2, pos2, x, route, norm_w.reshape(1, d), ys)


def moe_ffn(x, h, route, route_t, counts, wg, wu, wd, norm_w, final_norm, bm=GMM_TILE):
    t, d = x.shape

    cnt = counts[0, :N_EXPERTS].astype(jnp.int32)
    tiles = (cnt + bm - 1) // bm
    tile_end = jnp.cumsum(tiles)
    offset = (tile_end - tiles) * bm
    n_tiles = (TOP_K * t) // bm + N_EXPERTS
    n_valid = tile_end[-1:].astype(jnp.int32)
    tile_id = jnp.minimum(jnp.arange(n_tiles, dtype=jnp.int32), n_valid[0] - 1)
    tile_expert = jnp.sum((tile_end[None, :] <= tile_id[:, None]).astype(jnp.int32), axis=1)
    tile_in_group = tile_id - (tile_end - tiles)[tile_expert]
    tile_rows = jnp.clip(cnt[tile_expert] - tile_in_group * bm, 0, bm)
    tile_rows = jnp.where(jnp.arange(n_tiles) < n_valid[0], tile_rows, 0).astype(jnp.int32)
    fields = route_t.astype(jnp.int32)
    pos = jnp.stack([offset[fields[ROUTE_E0 + kk]] + fields[ROUTE_R0 + kk] for kk in range(TOP_K)], axis=-1)
    zero_lo = offset + cnt
    zero_hi = jnp.concatenate([offset[1:], jnp.full((1,), n_tiles * bm, jnp.int32)])

    xs = moe_dispatch(h, pos, zero_lo, zero_hi, n_tiles * bm)
    ys = moe_grouped_ffn(xs, tile_expert, tile_rows, n_valid, wg, wu, wd, bm)
    return moe_combine(x, route, pos, ys, norm_w, final_norm)


def _norm_kernel(x_ref, nw_ref, o_ref):
    o_ref[...] = _rms(x_ref[...], nw_ref[...])


def final_norm(x, norm_w, bm=512):
    t, d = x.shape
    bm = min(bm, t)
    return pl.pallas_call(
        _norm_kernel,
        grid=(t // bm,),
        in_specs=[pl.BlockSpec((bm, d), lambda i: (i, 0)), pl.BlockSpec((1, d), lambda i: (0, 0))],
        out_specs=pl.BlockSpec((bm, d), lambda i: (i, 0)),
        out_shape=jax.ShapeDtypeStruct((t, d), F32),
        compiler_params=_cparams(("parallel",)),
        name="final_norm",
    )(x, norm_w.reshape(1, d))


def kernel(x, mix_norm_w, w_in, gmlp_w_s, gmlp_b_s, gmlp_ln_w, gmlp_ln_b, gmlp_out_w, hgrn_lb_logits, hgrn_out_w, w_o, ffn_norm_w, dense_w_gate, dense_w_up, dense_w_down, router_w, expert_w_gate, expert_w_up, expert_w_down, final_norm_w):
    batch, seq, d = x.shape
    depth = w_in.shape[0]
    d_gmlp = gmlp_ln_w.shape[1]
    d_hgrn = hgrn_out_w.shape[1]
    assert d_gmlp == d_hgrn and seq % HGRN_C == 0 and seq % GMLP_CHUNK == 0

    lbs = jax.nn.softmax(hgrn_lb_logits.astype(F32), axis=0)
    lbs = jnp.cumsum(lbs, axis=0) - lbs[0:1]

    xt = x.reshape(batch * seq, d)
    w_in_bf16 = w_in[0].astype(BF16)
    for l in range(depth):
        last = l == depth - 1
        j = l // 2
        soon = [] if last else [(w_in.reshape(depth * d, -1), (l + 1) * d, d)]
        if l % 2 == 0:
            soon += [dense_w_gate[j], dense_w_up[j], dense_w_down[j]]
        z, narrowed = in_proj(xt, mix_norm_w[l], w_in_bf16, cast=tuple(soon))
        narrowed = list(narrowed)
        if not last:
            w_in_bf16 = narrowed.pop(0)
        y_a = gmlp_mixer(z, gmlp_w_s[l], gmlp_b_s[l], gmlp_ln_w[l], gmlp_ln_b[l], gmlp_out_w[l])
        y_b = hgrn2_mixer(z, lbs[l], hgrn_out_w[l], batch, col0=2 * d_gmlp // d_hgrn)
        if l % 2 == 0:
            xt, h = out_proj(xt, y_a, y_b, w_o, l, ffn_norm_w[l])
            experts = () if last else tuple(w[j].reshape(-1, w.shape[-1])
                                            for w in (expert_w_gate, expert_w_up, expert_w_down))
            xt, experts_bf16 = dense_ffn(xt, h, *narrowed, cast=experts)
            if last:
                xt = final_norm(xt, final_norm_w)
        else:
            xt, h, route, route_t, counts = out_proj(xt, y_a, y_b, w_o, l, ffn_norm_w[l], router_w=router_w[j])
            wg, wu, wd = (w.reshape(s.shape[1:]) for w, s in
                          zip(experts_bf16, (expert_w_gate, expert_w_up, expert_w_down)))
            xt = moe_ffn(xt, h, route, route_t, counts, wg, wu, wd, final_norm_w, final_norm=last)
    return xt.reshape(batch, seq, d)
```

```python
import functools

import jax
import jax.numpy as jnp
from jax import lax
from jax.experimental import pallas as pl
from jax.experimental.pallas import tpu as pltpu

F32 = jnp.float32
BF16 = jnp.bfloat16

GMLP_HEAD = 128
GMLP_CHUNK = 128
HGRN_HEAD = 128
N_EXPERTS = 8
TOP_K = 2
EPS = 1e-6
F_MIN = 1e-6

LANES = 128
GMLP_GROUP = 8
HGRN_C = 128
HGRN_GROUP = 8
HGRN_DIAG = 8
HGRN_DIAG_REF = 3
VMEM_LIMIT = 56 * 1024 * 1024
VMEM_LIMIT_IN_PROJ = 60 * 1024 * 1024


def _cparams(sem, vmem=VMEM_LIMIT):
    return pltpu.CompilerParams(dimension_semantics=sem, vmem_limit_bytes=vmem)


def _silu(x):
    hx = 0.5 * x
    return hx + hx * jnp.tanh(hx)


GELU_C1 = 0.7978845608028654
GELU_C2 = GELU_C1 * 0.044715


def _gelu_tanh(x):
    hx = 0.5 * x
    return hx + hx * jnp.tanh(x * (GELU_C1 + GELU_C2 * (x * x)))


def _rms(x, w):
    return x * lax.rsqrt(jnp.mean(x * x, axis=-1, keepdims=True) + EPS) * w


BF16_SUBLANES = 16


def _cast_block_specs(items, ni, nj, nk=1):
    arrays, in_specs, out_specs, out_shapes = [], [], [], []
    for item in items:
        w, row0, r = item if isinstance(item, tuple) else (item, 0, item.shape[0])
        c = w.shape[1]
        if c % nj == 0 and (c // nj) % LANES == 0 and r % (ni * nk * BF16_SUBLANES) == 0:
            block = (r // (ni * nk), c // nj)
            index = lambda i, j, k=0, off=0: (i * nk + k + off, j)
        else:
            assert r % (ni * nj * nk * BF16_SUBLANES) == 0, (w.shape, r, ni, nj, nk)
            block = (r // (ni * nj * nk), c)
            index = lambda i, j, k=0, off=0: ((i * nj + j) * nk + k + off, 0)
        assert row0 % block[0] == 0
        arrays.append(w)
        in_specs.append(pl.BlockSpec(block, functools.partial(index, off=row0 // block[0])))
        out_specs.append(pl.BlockSpec(block, index))
        out_shapes.append(jax.ShapeDtypeStruct((r, c), BF16))
    return arrays, in_specs, out_specs, out_shapes


def _narrow_slabs(src_refs, dst_refs):
    for src_ref, dst_ref in zip(src_refs, dst_refs):
        dst_ref[...] = src_ref[...].astype(BF16)


def _inproj_kernel(x_ref, nw_ref, w_ref, *rest):
    n_cast = (len(rest) - 2) // 2
    z_ref, h_scr = rest[n_cast], rest[-1]

    @pl.when(pl.program_id(1) == 0)
    def _():
        h_scr[...] = _rms(x_ref[...], nw_ref[...]).astype(BF16)

    _narrow_slabs(rest[:n_cast], rest[n_cast + 1:-1])
    z_ref[...] = jnp.dot(h_scr[...], w_ref[...], preferred_element_type=F32)


def in_proj(x, norm_w, w, cast=(), bm=1024, bn=1536):
    t, d = x.shape
    n = w.shape[1]
    bm = min(bm, t)
    ni, nj = t // bm, n // bn
    cast_arrays, cast_in, cast_out, cast_shapes = _cast_block_specs(cast, ni, nj)
    outs = pl.pallas_call(
        _inproj_kernel,
        grid=(ni, nj),
        in_specs=[
            pl.BlockSpec((bm, d), lambda i, j: (i, 0)),
            pl.BlockSpec((1, d), lambda i, j: (0, 0)),
            pl.BlockSpec((d, bn), lambda i, j: (0, j)),
        ] + cast_in,
        out_specs=[pl.BlockSpec((bm, bn), lambda i, j: (i, j))] + cast_out,
        out_shape=[jax.ShapeDtypeStruct((t, n), F32)] + cast_shapes,
        scratch_shapes=[pltpu.VMEM((bm, d), BF16)],
        compiler_params=_cparams(("parallel", "arbitrary"), VMEM_LIMIT_IN_PROJ),
        name="in_proj",
    )(x, norm_w.reshape(1, d), w, *cast_arrays)
    return outs[0], tuple(outs[1:])


def _gmlp_kernel(zu_ref, zv_ref, ws_ref, bs_ref, lnw_ref, lnb_ref, ow_ref, y_ref, wc_scr, *, n_heads, n_chunks):
    c = GMLP_CHUNK
    row = lax.broadcasted_iota(jnp.int32, (c, c), 0)
    col = lax.broadcasted_iota(jnp.int32, (c, c), 1)
    causal = row >= col
    ones = jnp.ones((GMLP_HEAD, GMLP_HEAD), BF16)
    lane_mean = lambda x: jnp.dot(x.astype(BF16), ones, preferred_element_type=F32) * (1.0 / GMLP_HEAD)

    for h in range(n_heads):
        wc_scr[h] = jnp.where(causal, ws_ref[h], 0.0).astype(BF16)

    def chunk(ci, carry):
        r0 = pl.multiple_of(ci * c, c)
        for g0 in range(0, n_heads, GMLP_GROUP):
            heads = range(g0, g0 + GMLP_GROUP)
            sls = [slice(h * GMLP_HEAD, (h + 1) * GMLP_HEAD) for h in heads]
            v = [_gelu_tanh(zv_ref[pl.ds(r0, c), sl]) for sl in sls]
            mu = [lane_mean(x) for x in v]
            vc = [x - m for x, m in zip(v, mu)]
            var = [lane_mean(x * x) for x in vc]
            vn = [(x * lax.rsqrt(s + EPS) * lnw_ref[:, sl] + lnb_ref[:, sl]).astype(BF16)
                  for x, s, sl in zip(vc, var, sls)]
            sv = [jnp.dot(wc_scr[h], x, preferred_element_type=F32) + bs_ref[:, h:h + 1] for h, x in zip(heads, vn)]
            y = [_gelu_tanh(zu_ref[pl.ds(r0, c), sl]) * x for sl, x in zip(sls, sv)]
            ms = [lane_mean(x * x) for x in y]
            for x, s, sl in zip(y, ms, sls):
                y_ref[pl.ds(r0, c), sl] = (x * lax.rsqrt(s + EPS) * ow_ref[:, sl]).astype(BF16)
        return carry

    lax.fori_loop(0, n_chunks, chunk, 0)


def gmlp_mixer(z, w_s, b_s, ln_w, ln_b, out_w, rows=1024):
    t = z.shape[0]
    n_heads = w_s.shape[0]
    dg = n_heads * GMLP_HEAD
    rows = min(rows, t)
    kern = functools.partial(_gmlp_kernel, n_heads=n_heads, n_chunks=rows // GMLP_CHUNK)
    vec = lambda a: a.reshape(1, dg)
    return pl.pallas_call(
        kern,
        grid=(t // rows,),
        in_specs=[
            pl.BlockSpec((rows, dg), lambda i: (i, 0)),
            pl.BlockSpec((rows, dg), lambda i: (i, 1)),
            pl.BlockSpec((n_heads, GMLP_CHUNK, GMLP_CHUNK), lambda i: (0, 0, 0)),
            pl.BlockSpec((GMLP_CHUNK, n_heads), lambda i: (0, 0)),
            pl.BlockSpec((1, dg), lambda i: (0, 0)),
            pl.BlockSpec((1, dg), lambda i: (0, 0)),
            pl.BlockSpec((1, dg), lambda i: (0, 0)),
        ],
        out_specs=pl.BlockSpec((rows, dg), lambda i: (i, 0)),
        out_shape=jax.ShapeDtypeStruct((t, dg), BF16),
        scratch_shapes=[pltpu.VMEM((n_heads, GMLP_CHUNK, GMLP_CHUNK), BF16)],
        compiler_params=_cparams(("parallel",)),
        name="gmlp_mixer",
    )(z, z, w_s, b_s.T, vec(ln_w), vec(ln_b), vec(out_w))


def _hgrn_level_terms(a, q, k, blk):
    c, d = a.shape
    half = blk // 2
    nb = c // blk
    lower = lambda v: v.reshape(nb, blk, d)[:, :half, :]
    upper = lambda v: v.reshape(nb, blk, d)[:, half:, :]
    ref = a.reshape(nb, blk, d)[:, half - 1:half, :]
    qh = upper(q) * jnp.exp2(upper(a) - ref)
    kh = lower(k) * jnp.exp2(ref - lower(a))
    zeros = jnp.zeros_like(qh)
    qh = jnp.concatenate([zeros, qh], axis=1).reshape(c, d)
    kh = jnp.concatenate([kh, zeros], axis=1).reshape(c, d)
    return qh.astype(BF16), kh.astype(BF16)


def _hgrn_kernel(zq_ref, zf_ref, zi_ref, zg_ref, lb_ref, ow_ref, y_ref, st_ref, *, n_heads, n_chunks):
    c, d = HGRN_C, HGRN_HEAD
    nt = (((1,), (1,)), ((), ()))
    tn = (((0,), (0,)), ((), ()))

    @pl.when(pl.program_id(1) == 0)
    def _():
        st_ref[...] = jnp.zeros_like(st_ref)

    row = lax.broadcasted_iota(jnp.int32, (c, c), 0)
    col = lax.broadcasted_iota(jnp.int32, (c, c), 1)
    tril = (row >= col).astype(BF16)
    diag_mask = (row >= col) & ((row // HGRN_DIAG) == (col // HGRN_DIAG))
    level_masks = []
    blk = 2 * HGRN_DIAG
    while blk <= c:
        level_masks.append((blk, None if blk == c else (row // blk) == (col // blk)))
        blk *= 2

    def nt_dot(x, y):
        return lax.dot_general(x, y, nt, preferred_element_type=F32)

    def chunk(ci, carry):
        r0 = pl.multiple_of(ci * c, c)
        for g0 in range(0, n_heads, HGRN_GROUP):
            heads = list(range(g0, g0 + HGRN_GROUP))
            sls = [slice(h * d, (h + 1) * d) for h in heads]
            f = [0.5 * (1.0 + lb_ref[:, sl]) + (0.5 * (1.0 - lb_ref[:, sl])) * jnp.tanh(0.5 * zf_ref[pl.ds(r0, c), sl])
                 for sl in sls]
            logf = [jnp.log2(jnp.clip(x, F_MIN, 1.0)) for x in f]
            pieces = [jnp.concatenate(_split_bf16(x), axis=1) for x in logf]
            cs = [jnp.dot(tril, x, preferred_element_type=F32) for x in pieces]
            a = [x[:, :d] + x[:, d:] for x in cs]
            k = [1.0 - x for x in f]
            q = [_silu(zq_ref[pl.ds(r0, c), sl]) for sl in sls]
            iv = [zi_ref[pl.ds(r0, c), sl].astype(BF16) for sl in sls]

            dd = []
            for x in a:
                a3 = x.reshape(c // HGRN_DIAG, HGRN_DIAG, d)
                dd.append((a3 - a3[:, HGRN_DIAG_REF:HGRN_DIAG_REF + 1, :]).reshape(c, d))
            sd = [nt_dot((qx * jnp.exp2(x)).astype(BF16), (kx * jnp.exp2(-x)).astype(BF16))
                  for qx, kx, x in zip(q, k, dd)]

            scores = None
            for blk, same_block in reversed(level_masks):
                terms = [_hgrn_level_terms(ax, qx, kx, blk) for ax, qx, kx in zip(a, q, k)]
                prods = [nt_dot(qh, kh) for qh, kh in terms]
                scores = prods if scores is None else [jnp.where(same_block, p, s) for s, p in zip(scores, prods)]
            scores = [jnp.where(diag_mask, x, s) for x, s in zip(sd, scores)]

            st = [st_ref[h] for h in heads]
            qe = [(qx * jnp.exp2(ax)).astype(BF16) for qx, ax in zip(q, a)]
            o = [jnp.dot(s.astype(BF16), ivx, preferred_element_type=F32) + nt_dot(qx, sx.astype(BF16))
                 for s, ivx, qx, sx in zip(scores, iv, qe, st)]
            kd = [(kx * jnp.exp2(ax[c - 1:c, :] - ax)).astype(BF16) for kx, ax in zip(k, a)]
            for h, ax, sx, ivx, kx in zip(heads, a, st, iv, kd):
                st_ref[h] = jnp.exp2(ax[c - 1:c, :]) * sx + lax.dot_general(ivx, kx, tn, preferred_element_type=F32)
            for x, sl in zip(o, sls):
                y_ref[pl.ds(r0, c), sl] = (_rms(x, ow_ref[:, sl]) * _silu(zg_ref[pl.ds(r0, c), sl])).astype(BF16)
        return carry

    lax.fori_loop(0, n_chunks, chunk, 0)


def hgrn2_mixer(z, lb, out_w, batch, col0, rows=1024):
    t = z.shape[0]
    s = t // batch
    dh = lb.shape[0]
    n_heads = dh // HGRN_HEAD
    rows = min(rows, s)
    spb = s // rows
    kern = functools.partial(_hgrn_kernel, n_heads=n_heads, n_chunks=rows // HGRN_C)
    zspec = lambda j: pl.BlockSpec((rows, dh), lambda b, i, j=j: (b * spb + i, col0 + j))
    return pl.pallas_call(
        kern,
        grid=(batch, spb),
        in_specs=[zspec(0), zspec(1), zspec(2), zspec(3),
                  pl.BlockSpec((1, dh), lambda b, i: (0, 0)),
                  pl.BlockSpec((1, dh), lambda b, i: (0, 0))],
        out_specs=pl.BlockSpec((rows, dh), lambda b, i: (b * spb + i, 0)),
        out_shape=jax.ShapeDtypeStruct((t, dh), BF16),
        scratch_shapes=[pltpu.VMEM((n_heads, HGRN_HEAD, HGRN_HEAD), F32)],
        compiler_params=_cparams(("parallel", "arbitrary")),
        name="hgrn2_mixer",
    )(z, z, z, z, lb.reshape(1, dh), out_w.reshape(1, dh))


ROUTE_E0, ROUTE_E1, ROUTE_W0, ROUTE_W1, ROUTE_R0, ROUTE_R1 = range(6)
ROUTE_FIELDS = 8


def _split_bf16(v):
    hi = v.astype(BF16)
    return hi, (v - hi.astype(F32)).astype(BF16)


def _route_rows(h, rw_pieces, carry_ref):
    bm = h.shape[0]
    h_hi, h_lo = _split_bf16(h)
    hh_hl = jnp.dot(h_hi, rw_pieces, preferred_element_type=F32)
    logits = hh_hl[:, :LANES] + hh_hl[:, LANES:] + jnp.dot(h_lo, rw_pieces[:, :LANES], preferred_element_type=F32)
    lane = lax.broadcasted_iota(jnp.int32, (bm, LANES), 1)
    neg = jnp.float32(-jnp.inf)
    logits = jnp.where(lane < N_EXPERTS, logits, neg)
    v0 = jnp.max(logits, axis=-1, keepdims=True)
    e0 = jnp.min(jnp.where(logits == v0, lane, LANES), axis=-1, keepdims=True)
    rest = jnp.where(lane == e0, neg, logits)
    v1 = jnp.max(rest, axis=-1, keepdims=True)
    e1 = jnp.min(jnp.where(rest == v1, lane, LANES), axis=-1, keepdims=True)
    ex = jnp.exp(v1 - v0)
    w0 = 1.0 / (1.0 + ex)
    w1 = ex / (1.0 + ex)

    onehot = ((lane == e0) | (lane == e1)).astype(F32)
    row = lax.broadcasted_iota(jnp.int32, (bm, bm), 0)
    col = lax.broadcasted_iota(jnp.int32, (bm, bm), 1)
    before = (row > col).astype(BF16)
    excl = jnp.dot(before, onehot.astype(BF16), preferred_element_type=F32) + carry_ref[...]
    r0 = jnp.sum(jnp.where(lane == e0, excl, 0.0), axis=-1, keepdims=True)
    r1 = jnp.sum(jnp.where(lane == e1, excl, 0.0), axis=-1, keepdims=True)
    carry_ref[...] += jnp.sum(onehot, axis=0, keepdims=True)

    out = jnp.zeros((bm, LANES), F32)
    for idx, val in ((ROUTE_E0, e0.astype(F32)), (ROUTE_E1, e1.astype(F32)), (ROUTE_W0, w0), (ROUTE_W1, w1),
                     (ROUTE_R0, r0), (ROUTE_R1, r1)):
        out = jnp.where(lane == idx, val, out)
    return out


def _outproj_kernel(*refs, da, moe):
    if moe:
        (x_ref, ya_ref, yb_ref, wo_ref, nw_ref, rw_ref, xo_ref, h_ref, route_ref, route_t_ref, cnt_ref,
         wo_scr, rw_scr, carry_scr) = refs
    else:
        x_ref, ya_ref, yb_ref, wo_ref, nw_ref, xo_ref, h_ref, wo_scr = refs

    @pl.when(pl.program_id(0) == 0)
    def _():
        wo_scr[...] = wo_ref[...].astype(BF16)
        if moe:
            rw_hi, rw_lo = _split_bf16(rw_ref[...])
            rw_scr[:, :LANES] = rw_hi
            rw_scr[:, LANES:] = rw_lo
            carry_scr[...] = jnp.zeros_like(carry_scr)

    acc = x_ref[...] + jnp.dot(ya_ref[...], wo_scr[:da, :], preferred_element_type=F32) \
        + jnp.dot(yb_ref[...], wo_scr[da:, :], preferred_element_type=F32)
    xo_ref[...] = acc
    h = _rms(acc, nw_ref[...])
    h_ref[...] = h.astype(h_ref.dtype)
    if moe:
        route = _route_rows(h, rw_scr[...], carry_scr)
        route_ref[...] = route
        route_t_ref[...] = route.T[:route_t_ref.shape[0], :]
        cnt_ref[...] = carry_scr[...]


def out_proj(x, y_a, y_b, w_o, layer, norm_w, router_w=None, bm=None):
    t, d = x.shape
    da, db = y_a.shape[1], y_b.shape[1]
    moe = router_w is not None
    bm = min(bm or (256 if moe else 512), t)
    row_spec = lambda w: pl.BlockSpec((bm, w), lambda i: (i, 0))
    const = lambda shape: pl.BlockSpec(shape, lambda i: (0,) * len(shape), pipeline_mode=pl.Buffered(1))
    w_spec = pl.BlockSpec((None, da + db, d), lambda i: (layer, 0, 0), pipeline_mode=pl.Buffered(1))
    in_specs = [row_spec(d), row_spec(da), row_spec(db), w_spec, const((1, d))]
    out_specs = [row_spec(d), row_spec(d)]
    out_shape = [jax.ShapeDtypeStruct((t, d), F32), jax.ShapeDtypeStruct((t, d), F32 if moe else BF16)]
    scratch = [pltpu.VMEM((da + db, d), BF16)]
    args = [x, y_a, y_b, w_o, norm_w.reshape(1, d)]
    if moe:
        in_specs.append(const((d, LANES)))
        args.append(jnp.zeros((d, LANES), F32).at[:, :N_EXPERTS].set(router_w))
        out_specs += [row_spec(LANES), pl.BlockSpec((ROUTE_FIELDS, bm), lambda i: (0, i)),
                      pl.BlockSpec((1, LANES), lambda i: (0, 0))]
        out_shape += [jax.ShapeDtypeStruct((t, LANES), F32), jax.ShapeDtypeStruct((ROUTE_FIELDS, t), F32),
                      jax.ShapeDtypeStruct((1, LANES), F32)]
        scratch += [pltpu.VMEM((d, 2 * LANES), BF16), pltpu.VMEM((1, LANES), F32)]
    return pl.pallas_call(
        functools.partial(_outproj_kernel, da=da, moe=moe),
        grid=(t // bm,),
        in_specs=in_specs,
        out_specs=out_specs,
        out_shape=out_shape,
        scratch_shapes=scratch,
        compiler_params=_cparams(("arbitrary",)),
        name="out_proj_route" if moe else "out_proj",
    )(*args)


def _ffn_kernel(x_hbm, h_ref, wg_ref, wu_ref, wd_ref, *rest, n_pass):
    n_cast = (len(rest) - 2) // 2
    o_ref, sem = rest[n_cast], rest[-1]
    i, j, k = pl.program_id(0), pl.program_id(1), pl.program_id(2)
    bm = o_ref.shape[0]
    sub = bm // n_pass

    def x_copy(p):
        src = x_hbm.at[pl.ds(pl.multiple_of(i * bm + p * sub, sub), sub)]
        return pltpu.make_async_copy(src, o_ref.at[pl.ds(p * sub, sub)], sem.at[p])

    @pl.when((j == 0) & (k == 0))
    def _():
        for p in range(n_pass):
            x_copy(p).start()

    for p in range(n_pass):
        @pl.when((j == 0) & (k == p))
        def _():
            x_copy(p).wait()

    _narrow_slabs(rest[:n_cast], rest[n_cast + 1:-1])

    rows = pl.ds(pl.multiple_of(k * sub, sub), sub)
    hb = h_ref[rows, :]
    g = jnp.dot(hb, wg_ref[...], preferred_element_type=F32)
    u = jnp.dot(hb, wu_ref[...], preferred_element_type=F32)
    act = (_silu(g) * u).astype(BF16)
    o_ref[rows, :] += jnp.dot(act, wd_ref[...], preferred_element_type=F32)


def dense_ffn(x, h, wg, wu, wd, cast=(), bm=1024, bf=512, n_pass=2):
    t, d = x.shape
    f = wg.shape[1]
    bm = min(bm, t)
    ni, nj = t // bm, f // bf
    cast_arrays, cast_in, cast_out, cast_shapes = _cast_block_specs(cast, ni, nj, n_pass)
    outs = pl.pallas_call(
        functools.partial(_ffn_kernel, n_pass=n_pass),
        grid=(ni, nj, n_pass),
        in_specs=[
            pl.BlockSpec(memory_space=pl.ANY),
            pl.BlockSpec((bm, d), lambda i, j, k: (i, 0)),
            pl.BlockSpec((d, bf), lambda i, j, k: (0, j)),
            pl.BlockSpec((d, bf), lambda i, j, k: (0, j)),
            pl.BlockSpec((bf, d), lambda i, j, k: (j, 0)),
        ] + cast_in,
        out_specs=[pl.BlockSpec((bm, d), lambda i, j, k: (i, 0))] + cast_out,
        out_shape=[jax.ShapeDtypeStruct((t, d), F32)] + cast_shapes,
        scratch_shapes=[pltpu.SemaphoreType.DMA((n_pass,))],
        compiler_params=_cparams(("parallel", "arbitrary", "arbitrary")),
        name="dense_ffn",
    )(x, h, wg, wu, wd, *cast_arrays)
    return outs[0], tuple(outs[1:])


def _row_copy(src_ref, dst_ref, sem, src_row, dst_row):
    return pltpu.make_async_copy(src_ref.at[pl.ds(src_row, 1)], dst_ref.at[pl.ds(dst_row, 1)], sem)


ROW_DMA_UNROLL = 8
GMM_TILE = 1024
GMM_SUBTILES = 2


def _dispatch_kernel(zlo_ref, zhi_ref, pos_ref, h_ref, xs_ref, zero_scr, sem, zsem):
    bt = h_ref.shape[0]

    @pl.when(pl.program_id(0) == 0)
    def _():
        zero_scr[...] = jnp.zeros_like(zero_scr)
        zr = zero_scr.shape[0]

        def zero_rows(e, begin):
            lo, hi = zlo_ref[e], zhi_ref[e]
            mid = jnp.minimum(hi, (lo + zr - 1) // zr * zr)

            def one(r, carry):
                cp = _row_copy(zero_scr, xs_ref, zsem, 0, r)
                cp.start() if begin else cp.wait()
                return carry

            def slab(b, carry):
                cp = pltpu.make_async_copy(zero_scr, xs_ref.at[pl.ds(pl.multiple_of(b * zr, zr), zr)], zsem)
                cp.start() if begin else cp.wait()
                return carry

            lax.fori_loop(lo, mid, one, 0)
            lax.fori_loop(mid // zr, hi // zr, slab, 0)

        for e in range(N_EXPERTS):
            zero_rows(e, True)
        for e in range(N_EXPERTS):
            zero_rows(e, False)

    def start(r, carry):
        for kk in range(TOP_K):
            _row_copy(h_ref, xs_ref, sem, r, pos_ref[0, kk * bt + r]).start(priority=kk % 2)
        return carry

    lax.fori_loop(0, bt, start, 0, unroll=ROW_DMA_UNROLL)
    for kk in range(TOP_K):
        pltpu.make_async_copy(h_ref, xs_ref.at[pl.ds(0, bt)], sem).wait()


def _tile_positions(pos, bt):
    t = pos.shape[1]
    return pos.reshape(TOP_K, t // bt, bt).transpose(1, 0, 2).reshape(t // bt, 1, TOP_K * bt)


def moe_dispatch(h, pos, zero_lo, zero_hi, n_rows, bt=512):
    t, d = h.shape
    bt = min(bt, t)
    pos2 = _tile_positions(pos, bt)
    return pl.pallas_call(
        _dispatch_kernel,
        grid_spec=pltpu.PrefetchScalarGridSpec(
            num_scalar_prefetch=2,
            grid=(t // bt,),
            in_specs=[
                pl.BlockSpec((None, 1, TOP_K * bt), lambda i, zl, zh: (i, 0, 0), memory_space=pltpu.SMEM),
                pl.BlockSpec((bt, d), lambda i, zl, zh: (i, 0)),
            ],
            out_specs=pl.BlockSpec(memory_space=pl.ANY),
            scratch_shapes=[pltpu.VMEM((8, d), F32), pltpu.SemaphoreType.DMA(()), pltpu.SemaphoreType.DMA(())],
        ),
        out_shape=jax.ShapeDtypeStruct((n_rows, d), F32),
        compiler_params=_cparams(("arbitrary",)),
        name="moe_dispatch",
    )(zero_lo, zero_hi, pos2, h)


def _gmm_kernel(te_ref, tr_ref, nv_ref, xs_ref, wg_ref, wu_ref, wd_ref, ys_ref, xb_scr):
    i, j = pl.program_id(0), pl.program_id(1)
    bm = xs_ref.shape[0]
    rows = tr_ref[i]
    sub = bm // GMM_SUBTILES

    @pl.when(j == 0)
    def _():
        xb_scr[...] = xs_ref[...].astype(BF16)
        ys_ref[...] = jnp.zeros_like(ys_ref)

    def ffn(r0, n):
        xb = xb_scr[r0:r0 + n, :]
        g = jnp.dot(xb, wg_ref[...], preferred_element_type=F32)
        u = jnp.dot(xb, wu_ref[...], preferred_element_type=F32)
        act = (_silu(g) * u).astype(BF16)
        ys_ref[r0:r0 + n, :] += jnp.dot(act, wd_ref[...], preferred_element_type=F32)

    full = rows == bm

    @pl.when(full)
    def _():
        for s in range(GMM_SUBTILES):
            ffn(s * sub, sub)

    for s in range(GMM_SUBTILES):
        left = rows - s * sub

        @pl.when(jnp.logical_not(full) & (left > sub // 2))
        def _():
            ffn(s * sub, sub)

        @pl.when((left > 0) & (left <= sub // 2))
        def _():
            ffn(s * sub, sub // 2)


def moe_grouped_ffn(xs, tile_expert, tile_rows, n_valid, wg, wu, wd, bm, bf=512):
    p, d = xs.shape
    f = wg.shape[2]
    nf = f // bf

    def row_map(i, j, te, tr, nv):
        return (jnp.minimum(i, nv[0] - 1), 0)

    def fcol(i, j, nv):
        return jnp.where(i < nv[0], j, nf - 1)

    return pl.pallas_call(
        _gmm_kernel,
        grid_spec=pltpu.PrefetchScalarGridSpec(
            num_scalar_prefetch=3,
            grid=(p // bm, nf),
            in_specs=[
                pl.BlockSpec((bm, d), row_map),
                pl.BlockSpec((None, d, bf), lambda i, j, te, tr, nv: (te[i], 0, fcol(i, j, nv))),
                pl.BlockSpec((None, d, bf), lambda i, j, te, tr, nv: (te[i], 0, fcol(i, j, nv))),
                pl.BlockSpec((None, bf, d), lambda i, j, te, tr, nv: (te[i], fcol(i, j, nv), 0)),
            ],
            out_specs=pl.BlockSpec((bm, d), lambda i, j, te, tr, nv: (i, 0)),
            scratch_shapes=[pltpu.VMEM((bm, d), BF16)],
        ),
        out_shape=jax.ShapeDtypeStruct((p, d), F32),
        compiler_params=_cparams(("arbitrary", "arbitrary")),
        name="moe_grouped_ffn",
    )(tile_expert, tile_rows, n_valid, xs, wg, wu, wd)


def _combine_kernel(pos_ref, pos_next_ref, x_ref, route_ref, nw_ref, ys_ref, o_ref, y_scr, sem, *, final_norm):
    bt = x_ref.shape[0]
    i = pl.program_id(0)
    slot = i % 2

    def gather(p_ref, s, unroll):
        def start(r, carry):
            for kk in range(TOP_K):
                _row_copy(ys_ref, y_scr.at[s, kk], sem.at[s], p_ref[0, kk * bt + r], r).start(priority=kk % 2)
            return carry
        lax.fori_loop(0, bt, start, 0, unroll=unroll)

    def wait(s):
        for kk in range(TOP_K):
            pltpu.make_async_copy(ys_ref.at[pl.ds(0, bt)], y_scr.at[s, kk], sem.at[s]).wait()

    @pl.when(i == 0)
    def _():
        gather(pos_ref, 0, ROW_DMA_UNROLL)

    wait(slot)
    gather(pos_next_ref, 1 - slot, True)
    w0 = route_ref[:, ROUTE_W0:ROUTE_W0 + 1]
    w1 = route_ref[:, ROUTE_W1:ROUTE_W1 + 1]
    out = x_ref[...] + (w0 * y_scr[slot, 0] + w1 * y_scr[slot, 1])
    if final_norm:
        out = _rms(out, nw_ref[...])
    o_ref[...] = out

    @pl.when(i + 1 == pl.num_programs(0))
    def _():
        wait(1 - slot)


def moe_combine(x, route, pos, ys, norm_w, final_norm, bt=256):
    t, d = x.shape
    bt = min(bt, t)
    n = t // bt
    pos2 = _tile_positions(pos, bt)
    pos_spec = lambda nxt: pl.BlockSpec((None, 1, TOP_K * bt), lambda i: (jnp.minimum(i + nxt, n - 1), 0, 0),
                                        memory_space=pltpu.SMEM)
    return pl.pallas_call(
        functools.partial(_combine_kernel, final_norm=final_norm),
        grid=(n,),
        in_specs=[
            pos_spec(0),
            pos_spec(1),
            pl.BlockSpec((bt, d), lambda i: (i, 0)),
            pl.BlockSpec((bt, LANES), lambda i: (i, 0)),
            pl.BlockSpec((1, d), lambda i: (0, 0)),
            pl.BlockSpec(memory_space=pl.ANY),
        ],
        out_specs=pl.BlockSpec((bt, d), lambda i: (i, 0)),
        out_shape=jax.ShapeDtypeStruct((t, d), F32),
        scratch_shapes=[pltpu.VMEM((2, TOP_K, bt, d), F32), pltpu.SemaphoreType.DMA((2,))],
        compiler_params=_cparams(("arbitrary",)),
        name="moe_combine",
    )(pos2, pos2, x, route, norm_w.reshape(1, d), ys)


def moe_ffn(x, h, route, route_t, counts, wg, wu, wd, norm_w, final_norm, bm=GMM_TILE):
    t, d = x.shape

    cnt = counts[0, :N_EXPERTS].astype(jnp.int32)
    tiles = (cnt + bm - 1) // bm
    tile_end = jnp.cumsum(tiles)
    offset = (tile_end - tiles) * bm
    n_tiles = (TOP_K * t) // bm + N_EXPERTS
    n_valid = tile_end[-1:].astype(jnp.int32)
    tile_id = jnp.minimum(jnp.arange(n_tiles, dtype=jnp.int32), n_valid[0] - 1)
    tile_expert = jnp.sum((tile_end[None, :] <= tile_id[:, None]).astype(jnp.int32), axis=1)
    tile_in_group = tile_id - (tile_end - tiles)[tile_expert]
    tile_rows = jnp.clip(cnt[tile_expert] - tile_in_group * bm, 0, bm)
    tile_rows = jnp.where(jnp.arange(n_tiles) < n_valid[0], tile_rows, 0).astype(jnp.int32)
    fields = route_t.astype(jnp.int32)
    pos = jnp.stack([offset[fields[ROUTE_E0 + kk]] + fields[ROUTE_R0 + kk] for kk in range(TOP_K)])
    zero_lo = offset + cnt
    zero_hi = jnp.concatenate([offset[1:], jnp.full((1,), n_tiles * bm, jnp.int32)])

    xs = moe_dispatch(h, pos, zero_lo, zero_hi, n_tiles * bm)
    ys = moe_grouped_ffn(xs, tile_expert, tile_rows, n_valid, wg, wu, wd, bm)
    return moe_combine(x, route, pos, ys, norm_w, final_norm)


def _norm_kernel(x_ref, nw_ref, o_ref):
    o_ref[...] = _rms(x_ref[...], nw_ref[...])


def final_norm(x, norm_w, bm=512):
    t, d = x.shape
    bm = min(bm, t)
    return pl.pallas_call(
        _norm_kernel,
        grid=(t // bm,),
        in_specs=[pl.BlockSpec((bm, d), lambda i: (i, 0)), pl.BlockSpec((1, d), lambda i: (0, 0))],
        out_specs=pl.BlockSpec((bm, d), lambda i: (i, 0)),
        out_shape=jax.ShapeDtypeStruct((t, d), F32),
        compiler_params=_cparams(("parallel",)),
        name="final_norm",
    )(x, norm_w.reshape(1, d))


def kernel(x, mix_norm_w, w_in, gmlp_w_s, gmlp_b_s, gmlp_ln_w, gmlp_ln_b, gmlp_out_w, hgrn_lb_logits, hgrn_out_w, w_o, ffn_norm_w, dense_w_gate, dense_w_up, dense_w_down, router_w, expert_w_gate, expert_w_up, expert_w_down, final_norm_w):
    batch, seq, d = x.shape
    depth = w_in.shape[0]
    d_gmlp = gmlp_ln_w.shape[1]
    d_hgrn = hgrn_out_w.shape[1]
    assert d_gmlp == d_hgrn and seq % HGRN_C == 0 and seq % GMLP_CHUNK == 0

    lbs = jax.nn.softmax(hgrn_lb_logits.astype(F32), axis=0)
    lbs = jnp.cumsum(lbs, axis=0) - lbs[0:1]

    xt = x.reshape(batch * seq, d)
    w_in_bf16 = w_in[0].astype(BF16)
    for l in range(depth):
        last = l == depth - 1
        j = l // 2
        soon = [] if last else [(w_in.reshape(depth * d, -1), (l + 1) * d, d)]
        if l % 2 == 0:
            soon += [dense_w_gate[j], dense_w_up[j], dense_w_down[j]]
        z, narrowed = in_proj(xt, mix_norm_w[l], w_in_bf16, cast=tuple(soon))
        narrowed = list(narrowed)
        if not last:
            w_in_bf16 = narrowed.pop(0)
        y_a = gmlp_mixer(z, gmlp_w_s[l], gmlp_b_s[l], gmlp_ln_w[l], gmlp_ln_b[l], gmlp_out_w[l])
        y_b = hgrn2_mixer(z, lbs[l], hgrn_out_w[l], batch, col0=2 * d_gmlp // d_hgrn)
        if l % 2 == 0:
            xt, h = out_proj(xt, y_a, y_b, w_o, l, ffn_norm_w[l])
            experts = () if last else tuple(w[j].reshape(-1, w.shape[-1])
                                            for w in (expert_w_gate, expert_w_up, expert_w_down))
            xt, experts_bf16 = dense_ffn(xt, h, *narrowed, cast=experts)
            if last:
                xt = final_norm(xt, final_norm_w)
        else:
            xt, h, route, route_t, counts = out_proj(xt, y_a, y_b, w_o, l, ffn_norm_w[l], router_w=router_w[j])
            wg, wu, wd = (w.reshape(s.shape[1:]) for w, s in
                          zip(experts_bf16, (expert_w_gate, expert_w_up, expert_w_down)))
            xt = moe_ffn(xt, h, route, route_t, counts, wg, wu, wd, final_norm_w, final_norm=last)
    return xt.reshape(batch, seq, d)
```

```python
import functools

import jax
import jax.numpy as jnp
from jax import lax
from jax.experimental import pallas as pl
from jax.experimental.pallas import tpu as pltpu

F32 = jnp.float32
BF16 = jnp.bfloat16

GMLP_HEAD = 128
GMLP_CHUNK = 128
HGRN_HEAD = 128
N_EXPERTS = 8
TOP_K = 2
EPS = 1e-6
F_MIN = 1e-6

LANES = 128
GMLP_GROUP = 8
HGRN_C = 128
HGRN_GROUP = 8
HGRN_DIAG = 8
HGRN_DIAG_REF = 3
VMEM_LIMIT = 56 * 1024 * 1024
VMEM_LIMIT_IN_PROJ = 60 * 1024 * 1024


def _cparams(sem, vmem=VMEM_LIMIT):
    return pltpu.CompilerParams(dimension_semantics=sem, vmem_limit_bytes=vmem)


def _silu(x):
    hx = 0.5 * x
    return hx + hx * jnp.tanh(hx)


GELU_C1 = 0.7978845608028654
GELU_C2 = GELU_C1 * 0.044715


def _gelu_tanh(x):
    hx = 0.5 * x
    return hx + hx * jnp.tanh(x * (GELU_C1 + GELU_C2 * (x * x)))


def _rms(x, w):
    return x * lax.rsqrt(jnp.mean(x * x, axis=-1, keepdims=True) + EPS) * w


BF16_SUBLANES = 16


def _cast_block_specs(items, ni, nj, nk=1):
    arrays, in_specs, out_specs, out_shapes = [], [], [], []
    for item in items:
        w, row0, r = item if isinstance(item, tuple) else (item, 0, item.shape[0])
        c = w.shape[1]
        if c % nj == 0 and (c // nj) % LANES == 0 and r % (ni * nk * BF16_SUBLANES) == 0:
            block = (r // (ni * nk), c // nj)
            index = lambda i, j, k=0, off=0: (i * nk + k + off, j)
        else:
            assert r % (ni * nj * nk * BF16_SUBLANES) == 0, (w.shape, r, ni, nj, nk)
            block = (r // (ni * nj * nk), c)
            index = lambda i, j, k=0, off=0: ((i * nj + j) * nk + k + off, 0)
        assert row0 % block[0] == 0
        arrays.append(w)
        in_specs.append(pl.BlockSpec(block, functools.partial(index, off=row0 // block[0])))
        out_specs.append(pl.BlockSpec(block, index))
        out_shapes.append(jax.ShapeDtypeStruct((r, c), BF16))
    return arrays, in_specs, out_specs, out_shapes


def _narrow_slabs(src_refs, dst_refs):
    for src_ref, dst_ref in zip(src_refs, dst_refs):
        dst_ref[...] = src_ref[...].astype(BF16)


def _inproj_kernel(x_ref, nw_ref, w_ref, *rest):
    n_cast = (len(rest) - 2) // 2
    z_ref, h_scr = rest[n_cast], rest[-1]

    @pl.when(pl.program_id(1) == 0)
    def _():
        h_scr[...] = _rms(x_ref[...], nw_ref[...]).astype(BF16)

    _narrow_slabs(rest[:n_cast], rest[n_cast + 1:-1])
    z_ref[...] = jnp.dot(h_scr[...], w_ref[...], preferred_element_type=F32)


def in_proj(x, norm_w, w, cast=(), bm=1024, bn=1536):
    t, d = x.shape
    n = w.shape[1]
    bm = min(bm, t)
    ni, nj = t // bm, n // bn
    cast_arrays, cast_in, cast_out, cast_shapes = _cast_block_specs(cast, ni, nj)
    outs = pl.pallas_call(
        _inproj_kernel,
        grid=(ni, nj),
        in_specs=[
            pl.BlockSpec((bm, d), lambda i, j: (i, 0)),
            pl.BlockSpec((1, d), lambda i, j: (0, 0)),
            pl.BlockSpec((d, bn), lambda i, j: (0, j)),
        ] + cast_in,
        out_specs=[pl.BlockSpec((bm, bn), lambda i, j: (i, j))] + cast_out,
        out_shape=[jax.ShapeDtypeStruct((t, n), F32)] + cast_shapes,
        scratch_shapes=[pltpu.VMEM((bm, d), BF16)],
        compiler_params=_cparams(("parallel", "arbitrary"), VMEM_LIMIT_IN_PROJ),
        name="in_proj",
    )(x, norm_w.reshape(1, d), w, *cast_arrays)
    return outs[0], tuple(outs[1:])


def _gmlp_kernel(zu_ref, zv_ref, ws_ref, bs_ref, lnw_ref, lnb_ref, ow_ref, y_ref, wc_scr, *, n_heads, n_chunks):
    c = GMLP_CHUNK
    row = lax.broadcasted_iota(jnp.int32, (c, c), 0)
    col = lax.broadcasted_iota(jnp.int32, (c, c), 1)
    causal = row >= col
    ones = jnp.ones((GMLP_HEAD, GMLP_HEAD), BF16)
    lane_mean = lambda x: jnp.dot(x.astype(BF16), ones, preferred_element_type=F32) * (1.0 / GMLP_HEAD)

    for h in range(n_heads):
        wc_scr[h] = jnp.where(causal, ws_ref[h], 0.0).astype(BF16)

    def chunk(ci, carry):
        r0 = pl.multiple_of(ci * c, c)
        for g0 in range(0, n_heads, GMLP_GROUP):
            heads = range(g0, g0 + GMLP_GROUP)
            sls = [slice(h * GMLP_HEAD, (h + 1) * GMLP_HEAD) for h in heads]
            v = [_gelu_tanh(zv_ref[pl.ds(r0, c), sl]) for sl in sls]
            mu = [lane_mean(x) for x in v]
            vc = [x - m for x, m in zip(v, mu)]
            var = [lane_mean(x * x) for x in vc]
            vn = [(x * lax.rsqrt(s + EPS) * lnw_ref[:, sl] + lnb_ref[:, sl]).astype(BF16)
                  for x, s, sl in zip(vc, var, sls)]
            sv = [jnp.dot(wc_scr[h], x, preferred_element_type=F32) + bs_ref[:, h:h + 1] for h, x in zip(heads, vn)]
            y = [_gelu_tanh(zu_ref[pl.ds(r0, c), sl]) * x for sl, x in zip(sls, sv)]
            ms = [lane_mean(x * x) for x in y]
            for x, s, sl in zip(y, ms, sls):
                y_ref[pl.ds(r0, c), sl] = (x * lax.rsqrt(s + EPS) * ow_ref[:, sl]).astype(BF16)
        return carry

    lax.fori_loop(0, n_chunks, chunk, 0)


def gmlp_mixer(z, w_s, b_s, ln_w, ln_b, out_w, rows=1024):
    t = z.shape[0]
    n_heads = w_s.shape[0]
    dg = n_heads * GMLP_HEAD
    rows = min(rows, t)
    kern = functools.partial(_gmlp_kernel, n_heads=n_heads, n_chunks=rows // GMLP_CHUNK)
    vec = lambda a: a.reshape(1, dg)
    return pl.pallas_call(
        kern,
        grid=(t // rows,),
        in_specs=[
            pl.BlockSpec((rows, dg), lambda i: (i, 0)),
            pl.BlockSpec((rows, dg), lambda i: (i, 1)),
            pl.BlockSpec((n_heads, GMLP_CHUNK, GMLP_CHUNK), lambda i: (0, 0, 0)),
            pl.BlockSpec((GMLP_CHUNK, n_heads), lambda i: (0, 0)),
            pl.BlockSpec((1, dg), lambda i: (0, 0)),
            pl.BlockSpec((1, dg), lambda i: (0, 0)),
            pl.BlockSpec((1, dg), lambda i: (0, 0)),
        ],
        out_specs=pl.BlockSpec((rows, dg), lambda i: (i, 0)),
        out_shape=jax.ShapeDtypeStruct((t, dg), BF16),
        scratch_shapes=[pltpu.VMEM((n_heads, GMLP_CHUNK, GMLP_CHUNK), BF16)],
        compiler_params=_cparams(("parallel",)),
        name="gmlp_mixer",
    )(z, z, w_s, b_s.T, vec(ln_w), vec(ln_b), vec(out_w))


def _hgrn_level_terms(a, q, k, blk):
    c, d = a.shape
    half = blk // 2
    nb = c // blk
    lower = lambda v: v.reshape(nb, blk, d)[:, :half, :]
    upper = lambda v: v.reshape(nb, blk, d)[:, half:, :]
    ref = a.reshape(nb, blk, d)[:, half - 1:half, :]
    qh = upper(q) * jnp.exp2(upper(a) - ref)
    kh = lower(k) * jnp.exp2(ref - lower(a))
    zeros = jnp.zeros_like(qh)
    qh = jnp.concatenate([zeros, qh], axis=1).reshape(c, d)
    kh = jnp.concatenate([kh, zeros], axis=1).reshape(c, d)
    return qh.astype(BF16), kh.astype(BF16)


def _hgrn_kernel(zq_ref, zf_ref, zi_ref, zg_ref, lb_ref, ow_ref, y_ref, st_ref, *, n_heads, n_chunks):
    c, d = HGRN_C, HGRN_HEAD
    nt = (((1,), (1,)), ((), ()))
    tn = (((0,), (0,)), ((), ()))

    @pl.when(pl.program_id(1) == 0)
    def _():
        st_ref[...] = jnp.zeros_like(st_ref)

    row = lax.broadcasted_iota(jnp.int32, (c, c), 0)
    col = lax.broadcasted_iota(jnp.int32, (c, c), 1)
    tril = (row >= col).astype(BF16)
    diag_mask = (row >= col) & ((row // HGRN_DIAG) == (col // HGRN_DIAG))
    level_masks = []
    blk = 2 * HGRN_DIAG
    while blk <= c:
        level_masks.append((blk, None if blk == c else (row // blk) == (col // blk)))
        blk *= 2

    def nt_dot(x, y):
        return lax.dot_general(x, y, nt, preferred_element_type=F32)

    def chunk(ci, carry):
        r0 = pl.multiple_of(ci * c, c)
        for g0 in range(0, n_heads, HGRN_GROUP):
            heads = list(range(g0, g0 + HGRN_GROUP))
            sls = [slice(h * d, (h + 1) * d) for h in heads]
            f = [0.5 * (1.0 + lb_ref[:, sl]) + (0.5 * (1.0 - lb_ref[:, sl])) * jnp.tanh(0.5 * zf_ref[pl.ds(r0, c), sl])
                 for sl in sls]
            logf = [jnp.log2(jnp.clip(x, F_MIN, 1.0)) for x in f]
            pieces = [jnp.concatenate(_split_bf16(x), axis=1) for x in logf]
            cs = [jnp.dot(tril, x, preferred_element_type=F32) for x in pieces]
            a = [x[:, :d] + x[:, d:] for x in cs]
            k = [1.0 - x for x in f]
            q = [_silu(zq_ref[pl.ds(r0, c), sl]) for sl in sls]
            iv = [zi_ref[pl.ds(r0, c), sl].astype(BF16) for sl in sls]

            dd = []
            for x in a:
                a3 = x.reshape(c // HGRN_DIAG, HGRN_DIAG, d)
                dd.append((a3 - a3[:, HGRN_DIAG_REF:HGRN_DIAG_REF + 1, :]).reshape(c, d))
            sd = [nt_dot((qx * jnp.exp2(x)).astype(BF16), (kx * jnp.exp2(-x)).astype(BF16))
                  for qx, kx, x in zip(q, k, dd)]

            scores = None
            for blk, same_block in reversed(level_masks):
                terms = [_hgrn_level_terms(ax, qx, kx, blk) for ax, qx, kx in zip(a, q, k)]
                prods = [nt_dot(qh, kh) for qh, kh in terms]
                scores = prods if scores is None else [jnp.where(same_block, p, s) for s, p in zip(scores, prods)]
            scores = [jnp.where(diag_mask, x, s) for x, s in zip(sd, scores)]

            st = [st_ref[h] for h in heads]
            qe = [(qx * jnp.exp2(ax)).astype(BF16) for qx, ax in zip(q, a)]
            o = [jnp.dot(s.astype(BF16), ivx, preferred_element_type=F32) + nt_dot(qx, sx.astype(BF16))
                 for s, ivx, qx, sx in zip(scores, iv, qe, st)]
            kd = [(kx * jnp.exp2(ax[c - 1:c, :] - ax)).astype(BF16) for kx, ax in zip(k, a)]
            for h, ax, sx, ivx, kx in zip(heads, a, st, iv, kd):
                st_ref[h] = jnp.exp2(ax[c - 1:c, :]) * sx + lax.dot_general(ivx, kx, tn, preferred_element_type=F32)
            for x, sl in zip(o, sls):
                y_ref[pl.ds(r0, c), sl] = (_rms(x, ow_ref[:, sl]) * _silu(zg_ref[pl.ds(r0, c), sl])).astype(BF16)
        return carry

    lax.fori_loop(0, n_chunks, chunk, 0)


def hgrn2_mixer(z, lb, out_w, batch, col0, rows=1024):
    t = z.shape[0]
    s = t // batch
    dh = lb.shape[0]
    n_heads = dh // HGRN_HEAD
    rows = min(rows, s)
    spb = s // rows
    kern = functools.partial(_hgrn_kernel, n_heads=n_heads, n_chunks=rows // HGRN_C)
    zspec = lambda j: pl.BlockSpec((rows, dh), lambda b, i, j=j: (b * spb + i, col0 + j))
    return pl.pallas_call(
        kern,
        grid=(batch, spb),
        in_specs=[zspec(0), zspec(1), zspec(2), zspec(3),
                  pl.BlockSpec((1, dh), lambda b, i: (0, 0)),
                  pl.BlockSpec((1, dh), lambda b, i: (0, 0))],
        out_specs=pl.BlockSpec((rows, dh), lambda b, i: (b * spb + i, 0)),
        out_shape=jax.ShapeDtypeStruct((t, dh), BF16),
        scratch_shapes=[pltpu.VMEM((n_heads, HGRN_HEAD, HGRN_HEAD), F32)],
        compiler_params=_cparams(("parallel", "arbitrary")),
        name="hgrn2_mixer",
    )(z, z, z, z, lb.reshape(1, dh), out_w.reshape(1, dh))


ROUTE_E0, ROUTE_E1, ROUTE_W0, ROUTE_W1, ROUTE_R0, ROUTE_R1 = range(6)
ROUTE_FIELDS = 8


def _split_bf16(v):
    hi = v.astype(BF16)
    return hi, (v - hi.astype(F32)).astype(BF16)


def _route_rows(h, rw_pieces, carry_ref):
    bm = h.shape[0]
    h_hi, h_lo = _split_bf16(h)
    hh_hl = jnp.dot(h_hi, rw_pieces, preferred_element_type=F32)
    logits = hh_hl[:, :LANES] + hh_hl[:, LANES:] + jnp.dot(h_lo, rw_pieces[:, :LANES], preferred_element_type=F32)
    lane = lax.broadcasted_iota(jnp.int32, (bm, LANES), 1)
    neg = jnp.float32(-jnp.inf)
    logits = jnp.where(lane < N_EXPERTS, logits, neg)
    v0 = jnp.max(logits, axis=-1, keepdims=True)
    e0 = jnp.min(jnp.where(logits == v0, lane, LANES), axis=-1, keepdims=True)
    rest = jnp.where(lane == e0, neg, logits)
    v1 = jnp.max(rest, axis=-1, keepdims=True)
    e1 = jnp.min(jnp.where(rest == v1, lane, LANES), axis=-1, keepdims=True)
    ex = jnp.exp(v1 - v0)
    w0 = 1.0 / (1.0 + ex)
    w1 = ex / (1.0 + ex)

    onehot = ((lane == e0) | (lane == e1)).astype(F32)
    row = lax.broadcasted_iota(jnp.int32, (bm, bm), 0)
    col = lax.broadcasted_iota(jnp.int32, (bm, bm), 1)
    before = (row > col).astype(BF16)
    excl = jnp.dot(before, onehot.astype(BF16), preferred_element_type=F32) + carry_ref[...]
    r0 = jnp.sum(jnp.where(lane == e0, excl, 0.0), axis=-1, keepdims=True)
    r1 = jnp.sum(jnp.where(lane == e1, excl, 0.0), axis=-1, keepdims=True)
    carry_ref[...] += jnp.sum(onehot, axis=0, keepdims=True)

    out = jnp.zeros((bm, LANES), F32)
    for idx, val in ((ROUTE_E0, e0.astype(F32)), (ROUTE_E1, e1.astype(F32)), (ROUTE_W0, w0), (ROUTE_W1, w1),
                     (ROUTE_R0, r0), (ROUTE_R1, r1)):
        out = jnp.where(lane == idx, val, out)
    return out


def _outproj_kernel(*refs, da, moe):
    if moe:
        (x_ref, ya_ref, yb_ref, wo_ref, nw_ref, rw_ref, xo_ref, h_ref, route_ref, route_t_ref, cnt_ref,
         wo_scr, rw_scr, carry_scr) = refs
    else:
        x_ref, ya_ref, yb_ref, wo_ref, nw_ref, xo_ref, h_ref, wo_scr = refs

    @pl.when(pl.program_id(0) == 0)
    def _():
        wo_scr[...] = wo_ref[...].astype(BF16)
        if moe:
            rw_hi, rw_lo = _split_bf16(rw_ref[...])
            rw_scr[:, :LANES] = rw_hi
            rw_scr[:, LANES:] = rw_lo
            carry_scr[...] = jnp.zeros_like(carry_scr)

    acc = x_ref[...] + jnp.dot(ya_ref[...], wo_scr[:da, :], preferred_element_type=F32) \
        + jnp.dot(yb_ref[...], wo_scr[da:, :], preferred_element_type=F32)
    xo_ref[...] = acc
    h = _rms(acc, nw_ref[...])
    h_ref[...] = h.astype(h_ref.dtype)
    if moe:
        route = _route_rows(h, rw_scr[...], carry_scr)
        route_ref[...] = route
        route_t_ref[...] = route.T[:route_t_ref.shape[0], :]
        cnt_ref[...] = carry_scr[...]


def out_proj(x, y_a, y_b, w_o, layer, norm_w, router_w=None, bm=None):
    t, d = x.shape
    da, db = y_a.shape[1], y_b.shape[1]
    moe = router_w is not None
    bm = min(bm or (256 if moe else 512), t)
    row_spec = lambda w: pl.BlockSpec((bm, w), lambda i: (i, 0))
    const = lambda shape: pl.BlockSpec(shape, lambda i: (0,) * len(shape), pipeline_mode=pl.Buffered(1))
    w_spec = pl.BlockSpec((None, da + db, d), lambda i: (layer, 0, 0), pipeline_mode=pl.Buffered(1))
    in_specs = [row_spec(d), row_spec(da), row_spec(db), w_spec, const((1, d))]
    out_specs = [row_spec(d), row_spec(d)]
    out_shape = [jax.ShapeDtypeStruct((t, d), F32), jax.ShapeDtypeStruct((t, d), F32 if moe else BF16)]
    scratch = [pltpu.VMEM((da + db, d), BF16)]
    args = [x, y_a, y_b, w_o, norm_w.reshape(1, d)]
    if moe:
        in_specs.append(const((d, LANES)))
        args.append(jnp.zeros((d, LANES), F32).at[:, :N_EXPERTS].set(router_w))
        out_specs += [row_spec(LANES), pl.BlockSpec((ROUTE_FIELDS, bm), lambda i: (0, i)),
                      pl.BlockSpec((1, LANES), lambda i: (0, 0))]
        out_shape += [jax.ShapeDtypeStruct((t, LANES), F32), jax.ShapeDtypeStruct((ROUTE_FIELDS, t), F32),
                      jax.ShapeDtypeStruct((1, LANES), F32)]
        scratch += [pltpu.VMEM((d, 2 * LANES), BF16), pltpu.VMEM((1, LANES), F32)]
    return pl.pallas_call(
        functools.partial(_outproj_kernel, da=da, moe=moe),
        grid=(t // bm,),
        in_specs=in_specs,
        out_specs=out_specs,
        out_shape=out_shape,
        scratch_shapes=scratch,
        compiler_params=_cparams(("arbitrary",)),
        name="out_proj_route" if moe else "out_proj",
    )(*args)


def _ffn_kernel(x_hbm, h_ref, wg_ref, wu_ref, wd_ref, *rest, n_pass):
    n_cast = (len(rest) - 2) // 2
    o_ref, sem = rest[n_cast], rest[-1]
    i, j = pl.program_id(0), pl.program_id(1)
    bm = o_ref.shape[0]
    sub = bm // n_pass

    def x_copy(p):
        src = x_hbm.at[pl.ds(pl.multiple_of(i * bm + p * sub, sub), sub)]
        return pltpu.make_async_copy(src, o_ref.at[pl.ds(p * sub, sub)], sem.at[p])

    @pl.when(j == 0)
    def _():
        for p in range(n_pass):
            x_copy(p).start()
        for p in range(n_pass):
            x_copy(p).wait()

    _narrow_slabs(rest[:n_cast], rest[n_cast + 1:-1])

    for p in range(n_pass):
        hb = h_ref[p * sub:(p + 1) * sub, :]
        g = jnp.dot(hb, wg_ref[...], preferred_element_type=F32)
        u = jnp.dot(hb, wu_ref[...], preferred_element_type=F32)
        act = (_silu(g) * u).astype(BF16)
        o_ref[p * sub:(p + 1) * sub, :] += jnp.dot(act, wd_ref[...], preferred_element_type=F32)


def dense_ffn(x, h, wg, wu, wd, cast=(), bm=1024, bf=512, n_pass=2):
    t, d = x.shape
    f = wg.shape[1]
    bm = min(bm, t)
    ni, nj = t // bm, f // bf
    cast_arrays, cast_in, cast_out, cast_shapes = _cast_block_specs(cast, ni, nj)
    outs = pl.pallas_call(
        functools.partial(_ffn_kernel, n_pass=n_pass),
        grid=(ni, nj),
        in_specs=[
            pl.BlockSpec(memory_space=pl.ANY),
            pl.BlockSpec((bm, d), lambda i, j: (i, 0), pipeline_mode=pl.Buffered(1)),
            pl.BlockSpec((d, bf), lambda i, j: (0, j)),
            pl.BlockSpec((d, bf), lambda i, j: (0, j)),
            pl.BlockSpec((bf, d), lambda i, j: (j, 0)),
        ] + cast_in,
        out_specs=[pl.BlockSpec((bm, d), lambda i, j: (i, 0))] + cast_out,
        out_shape=[jax.ShapeDtypeStruct((t, d), F32)] + cast_shapes,
        scratch_shapes=[pltpu.SemaphoreType.DMA((n_pass,))],
        compiler_params=_cparams(("parallel", "arbitrary"), VMEM_LIMIT_IN_PROJ),
        name="dense_ffn",
    )(x, h, wg, wu, wd, *cast_arrays)
    return outs[0], tuple(outs[1:])


def _row_copy(src_ref, dst_ref, sem, src_row, dst_row):
    return pltpu.make_async_copy(src_ref.at[pl.ds(src_row, 1)], dst_ref.at[pl.ds(dst_row, 1)], sem)


ROW_DMA_UNROLL = 8
GMM_TILE = 1024
GMM_SUBTILES = 2


def _dispatch_kernel(zlo_ref, zhi_ref, pos_ref, h_ref, xs_ref, zero_scr, sem, zsem):
    bt = h_ref.shape[0]

    @pl.when(pl.program_id(0) == 0)
    def _():
        zero_scr[...] = jnp.zeros_like(zero_scr)
        zr = zero_scr.shape[0]

        def zero_rows(e, begin):
            lo, hi = zlo_ref[e], zhi_ref[e]
            mid = jnp.minimum(hi, (lo + zr - 1) // zr * zr)

            def one(r, carry):
                cp = _row_copy(zero_scr, xs_ref, zsem, 0, r)
                cp.start() if begin else cp.wait()
                return carry

            def slab(b, carry):
                cp = pltpu.make_async_copy(zero_scr, xs_ref.at[pl.ds(pl.multiple_of(b * zr, zr), zr)], zsem)
                cp.start() if begin else cp.wait()
                return carry

            lax.fori_loop(lo, mid, one, 0)
            lax.fori_loop(mid // zr, hi // zr, slab, 0)

        for e in range(N_EXPERTS):
            zero_rows(e, True)
        for e in range(N_EXPERTS):
            zero_rows(e, False)

    def start(r, carry):
        for kk in range(TOP_K):
            _row_copy(h_ref, xs_ref, sem, r, pos_ref[0, kk * bt + r]).start(priority=kk % 2)
        return carry

    lax.fori_loop(0, bt, start, 0, unroll=ROW_DMA_UNROLL)
    for kk in range(TOP_K):
        pltpu.make_async_copy(h_ref, xs_ref.at[pl.ds(0, bt)], sem).wait()


def _tile_positions(pos, bt):
    t = pos.shape[1]
    return pos.reshape(TOP_K, t // bt, bt).transpose(1, 0, 2).reshape(t // bt, 1, TOP_K * bt)


def moe_dispatch(h, pos, zero_lo, zero_hi, n_rows, bt=512):
    t, d = h.shape
    bt = min(bt, t)
    pos2 = _tile_positions(pos, bt)
    return pl.pallas_call(
        _dispatch_kernel,
        grid_spec=pltpu.PrefetchScalarGridSpec(
            num_scalar_prefetch=2,
            grid=(t // bt,),
            in_specs=[
                pl.BlockSpec((None, 1, TOP_K * bt), lambda i, zl, zh: (i, 0, 0), memory_space=pltpu.SMEM),
                pl.BlockSpec((bt, d), lambda i, zl, zh: (i, 0)),
            ],
            out_specs=pl.BlockSpec(memory_space=pl.ANY),
            scratch_shapes=[pltpu.VMEM((8, d), F32), pltpu.SemaphoreType.DMA(()), pltpu.SemaphoreType.DMA(())],
        ),
        out_shape=jax.ShapeDtypeStruct((n_rows, d), F32),
        compiler_params=_cparams(("arbitrary",)),
        name="moe_dispatch",
    )(zero_lo, zero_hi, pos2, h)


def _gmm_kernel(te_ref, tr_ref, nv_ref, xs_ref, wg_ref, wu_ref, wd_ref, ys_ref, xb_scr):
    i, j = pl.program_id(0), pl.program_id(1)
    bm = xs_ref.shape[0]
    rows = tr_ref[i]
    sub = bm // GMM_SUBTILES

    @pl.when(j == 0)
    def _():
        xb_scr[...] = xs_ref[...].astype(BF16)
        ys_ref[...] = jnp.zeros_like(ys_ref)

    def ffn(r0, n):
        xb = xb_scr[r0:r0 + n, :]
        g = jnp.dot(xb, wg_ref[...], preferred_element_type=F32)
        u = jnp.dot(xb, wu_ref[...], preferred_element_type=F32)
        act = (_silu(g) * u).astype(BF16)
        ys_ref[r0:r0 + n, :] += jnp.dot(act, wd_ref[...], preferred_element_type=F32)

    full = rows == bm

    @pl.when(full)
    def _():
        for s in range(GMM_SUBTILES):
            ffn(s * sub, sub)

    for s in range(GMM_SUBTILES):
        left = rows - s * sub

        @pl.when(jnp.logical_not(full) & (left > sub // 2))
        def _():
            ffn(s * sub, sub)

        @pl.when((left > 0) & (left <= sub // 2))
        def _():
            ffn(s * sub, sub // 2)


def moe_grouped_ffn(xs, tile_expert, tile_rows, n_valid, wg, wu, wd, bm, bf=512):
    p, d = xs.shape
    f = wg.shape[2]
    nf = f // bf

    def row_map(i, j, te, tr, nv):
        return (jnp.minimum(i, nv[0] - 1), 0)

    def fcol(i, j, nv):
        return jnp.where(i < nv[0], j, nf - 1)

    return pl.pallas_call(
        _gmm_kernel,
        grid_spec=pltpu.PrefetchScalarGridSpec(
            num_scalar_prefetch=3,
            grid=(p // bm, nf),
            in_specs=[
                pl.BlockSpec((bm, d), row_map),
                pl.BlockSpec((None, d, bf), lambda i, j, te, tr, nv: (te[i], 0, fcol(i, j, nv))),
                pl.BlockSpec((None, d, bf), lambda i, j, te, tr, nv: (te[i], 0, fcol(i, j, nv))),
                pl.BlockSpec((None, bf, d), lambda i, j, te, tr, nv: (te[i], fcol(i, j, nv), 0)),
            ],
            out_specs=pl.BlockSpec((bm, d), lambda i, j, te, tr, nv: (i, 0)),
            scratch_shapes=[pltpu.VMEM((bm, d), BF16)],
        ),
        out_shape=jax.ShapeDtypeStruct((p, d), F32),
        compiler_params=_cparams(("arbitrary", "arbitrary")),
        name="moe_grouped_ffn",
    )(tile_expert, tile_rows, n_valid, xs, wg, wu, wd)


def _combine_kernel(pos_ref, pos_next_ref, x_ref, route_ref, nw_ref, ys_ref, o_ref, y_scr, sem, *, final_norm):
    bt = x_ref.shape[0]
    i = pl.program_id(0)
    slot = i % 2

    def gather(p_ref, s, unroll):
        def start(r, carry):
            for kk in range(TOP_K):
                _row_copy(ys_ref, y_scr.at[s, kk], sem.at[s], p_ref[0, kk * bt + r], r).start(priority=kk % 2)
            return carry
        lax.fori_loop(0, bt, start, 0, unroll=unroll)

    def wait(s):
        for kk in range(TOP_K):
            pltpu.make_async_copy(ys_ref.at[pl.ds(0, bt)], y_scr.at[s, kk], sem.at[s]).wait()

    @pl.when(i == 0)
    def _():
        gather(pos_ref, 0, ROW_DMA_UNROLL)

    wait(slot)
    gather(pos_next_ref, 1 - slot, True)
    w0 = route_ref[:, ROUTE_W0:ROUTE_W0 + 1]
    w1 = route_ref[:, ROUTE_W1:ROUTE_W1 + 1]
    out = x_ref[...] + (w0 * y_scr[slot, 0] + w1 * y_scr[slot, 1])
    if final_norm:
        out = _rms(out, nw_ref[...])
    o_ref[...] = out

    @pl.when(i + 1 == pl.num_programs(0))
    def _():
        wait(1 - slot)


def moe_combine(x, route, pos, ys, norm_w, final_norm, bt=256):
    t, d = x.shape
    bt = min(bt, t)
    n = t // bt
    pos2 = _tile_positions(pos, bt)
    pos_spec = lambda nxt: pl.BlockSpec((None, 1, TOP_K * bt), lambda i: (jnp.minimum(i + nxt, n - 1), 0, 0),
                                        memory_space=pltpu.SMEM)
    return pl.pallas_call(
        functools.partial(_combine_kernel, final_norm=final_norm),
        grid=(n,),
        in_specs=[
            pos_spec(0),
            pos_spec(1),
            pl.BlockSpec((bt, d), lambda i: (i, 0)),
            pl.BlockSpec((bt, LANES), lambda i: (i, 0)),
            pl.BlockSpec((1, d), lambda i: (0, 0)),
            pl.BlockSpec(memory_space=pl.ANY),
        ],
        out_specs=pl.BlockSpec((bt, d), lambda i: (i, 0)),
        out_shape=jax.ShapeDtypeStruct((t, d), F32),
        scratch_shapes=[pltpu.VMEM((2, TOP_K, bt, d), F32), pltpu.SemaphoreType.DMA((2,))],
        compiler_params=_cparams(("arbitrary",)),
        name="moe_combine",
    )(pos2, pos2, x, route, norm_w.reshape(1, d), ys)


def moe_ffn(x, h, route, route_t, counts, wg, wu, wd, norm_w, final_norm, bm=GMM_TILE):
    t, d = x.shape

    cnt = counts[0, :N_EXPERTS].astype(jnp.int32)
    tiles = (cnt + bm - 1) // bm
    tile_end = jnp.cumsum(tiles)
    offset = (tile_end - tiles) * bm
    n_tiles = (TOP_K * t) // bm + N_EXPERTS
    n_valid = tile_end[-1:].astype(jnp.int32)
    tile_id = jnp.minimum(jnp.arange(n_tiles, dtype=jnp.int32), n_valid[0] - 1)
    tile_expert = jnp.sum((tile_end[None, :] <= tile_id[:, None]).astype(jnp.int32), axis=1)
    tile_in_group = tile_id - (tile_end - tiles)[tile_expert]
    tile_rows = jnp.clip(cnt[tile_expert] - tile_in_group * bm, 0, bm)
    tile_rows = jnp.where(jnp.arange(n_tiles) < n_valid[0], tile_rows, 0).astype(jnp.int32)
    fields = route_t.astype(jnp.int32)
    pos = jnp.stack([offset[fields[ROUTE_E0 + kk]] + fields[ROUTE_R0 + kk] for kk in range(TOP_K)])
    zero_lo = offset + cnt
    zero_hi = jnp.concatenate([offset[1:], jnp.full((1,), n_tiles * bm, jnp.int32)])

    xs = moe_dispatch(h, pos, zero_lo, zero_hi, n_tiles * bm)
    ys = moe_grouped_ffn(xs, tile_expert, tile_rows, n_valid, wg, wu, wd, bm)
    return moe_combine(x, route, pos, ys, norm_w, final_norm)


def _norm_kernel(x_ref, nw_ref, o_ref):
    o_ref[...] = _rms(x_ref[...], nw_ref[...])


def final_norm(x, norm_w, bm=512):
    t, d = x.shape
    bm = min(bm, t)
    return pl.pallas_call(
        _norm_kernel,
        grid=(t // bm,),
        in_specs=[pl.BlockSpec((bm, d), lambda i: (i, 0)), pl.BlockSpec((1, d), lambda i: (0, 0))],
        out_specs=pl.BlockSpec((bm, d), lambda i: (i, 0)),
        out_shape=jax.ShapeDtypeStruct((t, d), F32),
        compiler_params=_cparams(("parallel",)),
        name="final_norm",
    )(x, norm_w.reshape(1, d))


def kernel(x, mix_norm_w, w_in, gmlp_w_s, gmlp_b_s, gmlp_ln_w, gmlp_ln_b, gmlp_out_w, hgrn_lb_logits, hgrn_out_w, w_o, ffn_norm_w, dense_w_gate, dense_w_up, dense_w_down, router_w, expert_w_gate, expert_w_up, expert_w_down, final_norm_w):
    batch, seq, d = x.shape
    depth = w_in.shape[0]
    d_gmlp = gmlp_ln_w.shape[1]
    d_hgrn = hgrn_out_w.shape[1]
    assert d_gmlp == d_hgrn and seq % HGRN_C == 0 and seq % GMLP_CHUNK == 0

    lbs = jax.nn.softmax(hgrn_lb_logits.astype(F32), axis=0)
    lbs = jnp.cumsum(lbs, axis=0) - lbs[0:1]

    xt = x.reshape(batch * seq, d)
    w_in_bf16 = w_in[0].astype(BF16)
    for l in range(depth):
        last = l == depth - 1
        j = l // 2
        soon = [] if last else [(w_in.reshape(depth * d, -1), (l + 1) * d, d)]
        if l % 2 == 0:
            soon += [dense_w_gate[j], dense_w_up[j], dense_w_down[j]]
        z, narrowed = in_proj(xt, mix_norm_w[l], w_in_bf16, cast=tuple(soon))
        narrowed = list(narrowed)
        if not last:
            w_in_bf16 = narrowed.pop(0)
        y_a = gmlp_mixer(z, gmlp_w_s[l], gmlp_b_s[l], gmlp_ln_w[l], gmlp_ln_b[l], gmlp_out_w[l])
        y_b = hgrn2_mixer(z, lbs[l], hgrn_out_w[l], batch, col0=2 * d_gmlp // d_hgrn)
        if l % 2 == 0:
            xt, h = out_proj(xt, y_a, y_b, w_o, l, ffn_norm_w[l])
            experts = () if last else tuple(w[j].reshape(-1, w.shape[-1])
                                            for w in (expert_w_gate, expert_w_up, expert_w_down))
            xt, experts_bf16 = dense_ffn(xt, h, *narrowed, cast=experts)
            if last:
                xt = final_norm(xt, final_norm_w)
        else:
            xt, h, route, route_t, counts = out_proj(xt, y_a, y_b, w_o, l, ffn_norm_w[l], router_w=router_w[j])
            wg, wu, wd = (w.reshape(s.shape[1:]) for w, s in
                          zip(experts_bf16, (expert_w_gate, expert_w_up, expert_w_down)))
            xt = moe_ffn(xt, h, route, route_t, counts, wg, wu, wd, final_norm_w, final_norm=last)
    return xt.reshape(batch, seq, d)
```

```python
import functools

import jax
import jax.numpy as jnp
from jax import lax
from jax.experimental import pallas as pl
from jax.experimental.pallas import tpu as pltpu

F32 = jnp.float32
BF16 = jnp.bfloat16

GMLP_HEAD = 128
GMLP_CHUNK = 128
HGRN_HEAD = 128
N_EXPERTS = 8
TOP_K = 2
EPS = 1e-6
F_MIN = 1e-6

LANES = 128
GMLP_GROUP = 8
HGRN_C = 128
HGRN_GROUP = 8
HGRN_DIAG = 8
HGRN_DIAG_REF = 3
VMEM_LIMIT = 56 * 1024 * 1024
VMEM_LIMIT_IN_PROJ = 60 * 1024 * 1024


def _cparams(sem, vmem=VMEM_LIMIT):
    return pltpu.CompilerParams(dimension_semantics=sem, vmem_limit_bytes=vmem)


def _silu(x):
    hx = 0.5 * x
    return hx + hx * jnp.tanh(hx)


GELU_C1 = 0.7978845608028654
GELU_C2 = GELU_C1 * 0.044715


def _gelu_tanh(x):
    hx = 0.5 * x
    return hx + hx * jnp.tanh(x * (GELU_C1 + GELU_C2 * (x * x)))


def _rms(x, w):
    return x * lax.rsqrt(jnp.mean(x * x, axis=-1, keepdims=True) + EPS) * w


BF16_SUBLANES = 16


def _cast_block_specs(items, ni, nj, nk=1):
    arrays, in_specs, out_specs, out_shapes = [], [], [], []
    for item in items:
        w, row0, r = item if isinstance(item, tuple) else (item, 0, item.shape[0])
        c = w.shape[1]
        if c % nj == 0 and (c // nj) % LANES == 0 and r % (ni * nk * BF16_SUBLANES) == 0:
            block = (r // (ni * nk), c // nj)
            index = lambda i, j, k=0, off=0: (i * nk + k + off, j)
        else:
            assert r % (ni * nj * nk * BF16_SUBLANES) == 0, (w.shape, r, ni, nj, nk)
            block = (r // (ni * nj * nk), c)
            index = lambda i, j, k=0, off=0: ((i * nj + j) * nk + k + off, 0)
        assert row0 % block[0] == 0
        arrays.append(w)
        in_specs.append(pl.BlockSpec(block, functools.partial(index, off=row0 // block[0])))
        out_specs.append(pl.BlockSpec(block, index))
        out_shapes.append(jax.ShapeDtypeStruct((r, c), BF16))
    return arrays, in_specs, out_specs, out_shapes


def _narrow_slabs(src_refs, dst_refs):
    for src_ref, dst_ref in zip(src_refs, dst_refs):
        dst_ref[...] = src_ref[...].astype(BF16)


def _inproj_kernel(x_ref, nw_ref, w_ref, *rest):
    n_cast = (len(rest) - 2) // 2
    z_ref, h_scr = rest[n_cast], rest[-1]

    @pl.when(pl.program_id(1) == 0)
    def _():
        h_scr[...] = _rms(x_ref[...], nw_ref[...]).astype(BF16)

    _narrow_slabs(rest[:n_cast], rest[n_cast + 1:-1])
    z_ref[...] = jnp.dot(h_scr[...], w_ref[...], preferred_element_type=F32)


def in_proj(x, norm_w, w, cast=(), bm=1024, bn=1536):
    t, d = x.shape
    n = w.shape[1]
    bm = min(bm, t)
    ni, nj = t // bm, n // bn
    cast_arrays, cast_in, cast_out, cast_shapes = _cast_block_specs(cast, ni, nj)
    outs = pl.pallas_call(
        _inproj_kernel,
        grid=(ni, nj),
        in_specs=[
            pl.BlockSpec((bm, d), lambda i, j: (i, 0)),
            pl.BlockSpec((1, d), lambda i, j: (0, 0)),
            pl.BlockSpec((d, bn), lambda i, j: (0, j)),
        ] + cast_in,
        out_specs=[pl.BlockSpec((bm, bn), lambda i, j: (i, j))] + cast_out,
        out_shape=[jax.ShapeDtypeStruct((t, n), F32)] + cast_shapes,
        scratch_shapes=[pltpu.VMEM((bm, d), BF16)],
        compiler_params=_cparams(("parallel", "arbitrary"), VMEM_LIMIT_IN_PROJ),
        name="in_proj",
    )(x, norm_w.reshape(1, d), w, *cast_arrays)
    return outs[0], tuple(outs[1:])


def _gmlp_kernel(zu_ref, zv_ref, ws_ref, bs_ref, lnw_ref, lnb_ref, ow_ref, y_ref, wc_scr, *, n_heads, n_chunks):
    c = GMLP_CHUNK
    row = lax.broadcasted_iota(jnp.int32, (c, c), 0)
    col = lax.broadcasted_iota(jnp.int32, (c, c), 1)
    causal = row >= col
    ones = jnp.ones((GMLP_HEAD, GMLP_HEAD), BF16)
    lane_mean = lambda x: jnp.dot(x.astype(BF16), ones, preferred_element_type=F32) * (1.0 / GMLP_HEAD)

    for h in range(n_heads):
        wc_scr[h] = jnp.where(causal, ws_ref[h], 0.0).astype(BF16)

    def chunk(ci, carry):
        r0 = pl.multiple_of(ci * c, c)
        for g0 in range(0, n_heads, GMLP_GROUP):
            heads = range(g0, g0 + GMLP_GROUP)
            sls = [slice(h * GMLP_HEAD, (h + 1) * GMLP_HEAD) for h in heads]
            v = [_gelu_tanh(zv_ref[pl.ds(r0, c), sl]) for sl in sls]
            mu = [lane_mean(x) for x in v]
            vc = [x - m for x, m in zip(v, mu)]
            var = [lane_mean(x * x) for x in vc]
            vn = [(x * lax.rsqrt(s + EPS) * lnw_ref[:, sl] + lnb_ref[:, sl]).astype(BF16)
                  for x, s, sl in zip(vc, var, sls)]
            sv = [jnp.dot(wc_scr[h], x, preferred_element_type=F32) + bs_ref[:, h:h + 1] for h, x in zip(heads, vn)]
            y = [_gelu_tanh(zu_ref[pl.ds(r0, c), sl]) * x for sl, x in zip(sls, sv)]
            ms = [lane_mean(x * x) for x in y]
            for x, s, sl in zip(y, ms, sls):
                y_ref[pl.ds(r0, c), sl] = (x * lax.rsqrt(s + EPS) * ow_ref[:, sl]).astype(BF16)
        return carry

    lax.fori_loop(0, n_chunks, chunk, 0)


def gmlp_mixer(z, w_s, b_s, ln_w, ln_b, out_w, rows=1024):
    t = z.shape[0]
    n_heads = w_s.shape[0]
    dg = n_heads * GMLP_HEAD
    rows = min(rows, t)
    kern = functools.partial(_gmlp_kernel, n_heads=n_heads, n_chunks=rows // GMLP_CHUNK)
    vec = lambda a: a.reshape(1, dg)
    return pl.pallas_call(
        kern,
        grid=(t // rows,),
        in_specs=[
            pl.BlockSpec((rows, dg), lambda i: (i, 0)),
            pl.BlockSpec((rows, dg), lambda i: (i, 1)),
            pl.BlockSpec((n_heads, GMLP_CHUNK, GMLP_CHUNK), lambda i: (0, 0, 0)),
            pl.BlockSpec((GMLP_CHUNK, n_heads), lambda i: (0, 0)),
            pl.BlockSpec((1, dg), lambda i: (0, 0)),
            pl.BlockSpec((1, dg), lambda i: (0, 0)),
            pl.BlockSpec((1, dg), lambda i: (0, 0)),
        ],
        out_specs=pl.BlockSpec((rows, dg), lambda i: (i, 0)),
        out_shape=jax.ShapeDtypeStruct((t, dg), BF16),
        scratch_shapes=[pltpu.VMEM((n_heads, GMLP_CHUNK, GMLP_CHUNK), BF16)],
        compiler_params=_cparams(("parallel",)),
        name="gmlp_mixer",
    )(z, z, w_s, b_s.T, vec(ln_w), vec(ln_b), vec(out_w))


def _hgrn_level_terms(a, q, k, blk):
    c, d = a.shape
    half = blk // 2
    nb = c // blk
    lower = lambda v: v.reshape(nb, blk, d)[:, :half, :]
    upper = lambda v: v.reshape(nb, blk, d)[:, half:, :]
    ref = a.reshape(nb, blk, d)[:, half - 1:half, :]
    qh = upper(q) * jnp.exp2(upper(a) - ref)
    kh = lower(k) * jnp.exp2(ref - lower(a))
    zeros = jnp.zeros_like(qh)
    qh = jnp.concatenate([zeros, qh], axis=1).reshape(c, d)
    kh = jnp.concatenate([kh, zeros], axis=1).reshape(c, d)
    return qh.astype(BF16), kh.astype(BF16)


def _hgrn_kernel(zq_ref, zf_ref, zi_ref, zg_ref, lb_ref, ow_ref, y_ref, st_ref, *, n_heads, n_chunks):
    c, d = HGRN_C, HGRN_HEAD
    nt = (((1,), (1,)), ((), ()))
    tn = (((0,), (0,)), ((), ()))

    @pl.when(pl.program_id(1) == 0)
    def _():
        st_ref[...] = jnp.zeros_like(st_ref)

    row = lax.broadcasted_iota(jnp.int32, (c, c), 0)
    col = lax.broadcasted_iota(jnp.int32, (c, c), 1)
    tril = (row >= col).astype(BF16)
    diag_mask = (row >= col) & ((row // HGRN_DIAG) == (col // HGRN_DIAG))
    level_masks = []
    blk = 2 * HGRN_DIAG
    while blk <= c:
        level_masks.append((blk, None if blk == c else (row // blk) == (col // blk)))
        blk *= 2

    def nt_dot(x, y):
        return lax.dot_general(x, y, nt, preferred_element_type=F32)

    def chunk(ci, carry):
        r0 = pl.multiple_of(ci * c, c)
        for g0 in range(0, n_heads, HGRN_GROUP):
            heads = list(range(g0, g0 + HGRN_GROUP))
            sls = [slice(h * d, (h + 1) * d) for h in heads]
            f = [0.5 * (1.0 + lb_ref[:, sl]) + (0.5 * (1.0 - lb_ref[:, sl])) * jnp.tanh(0.5 * zf_ref[pl.ds(r0, c), sl])
                 for sl in sls]
            logf = [jnp.log2(jnp.clip(x, F_MIN, 1.0)) for x in f]
            pieces = [jnp.concatenate(_split_bf16(x), axis=1) for x in logf]
            cs = [jnp.dot(tril, x, preferred_element_type=F32) for x in pieces]
            a = [x[:, :d] + x[:, d:] for x in cs]
            k = [1.0 - x for x in f]
            q = [_silu(zq_ref[pl.ds(r0, c), sl]) for sl in sls]
            iv = [zi_ref[pl.ds(r0, c), sl].astype(BF16) for sl in sls]

            dd = []
            for x in a:
                a3 = x.reshape(c // HGRN_DIAG, HGRN_DIAG, d)
                dd.append((a3 - a3[:, HGRN_DIAG_REF:HGRN_DIAG_REF + 1, :]).reshape(c, d))
            sd = [nt_dot((qx * jnp.exp2(x)).astype(BF16), (kx * jnp.exp2(-x)).astype(BF16))
                  for qx, kx, x in zip(q, k, dd)]

            scores = None
            for blk, same_block in reversed(level_masks):
                terms = [_hgrn_level_terms(ax, qx, kx, blk) for ax, qx, kx in zip(a, q, k)]
                prods = [nt_dot(qh, kh) for qh, kh in terms]
                scores = prods if scores is None else [jnp.where(same_block, p, s) for s, p in zip(scores, prods)]
            scores = [jnp.where(diag_mask, x, s) for x, s in zip(sd, scores)]

            st = [st_ref[h] for h in heads]
            qe = [(qx * jnp.exp2(ax)).astype(BF16) for qx, ax in zip(q, a)]
            o = [jnp.dot(s.astype(BF16), ivx, preferred_element_type=F32) + nt_dot(qx, sx.astype(BF16))
                 for s, ivx, qx, sx in zip(scores, iv, qe, st)]
            kd = [(kx * jnp.exp2(ax[c - 1:c, :] - ax)).astype(BF16) for kx, ax in zip(k, a)]
            for h, ax, sx, ivx, kx in zip(heads, a, st, iv, kd):
                st_ref[h] = jnp.exp2(ax[c - 1:c, :]) * sx + lax.dot_general(ivx, kx, tn, preferred_element_type=F32)
            for x, sl in zip(o, sls):
                y_ref[pl.ds(r0, c), sl] = (_rms(x, ow_ref[:, sl]) * _silu(zg_ref[pl.ds(r0, c), sl])).astype(BF16)
        return carry

    lax.fori_loop(0, n_chunks, chunk, 0)


def hgrn2_mixer(z, lb, out_w, batch, col0, rows=1024):
    t = z.shape[0]
    s = t // batch
    dh = lb.shape[0]
    n_heads = dh // HGRN_HEAD
    rows = min(rows, s)
    spb = s // rows
    kern = functools.partial(_hgrn_kernel, n_heads=n_heads, n_chunks=rows // HGRN_C)
    zspec = lambda j: pl.BlockSpec((rows, dh), lambda b, i, j=j: (b * spb + i, col0 + j))
    return pl.pallas_call(
        kern,
        grid=(batch, spb),
        in_specs=[zspec(0), zspec(1), zspec(2), zspec(3),
                  pl.BlockSpec((1, dh), lambda b, i: (0, 0)),
                  pl.BlockSpec((1, dh), lambda b, i: (0, 0))],
        out_specs=pl.BlockSpec((rows, dh), lambda b, i: (b * spb + i, 0)),
        out_shape=jax.ShapeDtypeStruct((t, dh), BF16),
        scratch_shapes=[pltpu.VMEM((n_heads, HGRN_HEAD, HGRN_HEAD), F32)],
        compiler_params=_cparams(("parallel", "arbitrary")),
        name="hgrn2_mixer",
    )(z, z, z, z, lb.reshape(1, dh), out_w.reshape(1, dh))


ROUTE_E0, ROUTE_E1, ROUTE_W0, ROUTE_W1, ROUTE_R0, ROUTE_R1 = range(6)
ROUTE_FIELDS = 8


def _split_bf16(v):
    hi = v.astype(BF16)
    return hi, (v - hi.astype(F32)).astype(BF16)


def _route_rows(h, rw_pieces, carry_ref):
    bm = h.shape[0]
    h_hi, h_lo = _split_bf16(h)
    hh_hl = jnp.dot(h_hi, rw_pieces, preferred_element_type=F32)
    logits = hh_hl[:, :LANES] + hh_hl[:, LANES:] + jnp.dot(h_lo, rw_pieces[:, :LANES], preferred_element_type=F32)
    lane = lax.broadcasted_iota(jnp.int32, (bm, LANES), 1)
    neg = jnp.float32(-jnp.inf)
    logits = jnp.where(lane < N_EXPERTS, logits, neg)
    v0 = jnp.max(logits, axis=-1, keepdims=True)
    e0 = jnp.min(jnp.where(logits == v0, lane, LANES), axis=-1, keepdims=True)
    rest = jnp.where(lane == e0, neg, logits)
    v1 = jnp.max(rest, axis=-1, keepdims=True)
    e1 = jnp.min(jnp.where(rest == v1, lane, LANES), axis=-1, keepdims=True)
    ex = jnp.exp(v1 - v0)
    w0 = 1.0 / (1.0 + ex)
    w1 = ex / (1.0 + ex)

    onehot = ((lane == e0) | (lane == e1)).astype(F32)
    row = lax.broadcasted_iota(jnp.int32, (bm, bm), 0)
    col = lax.broadcasted_iota(jnp.int32, (bm, bm), 1)
    before = (row > col).astype(BF16)
    excl = jnp.dot(before, onehot.astype(BF16), preferred_element_type=F32) + carry_ref[...]
    r0 = jnp.sum(jnp.where(lane == e0, excl, 0.0), axis=-1, keepdims=True)
    r1 = jnp.sum(jnp.where(lane == e1, excl, 0.0), axis=-1, keepdims=True)
    carry_ref[...] += jnp.sum(onehot, axis=0, keepdims=True)

    out = jnp.zeros((bm, LANES), F32)
    for idx, val in ((ROUTE_E0, e0.astype(F32)), (ROUTE_E1, e1.astype(F32)), (ROUTE_W0, w0), (ROUTE_W1, w1),
                     (ROUTE_R0, r0), (ROUTE_R1, r1)):
        out = jnp.where(lane == idx, val, out)
    return out


def _outproj_kernel(*refs, da, moe):
    if moe:
        (x_ref, ya_ref, yb_ref, wo_ref, nw_ref, rw_ref, xo_ref, h_ref, route_ref, route_t_ref, cnt_ref,
         wo_scr, rw_scr, carry_scr) = refs
    else:
        x_ref, ya_ref, yb_ref, wo_ref, nw_ref, xo_ref, h_ref, wo_scr = refs

    @pl.when(pl.program_id(0) == 0)
    def _():
        wo_scr[...] = wo_ref[...].astype(BF16)
        if moe:
            rw_hi, rw_lo = _split_bf16(rw_ref[...])
            rw_scr[:, :LANES] = rw_hi
            rw_scr[:, LANES:] = rw_lo
            carry_scr[...] = jnp.zeros_like(carry_scr)

    acc = x_ref[...] + jnp.dot(ya_ref[...], wo_scr[:da, :], preferred_element_type=F32) \
        + jnp.dot(yb_ref[...], wo_scr[da:, :], preferred_element_type=F32)
    xo_ref[...] = acc
    h = _rms(acc, nw_ref[...])
    h_ref[...] = h.astype(h_ref.dtype)
    if moe:
        route = _route_rows(h, rw_scr[...], carry_scr)
        route_ref[...] = route
        route_t_ref[...] = route.T[:route_t_ref.shape[0], :]
        cnt_ref[...] = carry_scr[...]


def out_proj(x, y_a, y_b, w_o, layer, norm_w, router_w=None, bm=None):
    t, d = x.shape
    da, db = y_a.shape[1], y_b.shape[1]
    moe = router_w is not None
    bm = min(bm or (256 if moe else 512), t)
    row_spec = lambda w: pl.BlockSpec((bm, w), lambda i: (i, 0))
    const = lambda shape: pl.BlockSpec(shape, lambda i: (0,) * len(shape), pipeline_mode=pl.Buffered(1))
    w_spec = pl.BlockSpec((None, da + db, d), lambda i: (layer, 0, 0), pipeline_mode=pl.Buffered(1))
    in_specs = [row_spec(d), row_spec(da), row_spec(db), w_spec, const((1, d))]
    out_specs = [row_spec(d), row_spec(d)]
    out_shape = [jax.ShapeDtypeStruct((t, d), F32), jax.ShapeDtypeStruct((t, d), F32 if moe else BF16)]
    scratch = [pltpu.VMEM((da + db, d), BF16)]
    args = [x, y_a, y_b, w_o, norm_w.reshape(1, d)]
    if moe:
        in_specs.append(const((d, LANES)))
        args.append(jnp.zeros((d, LANES), F32).at[:, :N_EXPERTS].set(router_w))
        out_specs += [row_spec(LANES), pl.BlockSpec((ROUTE_FIELDS, bm), lambda i: (0, i)),
                      pl.BlockSpec((1, LANES), lambda i: (0, 0))]
        out_shape += [jax.ShapeDtypeStruct((t, LANES), F32), jax.ShapeDtypeStruct((ROUTE_FIELDS, t), F32),
                      jax.ShapeDtypeStruct((1, LANES), F32)]
        scratch += [pltpu.VMEM((d, 2 * LANES), BF16), pltpu.VMEM((1, LANES), F32)]
    return pl.pallas_call(
        functools.partial(_outproj_kernel, da=da, moe=moe),
        grid=(t // bm,),
        in_specs=in_specs,
        out_specs=out_specs,
        out_shape=out_shape,
        scratch_shapes=scratch,
        compiler_params=_cparams(("arbitrary",)),
        name="out_proj_route" if moe else "out_proj",
    )(*args)


def _ffn_kernel(x_hbm, h_ref, wg_ref, wu_ref, wd_ref, *rest, n_pass):
    n_cast = (len(rest) - 2) // 2
    o_ref, sem = rest[n_cast], rest[-1]
    i, j = pl.program_id(0), pl.program_id(1)
    bm = o_ref.shape[0]
    sub = bm // n_pass

    def x_copy(p):
        src = x_hbm.at[pl.ds(pl.multiple_of(i * bm + p * sub, sub), sub)]
        return pltpu.make_async_copy(src, o_ref.at[pl.ds(p * sub, sub)], sem.at[p])

    @pl.when(j == 0)
    def _():
        for p in range(n_pass):
            x_copy(p).start()
        for p in range(n_pass):
            x_copy(p).wait()

    _narrow_slabs(rest[:n_cast], rest[n_cast + 1:-1])

    for p in range(n_pass):
        hb = h_ref[p * sub:(p + 1) * sub, :]
        g = jnp.dot(hb, wg_ref[...], preferred_element_type=F32)
        u = jnp.dot(hb, wu_ref[...], preferred_element_type=F32)
        act = (_silu(g) * u).astype(BF16)
        o_ref[p * sub:(p + 1) * sub, :] += jnp.dot(act, wd_ref[...], preferred_element_type=F32)


def dense_ffn(x, h, wg, wu, wd, cast=(), bm=1024, bf=512, n_pass=2):
    t, d = x.shape
    f = wg.shape[1]
    bm = min(bm, t)
    ni, nj = t // bm, f // bf
    cast_arrays, cast_in, cast_out, cast_shapes = _cast_block_specs(cast, ni, nj)
    outs = pl.pallas_call(
        functools.partial(_ffn_kernel, n_pass=n_pass),
        grid=(ni, nj),
        in_specs=[
            pl.BlockSpec(memory_space=pl.ANY),
            pl.BlockSpec((bm, d), lambda i, j: (i, 0)),
            pl.BlockSpec((d, bf), lambda i, j: (0, j)),
            pl.BlockSpec((d, bf), lambda i, j: (0, j)),
            pl.BlockSpec((bf, d), lambda i, j: (j, 0)),
        ] + cast_in,
        out_specs=[pl.BlockSpec((bm, d), lambda i, j: (i, 0))] + cast_out,
        out_shape=[jax.ShapeDtypeStruct((t, d), F32)] + cast_shapes,
        scratch_shapes=[pltpu.SemaphoreType.DMA((n_pass,))],
        compiler_params=_cparams(("parallel", "arbitrary"), VMEM_LIMIT_IN_PROJ),
        name="dense_ffn",
    )(x, h, wg, wu, wd, *cast_arrays)
    return outs[0], tuple(outs[1:])


def _row_copy(src_ref, dst_ref, sem, src_row, dst_row):
    return pltpu.make_async_copy(src_ref.at[pl.ds(src_row, 1)], dst_ref.at[pl.ds(dst_row, 1)], sem)


ROW_DMA_UNROLL = 8
GMM_TILE = 1024
GMM_SUBTILES = 2


def _dispatch_kernel(zlo_ref, zhi_ref, pos_ref, h_ref, xs_ref, zero_scr, sem, zsem):
    bt = h_ref.shape[0]

    @pl.when(pl.program_id(0) == 0)
    def _():
        zero_scr[...] = jnp.zeros_like(zero_scr)
        zr = zero_scr.shape[0]

        def zero_rows(e, begin):
            lo, hi = zlo_ref[e], zhi_ref[e]
            mid = jnp.minimum(hi, (lo + zr - 1) // zr * zr)

            def one(r, carry):
                cp = _row_copy(zero_scr, xs_ref, zsem, 0, r)
                cp.start() if begin else cp.wait()
                return carry

            def slab(b, carry):
                cp = pltpu.make_async_copy(zero_scr, xs_ref.at[pl.ds(pl.multiple_of(b * zr, zr), zr)], zsem)
                cp.start() if begin else cp.wait()
                return carry

            lax.fori_loop(lo, mid, one, 0)
            lax.fori_loop(mid // zr, hi // zr, slab, 0)

        for e in range(N_EXPERTS):
            zero_rows(e, True)
        for e in range(N_EXPERTS):
            zero_rows(e, False)

    def start(r, carry):
        for kk in range(TOP_K):
            _row_copy(h_ref, xs_ref, sem, r, pos_ref[0, kk * bt + r]).start(priority=kk % 2)
        return carry

    lax.fori_loop(0, bt, start, 0, unroll=ROW_DMA_UNROLL)
    for kk in range(TOP_K):
        pltpu.make_async_copy(h_ref, xs_ref.at[pl.ds(0, bt)], sem).wait()


def _tile_positions(pos, bt):
    t = pos.shape[1]
    return pos.reshape(TOP_K, t // bt, bt).transpose(1, 0, 2).reshape(t // bt, 1, TOP_K * bt)


def moe_dispatch(h, pos, zero_lo, zero_hi, n_rows, bt=512):
    t, d = h.shape
    bt = min(bt, t)
    pos2 = _tile_positions(pos, bt)
    return pl.pallas_call(
        _dispatch_kernel,
        grid_spec=pltpu.PrefetchScalarGridSpec(
            num_scalar_prefetch=2,
            grid=(t // bt,),
            in_specs=[
                pl.BlockSpec((None, 1, TOP_K * bt), lambda i, zl, zh: (i, 0, 0), memory_space=pltpu.SMEM),
                pl.BlockSpec((bt, d), lambda i, zl, zh: (i, 0)),
            ],
            out_specs=pl.BlockSpec(memory_space=pl.ANY),
            scratch_shapes=[pltpu.VMEM((8, d), F32), pltpu.SemaphoreType.DMA(()), pltpu.SemaphoreType.DMA(())],
        ),
        out_shape=jax.ShapeDtypeStruct((n_rows, d), F32),
        compiler_params=_cparams(("arbitrary",)),
        name="moe_dispatch",
    )(zero_lo, zero_hi, pos2, h)


def _gmm_kernel(te_ref, tr_ref, nv_ref, xs_ref, wg_ref, wu_ref, wd_ref, ys_ref, xb_scr):
    i, j = pl.program_id(0), pl.program_id(1)
    bm = xs_ref.shape[0]
    rows = tr_ref[i]
    sub = bm // GMM_SUBTILES

    @pl.when(j == 0)
    def _():
        xb_scr[...] = xs_ref[...].astype(BF16)
        ys_ref[...] = jnp.zeros_like(ys_ref)

    def ffn(r0, n):
        xb = xb_scr[r0:r0 + n, :]
        g = jnp.dot(xb, wg_ref[...], preferred_element_type=F32)
        u = jnp.dot(xb, wu_ref[...], preferred_element_type=F32)
        act = (_silu(g) * u).astype(BF16)
        ys_ref[r0:r0 + n, :] += jnp.dot(act, wd_ref[...], preferred_element_type=F32)

    full = rows == bm

    @pl.when(full)
    def _():
        for s in range(GMM_SUBTILES):
            ffn(s * sub, sub)

    for s in range(GMM_SUBTILES):
        left = rows - s * sub

        @pl.when(jnp.logical_not(full) & (left > sub // 2))
        def _():
            ffn(s * sub, sub)

        @pl.when((left > 0) & (left <= sub // 2))
        def _():
            ffn(s * sub, sub // 2)


def moe_grouped_ffn(xs, tile_expert, tile_rows, n_valid, wg, wu, wd, bm, bf=512):
    p, d = xs.shape
    f = wg.shape[2]
    nf = f // bf

    def row_map(i, j, te, tr, nv):
        return (jnp.minimum(i, nv[0] - 1), 0)

    def fcol(i, j, nv):
        return jnp.where(i < nv[0], j, nf - 1)

    return pl.pallas_call(
        _gmm_kernel,
        grid_spec=pltpu.PrefetchScalarGridSpec(
            num_scalar_prefetch=3,
            grid=(p // bm, nf),
            in_specs=[
                pl.BlockSpec((bm, d), row_map),
                pl.BlockSpec((None, d, bf), lambda i, j, te, tr, nv: (te[i], 0, fcol(i, j, nv))),
                pl.BlockSpec((None, d, bf), lambda i, j, te, tr, nv: (te[i], 0, fcol(i, j, nv))),
                pl.BlockSpec((None, bf, d), lambda i, j, te, tr, nv: (te[i], fcol(i, j, nv), 0)),
            ],
            out_specs=pl.BlockSpec((bm, d), lambda i, j, te, tr, nv: (i, 0)),
            scratch_shapes=[pltpu.VMEM((bm, d), BF16)],
        ),
        out_shape=jax.ShapeDtypeStruct((p, d), F32),
        compiler_params=_cparams(("arbitrary", "arbitrary")),
        name="moe_grouped_ffn",
    )(tile_expert, tile_rows, n_valid, xs, wg, wu, wd)


def _combine_kernel(pos_ref, pos_next_ref, x_ref, route_ref, nw_ref, ys_ref, o_ref, y_scr, sem, *, final_norm):
    bt = x_ref.shape[0]
    i = pl.program_id(0)
    slot = i % 2

    def gather(p_ref, s, unroll):
        def start(r, carry):
            for kk in range(TOP_K):
                _row_copy(ys_ref, y_scr.at[s, kk], sem.at[s], p_ref[0, kk * bt + r], r).start(priority=kk % 2)
            return carry
        lax.fori_loop(0, bt, start, 0, unroll=unroll)

    def wait(s):
        for kk in range(TOP_K):
            pltpu.make_async_copy(ys_ref.at[pl.ds(0, bt)], y_scr.at[s, kk], sem.at[s]).wait()

    @pl.when(i == 0)
    def _():
        gather(pos_ref, 0, ROW_DMA_UNROLL)

    wait(slot)
    gather(pos_next_ref, 1 - slot, True)
    w0 = route_ref[:, ROUTE_W0:ROUTE_W0 + 1]
    w1 = route_ref[:, ROUTE_W1:ROUTE_W1 + 1]
    out = x_ref[...] + (w0 * y_scr[slot, 0] + w1 * y_scr[slot, 1])
    if final_norm:
        out = _rms(out, nw_ref[...])
    o_ref[...] = out

    @pl.when(i + 1 == pl.num_programs(0))
    def _():
        wait(1 - slot)


def moe_combine(x, route, pos, ys, norm_w, final_norm, bt=256):
    t, d = x.shape
    bt = min(bt, t)
    n = t // bt
    pos2 = _tile_positions(pos, bt)
    pos_spec = lambda nxt: pl.BlockSpec((None, 1, TOP_K * bt), lambda i: (jnp.minimum(i + nxt, n - 1), 0, 0),
                                        memory_space=pltpu.SMEM)
    return pl.pallas_call(
        functools.partial(_combine_kernel, final_norm=final_norm),
        grid=(n,),
        in_specs=[
            pos_spec(0),
            pos_spec(1),
            pl.BlockSpec((bt, d), lambda i: (i, 0)),
            pl.BlockSpec((bt, LANES), lambda i: (i, 0)),
            pl.BlockSpec((1, d), lambda i: (0, 0)),
            pl.BlockSpec(memory_space=pl.ANY),
        ],
        out_specs=pl.BlockSpec((bt, d), lambda i: (i, 0)),
        out_shape=jax.ShapeDtypeStruct((t, d), F32),
        scratch_shapes=[pltpu.VMEM((2, TOP_K, bt, d), F32), pltpu.SemaphoreType.DMA((2,))],
        compiler_params=_cparams(("arbitrary",)),
        name="moe_combine",
    )(pos2, pos2, x, route, norm_w.reshape(1, d), ys)


def moe_ffn(x, h, route, route_t, counts, wg, wu, wd, norm_w, final_norm, bm=GMM_TILE):
    t, d = x.shape

    cnt = counts[0, :N_EXPERTS].astype(jnp.int32)
    tiles = (cnt + bm - 1) // bm
    tile_end = jnp.cumsum(tiles)
    offset = (tile_end - tiles) * bm
    n_tiles = (TOP_K * t) // bm + N_EXPERTS
    n_valid = tile_end[-1:].astype(jnp.int32)
    tile_id = jnp.minimum(jnp.arange(n_tiles, dtype=jnp.int32), n_valid[0] - 1)
    tile_expert = jnp.sum((tile_end[None, :] <= tile_id[:, None]).astype(jnp.int32), axis=1)
    tile_in_group = tile_id - (tile_end - tiles)[tile_expert]
    tile_rows = jnp.clip(cnt[tile_expert] - tile_in_group * bm, 0, bm)
    tile_rows = jnp.where(jnp.arange(n_tiles) < n_valid[0], tile_rows, 0).astype(jnp.int32)
    fields = route_t.astype(jnp.int32)
    pos = jnp.stack([offset[fields[ROUTE_E0 + kk]] + fields[ROUTE_R0 + kk] for kk in range(TOP_K)])
    zero_lo = offset + cnt
    zero_hi = jnp.concatenate([offset[1:], jnp.full((1,), n_tiles * bm, jnp.int32)])

    xs = moe_dispatch(h, pos, zero_lo, zero_hi, n_tiles * bm)
    ys = moe_grouped_ffn(xs, tile_expert, tile_rows, n_valid, wg, wu, wd, bm)
    return moe_combine(x, route, pos, ys, norm_w, final_norm)


def _norm_kernel(x_ref, nw_ref, o_ref):
    o_ref[...] = _rms(x_ref[...], nw_ref[...])


def final_norm(x, norm_w, bm=512):
    t, d = x.shape
    bm = min(bm, t)
    return pl.pallas_call(
        _norm_kernel,
        grid=(t // bm,),
        in_specs=[pl.BlockSpec((bm, d), lambda i: (i, 0)), pl.BlockSpec((1, d), lambda i: (0, 0))],
        out_specs=pl.BlockSpec((bm, d), lambda i: (i, 0)),
        out_shape=jax.ShapeDtypeStruct((t, d), F32),
        compiler_params=_cparams(("parallel",)),
        name="final_norm",
    )(x, norm_w.reshape(1, d))


def kernel(x, mix_norm_w, w_in, gmlp_w_s, gmlp_b_s, gmlp_ln_w, gmlp_ln_b, gmlp_out_w, hgrn_lb_logits, hgrn_out_w, w_o, ffn_norm_w, dense_w_gate, dense_w_up, dense_w_down, router_w, expert_w_gate, expert_w_up, expert_w_down, final_norm_w):
    batch, seq, d = x.shape
    depth = w_in.shape[0]
    d_gmlp = gmlp_ln_w.shape[1]
    d_hgrn = hgrn_out_w.shape[1]
    assert d_gmlp == d_hgrn and seq % HGRN_C == 0 and seq % GMLP_CHUNK == 0

    lbs = jax.nn.softmax(hgrn_lb_logits.astype(F32), axis=0)
    lbs = jnp.cumsum(lbs, axis=0) - lbs[0:1]

    xt = x.reshape(batch * seq, d)
    w_in_bf16 = w_in[0].astype(BF16)
    for l in range(depth):
        last = l == depth - 1
        j = l // 2
        soon = [] if last else [(w_in.reshape(depth * d, -1), (l + 1) * d, d)]
        if l % 2 == 0:
            soon += [dense_w_gate[j], dense_w_up[j], dense_w_down[j]]
        z, narrowed = in_proj(xt, mix_norm_w[l], w_in_bf16, cast=tuple(soon))
        narrowed = list(narrowed)
        if not last:
            w_in_bf16 = narrowed.pop(0)
        y_a = gmlp_mixer(z, gmlp_w_s[l], gmlp_b_s[l], gmlp_ln_w[l], gmlp_ln_b[l], gmlp_out_w[l])
        y_b = hgrn2_mixer(z, lbs[l], hgrn_out_w[l], batch, col0=2 * d_gmlp // d_hgrn)
        if l % 2 == 0:
            xt, h = out_proj(xt, y_a, y_b, w_o, l, ffn_norm_w[l])
            experts = () if last else tuple(w[j].reshape(-1, w.shape[-1])
                                            for w in (expert_w_gate, expert_w_up, expert_w_down))
            xt, experts_bf16 = dense_ffn(xt, h, *narrowed, cast=experts)
            if last:
                xt = final_norm(xt, final_norm_w)
        else:
            xt, h, route, route_t, counts = out_proj(xt, y_a, y_b, w_o, l, ffn_norm_w[l], router_w=router_w[j])
            wg, wu, wd = (w.reshape(s.shape[1:]) for w, s in
                          zip(experts_bf16, (expert_w_gate, expert_w_up, expert_w_down)))
            xt = moe_ffn(xt, h, route, route_t, counts, wg, wu, wd, final_norm_w, final_norm=last)
    return xt.reshape(batch, seq, d)
```
